```python
import jax, jax.numpy as jnp
from jax import lax
import numpy as np

D_MODEL = 1024
BATCH = 4
SEQ = 4096
DEPTH = 2
DEC_BATCH = 32
DEC_SEQ = 4
PAST_LEN = 16384
PAGE_SIZE = 128

HEAD_DIM = 64
POOL_WIDTH = 256
POOL_WINDOWS = (2, 4, 8, 16)
POOL_GROUPS = 4
POOL_GROUP_DIM = POOL_WIDTH // POOL_GROUPS
POOL_STATE = 15
NSA_HEADS = 8
NSA_KV_HEADS = 2
NSA_REP = NSA_HEADS // NSA_KV_HEADS
NSA_WIDTH = NSA_HEADS * HEAD_DIM
NSA_KV_WIDTH = NSA_KV_HEADS * HEAD_DIM
NSA_N_KV = 6
NSA_N_CACHED = 4
CMP_BLOCK = 32
SEL_BLOCK = 64
SEL_TOPK = 16
WINDOW = 512
FORCE_SCORE = 1e4
FOX_HEADS = 4
FOX_WIDTH = FOX_HEADS * HEAD_DIM
FORGET_BIAS = 4.0
N_BRANCH = 3
ROPE_THETA = 10000.0
Q_BLOCK = 128
RMS_EPS = 1e-6
NEG_INF = -1e30
IN_SPLITS = (POOL_WIDTH, POOL_WIDTH,
             NSA_WIDTH, NSA_N_KV * NSA_KV_WIDTH, 3 * NSA_HEADS, NSA_WIDTH,
             FOX_WIDTH, FOX_WIDTH, FOX_WIDTH, FOX_HEADS, FOX_WIDTH,
             N_BRANCH * D_MODEL)
D_IN = 6428

kernel_name = 'hybrid_pool_nsa_fox_decoder_step'


def _rmsnorm(x, w):
    x32 = x.astype(jnp.float32)
    y = x32 * lax.rsqrt(jnp.mean(x32 * x32, axis=-1, keepdims=True) + RMS_EPS)
    return y.astype(x.dtype) * w


def _rope(x, pos):
    half = HEAD_DIM // 2
    inv = ROPE_THETA ** (-jnp.arange(half, dtype=jnp.float32) / half)
    ang = pos.astype(jnp.float32)[:, None] * inv[None, :]
    cos = jnp.cos(ang)[None, :, None, :]
    sin = jnp.sin(ang)[None, :, None, :]
    x32 = x.astype(jnp.float32)
    x1, x2 = x32[..., :half], x32[..., half:]
    return jnp.concatenate([x1 * cos - x2 * sin, x2 * cos + x1 * sin], -1).astype(x.dtype)


def _masked_softmax(s, mask):
    s = jnp.where(mask, s, NEG_INF)
    m = jnp.max(s, axis=-1, keepdims=True)
    e = jnp.where(mask, jnp.exp(s - m), 0.0)
    return e / jnp.maximum(jnp.sum(e, axis=-1, keepdims=True), 1e-30)


def _split_in(proj):
    parts, off = [], 0
    for n in IN_SPLITS:
        parts.append(proj[..., off:off + n])
        off += n
    return parts


def _pool_mix(seq, pos, pool_w, pool_scale):
    bsz, n, _ = seq.shape
    t = n - POOL_STATE
    s32 = seq.astype(jnp.float32)
    csum = jnp.concatenate([jnp.zeros((bsz, 1, POOL_WIDTH), jnp.float32), jnp.cumsum(s32, axis=1)], 1)
    end = csum[:, POOL_STATE + 1:]
    u = s32[:, POOL_STATE:]
    parts = []
    for g, w in enumerate(POOL_WINDOWS):
        c = slice(g * POOL_GROUP_DIM, (g + 1) * POOL_GROUP_DIM)
        win_sum = end[..., c] - csum[:, POOL_STATE + 1 - w:POOL_STATE + 1 - w + t, c]
        cnt = jnp.minimum(pos + 1, w).astype(jnp.float32)[None, :, None]
        parts.append(win_sum / cnt - u[..., c])
    d = jnp.stack(parts, 2).astype(seq.dtype)
    y = jnp.einsum('btgc,gcd->btgd', d, pool_w).reshape(bsz, t, POOL_WIDTH)
    return y * pool_scale


def _nsa_blocks(full):
    bsz, L = full.shape[:2]
    lp = -(-L // SEL_BLOCK) * SEL_BLOCK
    fp = jnp.pad(full, ((0, 0), (0, lp - L), (0, 0), (0, 0), (0, 0)))
    cmp = fp[:, :, 0:2].reshape(bsz, lp // CMP_BLOCK, CMP_BLOCK, 2, NSA_KV_HEADS, HEAD_DIM)
    cmp = jnp.mean(cmp, axis=2).astype(full.dtype)
    sel = fp[:, :, 2:4].reshape(bsz, lp // SEL_BLOCK, SEL_BLOCK, 2, NSA_KV_HEADS, HEAD_DIM)
    sel = sel.transpose(3, 0, 4, 1, 2, 5)
    return cmp[:, :, 0], cmp[:, :, 1], sel[0], sel[1]


def _nsa_attend(q, q_pos, kc, vc, ks, vs, kw, vw, kw_pos):
    bsz, tq = q.shape[:2]
    scale = HEAD_DIM ** -0.5
    qg = q.reshape(bsz, tq, NSA_KV_HEADS, NSA_REP, HEAD_DIM)
    nc = kc.shape[1]
    ns = ks.shape[2]
    sc = jnp.einsum('btgrd,bngd->bgrtn', qg, kc, preferred_element_type=jnp.float32) * scale
    c_ok = ((jnp.arange(nc) + 1) * CMP_BLOCK - 1)[None, :] <= q_pos[:, None]
    pc = _masked_softmax(sc, c_ok)
    o_cmp = jnp.einsum('bgrtn,bngd->btgrd', pc.astype(vc.dtype), vc)
    imp = jnp.sum(pc, axis=2).reshape(bsz, NSA_KV_HEADS, tq, ns, SEL_BLOCK // CMP_BLOCK).sum(-1)
    blk = jnp.arange(ns)[None, :]
    cur = (q_pos // SEL_BLOCK)[:, None]
    forced = (blk == 0) | (blk == cur) | (blk == cur - 1)
    imp = jnp.where(blk <= cur, imp + jnp.where(forced, FORCE_SCORE, 0.0), -1.0)
    k_top = min(SEL_TOPK, ns)
    _, idx = lax.top_k(imp, k_top)
    bi = jnp.arange(bsz)[:, None, None, None]
    gi = jnp.arange(NSA_KV_HEADS)[None, :, None, None]
    kg = ks[bi, gi, idx]
    vg = vs[bi, gi, idx]
    ss = jnp.einsum('btgrd,bgtkjd->bgrtkj', qg, kg, preferred_element_type=jnp.float32) * scale
    kpos = idx[..., None] * SEL_BLOCK + jnp.arange(SEL_BLOCK)
    s_ok = (kpos <= q_pos[None, None, :, None, None]).reshape(bsz, NSA_KV_HEADS, 1, tq, k_top * SEL_BLOCK)
    ps = _masked_softmax(ss.reshape(bsz, NSA_KV_HEADS, NSA_REP, tq, k_top * SEL_BLOCK), s_ok)
    o_sel = jnp.einsum('bgrtn,bgtnd->btgrd', ps.astype(vg.dtype),
                       vg.reshape(bsz, NSA_KV_HEADS, tq, k_top * SEL_BLOCK, HEAD_DIM))
    sw = jnp.einsum('btgrd,blgd->bgrtl', qg, kw, preferred_element_type=jnp.float32) * scale
    w_ok = ((kw_pos[None, :] <= q_pos[:, None]) & (kw_pos[None, :] > q_pos[:, None] - WINDOW)
            & (kw_pos[None, :] >= 0))
    pw = _masked_softmax(sw, w_ok)
    o_win = jnp.einsum('bgrtl,blgd->btgrd', pw.astype(vw.dtype), vw)
    return o_cmp, o_sel, o_win


def _nsa_prompt(q, full, win_rows):
    bsz, t = q.shape[:2]
    kc, vc, ks, vs = _nsa_blocks(full)
    win_pad = jnp.pad(win_rows, ((0, 0), (WINDOW, 0), (0, 0), (0, 0), (0, 0)))

    def one_block(i):
        s = i * Q_BLOCK
        qb = lax.dynamic_slice_in_dim(q, s, Q_BLOCK, 1)
        wb = lax.dynamic_slice_in_dim(win_pad, s, Q_BLOCK + WINDOW, 1)
        qp = s + jnp.arange(Q_BLOCK)
        kp = s - WINDOW + jnp.arange(Q_BLOCK + WINDOW)
        return _nsa_attend(qb, qp, kc, vc, ks, vs, wb[:, :, 0], wb[:, :, 1], kp)

    outs = lax.map(one_block, jnp.arange(t // Q_BLOCK))
    return [jnp.moveaxis(o, 0, 1).reshape((bsz, t) + o.shape[3:]) for o in outs]


def _fox_attend(q, c_q, k, v, c_k, q_pos):
    s = jnp.einsum('bthd,blhd->bhtl', q, k, preferred_element_type=jnp.float32) * (HEAD_DIM ** -0.5)
    s = s + jnp.swapaxes(c_q, 1, 2)[..., :, None] - jnp.swapaxes(c_k, 1, 2)[..., None, :]
    ok = jnp.arange(k.shape[1])[None, :] <= q_pos[:, None]
    p = _masked_softmax(s, ok)
    return jnp.einsum('bhtl,blhd->bthd', p.astype(v.dtype), v)


def _fox_prompt(q, k, v, csum):
    bsz, t = q.shape[:2]

    def one_block(i):
        s = i * Q_BLOCK
        qb = lax.dynamic_slice_in_dim(q, s, Q_BLOCK, 1)
        cb = lax.dynamic_slice_in_dim(csum, s, Q_BLOCK, 1)
        return _fox_attend(qb, cb, k, v, csum, s + jnp.arange(Q_BLOCK))

    o = lax.map(one_block, jnp.arange(t // Q_BLOCK))
    return jnp.moveaxis(o, 0, 1).reshape(q.shape)


def _gather_pages(cache, layer, page_table):
    g = cache[layer, page_table]
    return g.reshape((g.shape[0], g.shape[1] * g.shape[2]) + g.shape[3:])


def _layer(x, pos, past, norm_w, w_in, pool_w, pool_scale, fox_bf, w_out_a, w_out_b, w_out_c, w_o):
    bsz, t = x.shape[:2]
    h = _rmsnorm(x, norm_w)
    (u_a, z_a, q_b, kv_b, g_b, z_b, q_c, k_c, v_c, f_c, z_c, g_m) = _split_in(h @ w_in)

    if past is None:
        prev_pool = jnp.zeros((bsz, POOL_STATE, POOL_WIDTH), u_a.dtype)
    else:
        prev_pool = past[4].astype(u_a.dtype)
    pool_seq = jnp.concatenate([prev_pool, u_a], 1)
    o_a = _pool_mix(pool_seq, pos, pool_w, pool_scale)
    y_a = (o_a * jax.nn.silu(z_a)) @ w_out_a
    new_pool = pool_seq[:, -POOL_STATE:]

    q = _rope(q_b.reshape(bsz, t, NSA_HEADS, HEAD_DIM), pos)
    kv = kv_b.reshape(bsz, t, NSA_N_KV, NSA_KV_HEADS, HEAD_DIM)
    nsa_rows = jnp.stack([_rope(kv[:, :, 0], pos), kv[:, :, 1], _rope(kv[:, :, 2], pos), kv[:, :, 3]], 2)
    win_rows = jnp.stack([_rope(kv[:, :, 4], pos), kv[:, :, 5]], 2)
    if past is None:
        win_seq = win_rows
        o_cmp, o_sel, o_win = _nsa_prompt(q, nsa_rows, win_rows)
    else:
        full = jnp.concatenate([past[0].astype(nsa_rows.dtype), nsa_rows], 1)
        win_seq = jnp.concatenate([past[3].astype(win_rows.dtype), win_rows], 1)
        lw = win_seq.shape[1]
        kw_pos = pos[0] + t - lw + jnp.arange(lw)
        kc, vc, ks, vs = _nsa_blocks(full)
        o_cmp, o_sel, o_win = _nsa_attend(q, pos, kc, vc, ks, vs, win_seq[:, :, 0], win_seq[:, :, 1], kw_pos)
    new_win = win_seq[:, -min(WINDOW, win_seq.shape[1]):]
    gb = jax.nn.sigmoid(g_b).reshape(bsz, t, NSA_KV_HEADS, NSA_REP, 3)
    o_b = (gb[..., 0:1] * o_cmp + gb[..., 1:2] * o_sel + gb[..., 2:3] * o_win).reshape(bsz, t, NSA_WIDTH)
    y_b = (o_b * jax.nn.silu(z_b)) @ w_out_b

    qf = q_c.reshape(bsz, t, FOX_HEADS, HEAD_DIM)
    kf = k_c.reshape(bsz, t, FOX_HEADS, HEAD_DIM)
    vf = v_c.reshape(bsz, t, FOX_HEADS, HEAD_DIM)
    logf = jax.nn.log_sigmoid(f_c.astype(jnp.float32) + fox_bf.astype(jnp.float32))
    fox_rows = jnp.stack([kf, vf], 2)
    if past is None:
        csum = jnp.cumsum(logf, axis=1)
        o_c = _fox_prompt(qf, kf, vf, csum)
    else:
        k_all = jnp.concatenate([past[1][:, :, 0].astype(kf.dtype), kf], 1)
        v_all = jnp.concatenate([past[1][:, :, 1].astype(vf.dtype), vf], 1)
        csum = jnp.cumsum(jnp.concatenate([past[2].astype(jnp.float32), logf], 1), axis=1)
        o_c = _fox_attend(qf, csum[:, -t:], k_all, v_all, csum, pos)
    y_c = (o_c.reshape(bsz, t, FOX_WIDTH) * jax.nn.silu(z_c)) @ w_out_c

    gm = jax.nn.sigmoid(g_m).reshape(bsz, t, N_BRANCH, D_MODEL)
    mixed = gm[:, :, 0] * y_a + gm[:, :, 1] * y_b + gm[:, :, 2] * y_c
    return x + mixed @ w_o, (nsa_rows, fox_rows, logf, new_win, new_pool)


def setup_inputs(seed: int = 0) -> dict:
    key = jax.random.key(seed)
    k = jax.random.split(key, 20)
    f32 = jnp.float32
    n_pages = PAST_LEN // PAGE_SIZE
    n_phys = (DEC_BATCH * n_pages * 5) // 4
    win_len = min(WINDOW, PAST_LEN)

    def nrm(kk, shape, s=1.0):
        return s * jax.random.normal(kk, shape, f32)

    page_table = jax.random.permutation(k[0], n_phys)[:DEC_BATCH * n_pages]
    page_table = page_table.reshape(DEC_BATCH, n_pages).astype(jnp.int32)
    return {
        'x_prompt': nrm(k[1], (BATCH, SEQ, D_MODEL)),
        'x_sample': nrm(k[2], (DEC_BATCH, DEC_SEQ, D_MODEL)),
        'cache_nsa_kv': nrm(k[3], (DEPTH, n_phys, PAGE_SIZE, NSA_N_CACHED, NSA_KV_HEADS, HEAD_DIM)),
        'cache_fox_kv': nrm(k[4], (DEPTH, n_phys, PAGE_SIZE, 2, FOX_HEADS, HEAD_DIM)),
        'cache_fox_logf': jax.nn.log_sigmoid(FORGET_BIAS + nrm(k[5], (DEPTH, n_phys, PAGE_SIZE, FOX_HEADS))),
        'state_win_kv': nrm(k[6], (DEPTH, DEC_BATCH, win_len, 2, NSA_KV_HEADS, HEAD_DIM)),
        'state_pool': nrm(k[7], (DEPTH, DEC_BATCH, POOL_STATE, POOL_WIDTH)),
        'page_table': page_table,
        'norm_w': 1.0 + nrm(k[8], (DEPTH, D_MODEL), 0.1),
        'w_in': nrm(k[9], (DEPTH, D_MODEL, D_IN), D_MODEL ** -0.5),
        'pool_w': nrm(k[10], (DEPTH, POOL_GROUPS, POOL_GROUP_DIM, POOL_GROUP_DIM), POOL_GROUP_DIM ** -0.5),
        'pool_scale': 1.0 + nrm(k[11], (DEPTH, POOL_WIDTH), 0.1),
        'fox_bf': FORGET_BIAS + nrm(k[12], (DEPTH, FOX_HEADS), 0.1),
        'w_out_a': nrm(k[13], (DEPTH, POOL_WIDTH, D_MODEL), POOL_WIDTH ** -0.5),
        'w_out_b': nrm(k[14], (DEPTH, NSA_WIDTH, D_MODEL), NSA_WIDTH ** -0.5),
        'w_out_c': nrm(k[15], (DEPTH, FOX_WIDTH, D_MODEL), FOX_WIDTH ** -0.5),
        'w_o': nrm(k[16], (DEPTH, D_MODEL, D_MODEL), D_MODEL ** -0.5),
        'final_norm': 1.0 + nrm(k[17], (D_MODEL,), 0.1),
    }


def reference(x_prompt, x_sample, cache_nsa_kv, cache_fox_kv, cache_fox_logf, state_win_kv, state_pool,
              page_table, norm_w, w_in, pool_w, pool_scale, fox_bf, w_out_a, w_out_b, w_out_c, w_o,
              final_norm):
    past_len = page_table.shape[1] * cache_nsa_kv.shape[2]
    pos_p = jnp.arange(x_prompt.shape[1])
    pos_s = past_len + jnp.arange(x_sample.shape[1])
    hp, hs = x_prompt, x_sample
    st_p, st_s = [], []
    for layer in range(DEPTH):
        wts = (norm_w[layer], w_in[layer], pool_w[layer], pool_scale[layer], fox_bf[layer],
               w_out_a[layer], w_out_b[layer], w_out_c[layer], w_o[layer])
        hp, sp = _layer(hp, pos_p, None, *wts)
        past = (_gather_pages(cache_nsa_kv, layer, page_table),
                _gather_pages(cache_fox_kv, layer, page_table),
                _gather_pages(cache_fox_logf, layer, page_table),
                state_win_kv[layer], state_pool[layer])
        hs, ss = _layer(hs, pos_s, past, *wts)
        st_p.append(sp)
        st_s.append(ss)
    y_prompt = _rmsnorm(hp, final_norm)
    y_sample = _rmsnorm(hs, final_norm)
    p_nsa = jnp.stack([s[0] for s in st_p], 0)
    s_nsa = jnp.stack([s[0] for s in st_s], 0)
    p_fox = jnp.stack([s[1] for s in st_p], 0)
    s_fox = jnp.stack([s[1] for s in st_s], 0)
    p_logf = jnp.stack([s[2] for s in st_p], 0)
    s_logf = jnp.stack([s[2] for s in st_s], 0)
    p_win = jnp.stack([s[3] for s in st_p], 0)
    s_win = jnp.stack([s[3] for s in st_s], 0)
    p_pool = jnp.stack([s[4] for s in st_p], 0)
    s_pool = jnp.stack([s[4] for s in st_s], 0)
    return (y_prompt, y_sample, p_nsa, s_nsa, p_fox, s_fox, p_logf, s_logf, p_win, s_win, p_pool, s_pool)
```

```python
import functools

import numpy as np
import jax
import jax.numpy as jnp
from jax import lax
from jax.experimental import pallas as pl
from jax.experimental.pallas import tpu as pltpu

F32 = jnp.float32
BF16 = jnp.bfloat16

HEAD_DIM = 64
POOL_WIDTH = 256
POOL_WINDOWS = (2, 4, 8, 16)
POOL_STATE = 15
NSA_HEADS = 8
NSA_KV_HEADS = 2
NSA_REP = NSA_HEADS // NSA_KV_HEADS
CMP_BLOCK = 32
SEL_BLOCK = 64
SEL_TOPK = 16
WINDOW = 512
FORCE_SCORE = 1e4
FOX_HEADS = 4
ROPE_THETA = 10000.0
RMS_EPS = 1e-6
NEG_INF = -1e30
SCALE = HEAD_DIM ** -0.5

LANES = 128
HALF = LANES // 2
VMEM_LIMIT = 56 * 1024 * 1024

C_UZ, C_QB, C_KV, C_ZB, C_FOX, C_ZC, C_GM, C_SM, C_END = 0, 512, 1024, 1792, 2304, 3072, 3328, 6400, 6528
SM_GB = 0
SM_FC = 24


def _cparams(sem):
    return pltpu.CompilerParams(dimension_semantics=sem, vmem_limit_bytes=VMEM_LIMIT)


def _lane(shape):
    return lax.broadcasted_iota(jnp.int32, shape, len(shape) - 1)


def _row(shape):
    return lax.broadcasted_iota(jnp.int32, shape, len(shape) - 2)


def _split3(x):
    hi = x.astype(BF16)
    r1 = x - hi.astype(F32)
    mid = r1.astype(BF16)
    lo = (r1 - mid.astype(F32)).astype(BF16)
    return hi, mid, lo


def _dot(a, b):
    return jnp.dot(a, b, preferred_element_type=F32)


def _dot_nt(a, b):
    return lax.dot_general(a, b, (((1,), (1,)), ((), ())), preferred_element_type=F32)


def _dot_tn(a, b):
    return lax.dot_general(a, b, (((0,), (0,)), ((), ())), preferred_element_type=F32)


def _dot3(x, m):
    hi, mid, lo = _split3(x)
    return _dot(hi, m) + _dot(mid, m) + _dot(lo, m)


def _swap_halves(x):
    return pltpu.roll(x, HALF, x.ndim - 1)


def _dup_half(x, g):
    lo = _lane(x.shape) < HALF
    sw = _swap_halves(x)
    return jnp.where(lo, x, sw) if g == 0 else jnp.where(lo, sw, x)


def _to_low_half(x, g, fill):
    lo = _lane(x.shape) < HALF
    return jnp.where(lo, x if g == 0 else _swap_halves(x), fill)


def _pair_heads(pieces):
    out = []
    for j in range(0, len(pieces), 2):
        lo = _lane(pieces[j].shape) < HALF
        out.append(jnp.where(lo, pieces[j], pieces[j + 1]))
    return out


def _masked_softmax(s, ok, axis):
    s = jnp.where(ok, s, NEG_INF)
    m = jnp.max(s, axis=axis, keepdims=True)
    e = jnp.where(ok, jnp.exp(s - m), 0.0)
    return e / jnp.maximum(jnp.sum(e, axis=axis, keepdims=True), 1e-30)


def _rope_tile(x, cosf, sinf):
    lane = _lane(x.shape)
    first = (lane % HEAD_DIM) < (HEAD_DIM // 2)
    rot = jnp.where(first, pltpu.roll(x, LANES - HEAD_DIM // 2, 1), pltpu.roll(x, HEAD_DIM // 2, 1))
    return x * cosf + rot * sinf


def _proj_kernel(*refs, prompt, tiles_per_seq):
    if prompt:
        (x_ref, nw_ref, w_ref, cos_ref, sin_ref, bf_ref, ohb_ref, tri_ref, sq_ref, sk_ref,
         uz_ref, nsa_ref, win_ref, fox_ref, zb_ref, zc_ref, gm_ref, sm_ref, q8_ref,
         ksel_ref, vsel_ref, kwin_ref, vwin_ref, fq_ref, fk_ref, fv_ref, carry_ref) = refs
    else:
        (x_ref, nw_ref, w_ref, cos_ref, sin_ref, bf_ref,
         uz_ref, nsa_ref, win_ref, fox_ref, zb_ref, zc_ref, gm_ref, sm_ref, q8_ref, qc_ref) = refs

    x = x_ref[...]
    ms = jnp.mean(x * x, axis=-1, keepdims=True)
    hb = ((x * lax.rsqrt(ms + RMS_EPS)) * nw_ref[...]).astype(BF16)
    cosf = cos_ref[...]
    sinf = sin_ref[...]

    def seg(c0, n):
        return _dot(hb, w_ref[:, c0:c0 + n])

    uz_ref[...] = seg(C_UZ, 512)

    for c in range(4):
        y = _rope_tile(seg(C_QB + c * LANES, LANES), cosf, sinf) * SCALE
        q8_ref[0, 2 * c] = _to_low_half(y, 0, 0.0).astype(BF16)
        q8_ref[0, 2 * c + 1] = _to_low_half(y, 1, 0.0).astype(BF16)

    kcmp = _rope_tile(seg(C_KV, LANES), cosf, sinf)
    vcmp = seg(C_KV + 128, LANES)
    ksel = _rope_tile(seg(C_KV + 256, LANES), cosf, sinf)
    vsel = seg(C_KV + 384, LANES)
    kwin = _rope_tile(seg(C_KV + 512, LANES), cosf, sinf)
    vwin = seg(C_KV + 640, LANES)
    nsa_ref[:, 0:128] = kcmp
    nsa_ref[:, 128:256] = vcmp
    nsa_ref[:, 256:384] = ksel
    nsa_ref[:, 384:512] = vsel
    win_ref[:, 0:128] = kwin
    win_ref[:, 128:256] = vwin

    zb_ref[...] = seg(C_ZB, 512)
    qc = seg(C_FOX, 256)
    kc = seg(C_FOX + 256, 256)
    vc = seg(C_FOX + 512, 256)
    fox_ref[:, 0:256] = kc
    fox_ref[:, 256:512] = vc
    zc_ref[...] = seg(C_ZC, 256)
    for c in range(3):
        gm_ref[:, c * 1024:(c + 1) * 1024] = seg(C_GM + c * 1024, 1024)

    sm = seg(C_SM, LANES)
    lane = _lane(sm.shape)
    is_f = (lane >= SM_FC) & (lane < SM_FC + FOX_HEADS)
    logf = jax.nn.log_sigmoid(sm + bf_ref[...])
    sm_ref[...] = jnp.where(is_f, logf, sm)

    if not prompt:
        qc_ref[...] = qc
        return

    ohb = ohb_ref[...].astype(F32)
    for g in range(2):
        ksel_ref[0, g] = _to_low_half(ksel, g, ohb).astype(BF16)
        vsel_ref[0, g] = _dup_half(vsel, g).astype(BF16)
        kwin_ref[0, g] = _dup_half(kwin, g).astype(BF16)
        vwin_ref[0, g] = _dup_half(vwin, g).astype(BF16)

    i = pl.program_id(0)
    carry = jnp.where(i % tiles_per_seq == 0, 0.0, carry_ref[...])
    csum = _dot3_left(tri_ref[...], jnp.where(is_f, logf, 0.0)) + carry
    carry_ref[...] = csum[csum.shape[0] - 1:, :]
    c_hi, c_mid, c_lo = _split3(csum)
    c3 = jnp.concatenate([c_hi, c_mid, c_lo], axis=1)
    cq = _dot(c3, sq_ref[...])
    ck = _dot(c3, sk_ref[...])
    one_q = ((lane >= HALF + 3) & (lane < HALF + 6)).astype(F32)
    one_k = ((lane >= HALF) & (lane < HALF + 3)).astype(F32)
    for h in range(FOX_HEADS):
        t, g = h // 2, h % 2
        sl = slice(t * LANES, (t + 1) * LANES)
        hs = slice(h * LANES, (h + 1) * LANES)
        fq_ref[0, h] = _to_low_half(qc[:, sl] * SCALE, g, cq[:, hs] + one_q).astype(BF16)
        fk_ref[0, h] = _to_low_half(kc[:, sl], g, ck[:, hs] + one_k).astype(BF16)
        fv_ref[0, h] = _dup_half(vc[:, sl], g).astype(BF16)


def _dot3_left(m, x):
    hi, mid, lo = _split3(x)
    return _dot(m, hi) + _dot(m, mid) + _dot(m, lo)


def _rope_tables(pos):
    half = HEAD_DIM // 2
    inv = ROPE_THETA ** (-jnp.arange(half, dtype=F32) / half)
    ang = pos.astype(F32)[:, None] * inv[None, :]
    cos, sin = jnp.cos(ang), jnp.sin(ang)
    return jnp.tile(cos, (1, 4)), jnp.tile(jnp.concatenate([-sin, sin], 1), (1, 2))


def _fox_aug_matrices():
    sq = np.zeros((3 * LANES, FOX_HEADS * LANES), np.float32)
    sk = np.zeros((3 * LANES, FOX_HEADS * LANES), np.float32)
    for h in range(FOX_HEADS):
        for p in range(3):
            sq[p * LANES + SM_FC + h, h * LANES + HALF + p] = 1.0
            sk[p * LANES + SM_FC + h, h * LANES + HALF + 3 + p] = -1.0
    return jnp.asarray(sq, BF16), jnp.asarray(sk, BF16)


def _project(x2d, norm_w, w_prep, fox_bf, pos, *, prompt, batch, seq):
    n, d = x2d.shape
    tm = 256 if prompt else n
    grid = (n // tm,)
    tiles_per_seq = seq // tm if prompt else 1
    cosf, sinf = _rope_tables(pos)
    bf_row = jnp.zeros((1, LANES), F32).at[0, SM_FC:SM_FC + FOX_HEADS].set(fox_bf)
    row = lambda w: pl.BlockSpec((tm, w), lambda i: (i, 0))
    full = lambda a: pl.BlockSpec(a.shape, lambda i: (0,) * a.ndim)
    if prompt:
        tab = pl.BlockSpec((tm, LANES), lambda i: (i % tiles_per_seq, 0))
    else:
        tab = row(LANES)
    in_arrays = [x2d, norm_w.reshape(1, d), w_prep, cosf, sinf, bf_row]
    in_specs = [row(d), full(in_arrays[1]), full(w_prep), tab, tab, full(bf_row)]
    hb = lambda nh: pl.BlockSpec((1, nh, tm, LANES), lambda i: (i // tiles_per_seq, 0, i % tiles_per_seq, 0))
    out_shapes = [
        jax.ShapeDtypeStruct((n, 512), F32), jax.ShapeDtypeStruct((n, 512), F32),
        jax.ShapeDtypeStruct((n, 256), F32), jax.ShapeDtypeStruct((n, 512), F32),
        jax.ShapeDtypeStruct((n, 512), F32), jax.ShapeDtypeStruct((n, 256), F32),
        jax.ShapeDtypeStruct((n, 3072), F32), jax.ShapeDtypeStruct((n, LANES), F32),
    ]
    out_specs = [row(512), row(512), row(256), row(512), row(512), row(256), row(3072), row(LANES)]
    scratch = []
    if prompt:
        t = np.arange(seq)
        ohb = np.zeros((seq, LANES), np.float32)
        ohb[t, HALF + t // SEL_BLOCK] = -NEG_INF
        tri = np.tril(np.ones((tm, tm), np.float32))
        sq, sk = _fox_aug_matrices()
        extra = [jnp.asarray(ohb, BF16), jnp.asarray(tri, BF16), sq, sk]
        in_arrays += extra
        in_specs += [tab, full(extra[1]), full(sq), full(sk)]
        out_shapes.append(jax.ShapeDtypeStruct((batch, NSA_HEADS, seq, LANES), BF16))
        out_specs.append(hb(NSA_HEADS))
        for nh in (2, 2, 2, 2, FOX_HEADS, FOX_HEADS, FOX_HEADS):
            out_shapes.append(jax.ShapeDtypeStruct((batch, nh, seq, LANES), BF16))
            out_specs.append(hb(nh))
        scratch = [pltpu.VMEM((1, LANES), F32)]
    else:
        out_shapes.append(jax.ShapeDtypeStruct((1, NSA_HEADS, n, LANES), BF16))
        out_specs.append(pl.BlockSpec((1, NSA_HEADS, tm, LANES), lambda i: (0, 0, i, 0)))
        out_shapes.append(jax.ShapeDtypeStruct((n, 256), F32))
        out_specs.append(row(256))
    return pl.pallas_call(
        functools.partial(_proj_kernel, prompt=prompt, tiles_per_seq=tiles_per_seq),
        grid=grid, in_specs=in_specs, out_specs=out_specs, out_shape=out_shapes,
        scratch_shapes=scratch, compiler_params=_cparams(("arbitrary",)),
        name="proj_prompt" if prompt else "proj_decode",
    )(*in_arrays)


def _pool_windows(s_ref, n, pos):
    u = s_ref[16:16 + n, :]
    grp = _lane(u.shape) // (POOL_WIDTH // len(POOL_WINDOWS))
    acc = u
    win = None
    for j in range(1, POOL_WINDOWS[-1]):
        acc = acc + s_ref[16 - j:16 - j + n, :]
        if j + 1 in POOL_WINDOWS:
            k = POOL_WINDOWS.index(j + 1)
            win = acc if win is None else jnp.where(grp >= k, acc, win)
    w = jnp.zeros(u.shape, jnp.int32)
    for k, wk in enumerate(POOL_WINDOWS):
        w = jnp.where(grp == k, wk, w)
    cnt = jnp.minimum(pos + 1, w).astype(F32)
    return win / cnt - u


def _pool_prompt_kernel(u_ref, halo_ref, d_ref, s_ref, *, tiles_per_seq, tm):
    i = pl.program_id(0) % tiles_per_seq
    s_ref[0:16, :] = jnp.where(i == 0, 0.0, halo_ref[...])
    s_ref[16:, :] = u_ref[...]
    pos = i * tm + _row((tm, POOL_WIDTH))
    d_ref[...] = _pool_windows(s_ref, tm, pos)


def _pool_prompt(uz, seq):
    n = uz.shape[0]
    tm = 512
    tiles_per_seq = seq // tm
    return pl.pallas_call(
        functools.partial(_pool_prompt_kernel, tiles_per_seq=tiles_per_seq, tm=tm),
        grid=(n // tm,),
        in_specs=[pl.BlockSpec((tm, POOL_WIDTH), lambda i: (i, 0)),
                  pl.BlockSpec((16, POOL_WIDTH), lambda i: (jnp.maximum(i * (tm // 16) - 1, 0), 0))],
        out_specs=pl.BlockSpec((tm, POOL_WIDTH), lambda i: (i, 0)),
        out_shape=jax.ShapeDtypeStruct((n, POOL_WIDTH), F32),
        scratch_shapes=[pltpu.VMEM((tm + 16, POOL_WIDTH), F32)],
        compiler_params=_cparams(("arbitrary",)), name="pool_prompt",
    )(uz, uz)


def _pool_decode_kernel(seq_ref, d_ref, *, pos0):
    n = d_ref.shape[1]
    pos = pos0 + _row((n, POOL_WIDTH))
    d_ref[0] = _pool_windows(seq_ref.at[0], n, pos)


def _pool_decode(state, u, pos0):
    b, t, w = u.shape
    n = 8
    seq = jnp.concatenate([jnp.zeros((b, 1, w), F32), state, u, jnp.zeros((b, n - t, w), F32)], 1)
    d = pl.pallas_call(
        functools.partial(_pool_decode_kernel, pos0=pos0),
        grid=(b,),
        in_specs=[pl.BlockSpec((1, 16 + n, w), lambda i: (i, 0, 0))],
        out_specs=pl.BlockSpec((1, n, w), lambda i: (i, 0, 0)),
        out_shape=jax.ShapeDtypeStruct((b, n, w), F32),
        compiler_params=_cparams(("arbitrary",)), name="pool_decode",
    )(seq)
    return d[:, :t]


def _select_rank(imp, blk, cur, k_top):
    forced = (blk == 0) | (blk == cur) | (blk == cur - 1)
    val = jnp.where(blk <= cur, imp + jnp.where(forced, FORCE_SCORE, 0.0), -1.0)
    cnt = jnp.zeros(val.shape, jnp.int32)
    for i in range(val.shape[0]):
        vi = val[i:i + 1, :]
        cnt = cnt + ((vi > val) | ((vi == val) & (blk > i))).astype(jnp.int32)
    return (cnt < k_top) & (blk <= cur)


def _even_odd_rows(ref, half):
    return jnp.concatenate([ref[pl.ds(0, half, stride=2), :], ref[pl.ds(1, half, stride=2), :]], axis=0)


def _cmp_prompt_kernel(nsa_ref, q_ref, o_ref, qaug_ref, tmp_ref, kc_ref, vc_ref, *, tq, seq):
    g = pl.program_id(1)
    qi = pl.program_id(2)
    nb = seq // CMP_BLOCK
    ns = seq // SEL_BLOCK

    @pl.when((g == 0) & (qi == 0))
    def _():
        xs = nsa_ref[0]
        means = xs.reshape(nb, CMP_BLOCK, xs.shape[-1]).sum(axis=1) * (1.0 / CMP_BLOCK)
        for c, dst in enumerate((kc_ref, vc_ref)):
            tmp_ref[...] = means[:, c * LANES:(c + 1) * LANES]
            perm = _even_odd_rows(tmp_ref, ns)
            for gg in range(2):
                dst[gg] = _dup_half(perm, gg).astype(BF16)

    qs = jnp.concatenate([q_ref[0, r] for r in range(NSA_REP)], axis=0)
    s = _dot_nt(kc_ref[g], qs)
    row = _row(s.shape)
    nblk = jnp.where(row < ns, 2 * row, 2 * (row - ns) + 1)
    t = qi * tq + _lane(s.shape) % tq
    p = _masked_softmax(s, (nblk + 1) * CMP_BLOCK - 1 <= t, 0)
    o = _dot_tn(p.astype(BF16), vc_ref[g])
    pieces = _pair_heads([o[r * tq:(r + 1) * tq] for r in range(NSA_REP)])
    for j, piece in enumerate(pieces):
        o_ref[0, :, j * LANES:(j + 1) * LANES] = piece

    imp = p[:, 0:tq]
    for r in range(1, NSA_REP):
        imp = imp + p[:, r * tq:(r + 1) * tq]
    imp = imp[:ns] + imp[ns:]
    blk = _row(imp.shape)
    cur = (qi * tq + _lane(imp.shape)) // SEL_BLOCK
    sel = _select_rank(imp, blk, cur, min(SEL_TOPK, ns))
    selm = jnp.where(sel, 0.0, -1.0)
    pad = [jnp.zeros((HALF, tq), F32), selm]
    if ns < HALF:
        pad.append(jnp.zeros((HALF - ns, tq), F32))
    sel_t = jnp.concatenate(pad, axis=0).T
    for r in range(NSA_REP):
        qaug_ref[0, 0, 0, r * tq:(r + 1) * tq, :] = (q_ref[0, r].astype(F32) + sel_t).astype(BF16)


def _cmp_prompt(nsa_rows, q8, batch, seq):
    tq = LANES
    nq = seq // tq
    nb = seq // CMP_BLOCK
    assert seq % (2 * LANES) == 0 and seq // SEL_BLOCK <= HALF
    return pl.pallas_call(
        functools.partial(_cmp_prompt_kernel, tq=tq, seq=seq),
        grid=(batch, NSA_KV_HEADS, nq),
        in_specs=[pl.BlockSpec((1, seq, 256), lambda b, g, q: (b, 0, 0)),
                  pl.BlockSpec((1, NSA_REP, tq, LANES), lambda b, g, q: (b, g, q, 0))],
        out_specs=[pl.BlockSpec((1, tq, 256), lambda b, g, q: (b, q, g)),
                   pl.BlockSpec((1, 1, 1, NSA_REP * tq, LANES), lambda b, g, q: (b, g, q, 0, 0))],
        out_shape=[jax.ShapeDtypeStruct((batch, seq, 512), F32),
                   jax.ShapeDtypeStruct((batch, NSA_KV_HEADS, nq, NSA_REP * tq, LANES), BF16)],
        scratch_shapes=[pltpu.VMEM((nb, LANES), F32), pltpu.VMEM((2, nb, LANES), BF16),
                        pltpu.VMEM((2, nb, LANES), BF16)],
        compiler_params=_cparams(("arbitrary", "arbitrary", "arbitrary")), name="nsa_cmp_prompt",
    )(nsa_rows.reshape(batch, seq, 512), q8)


def _flash_kernel(q_ref, k_ref, v_ref, o_ref, *, n_sub, n_stack, tpt, tk):
    qi = pl.program_id(2)
    q0 = qi * tpt
    nfull = q0 // tk
    rows = n_stack * tpt
    pieces = []
    for u in range(n_sub):
        q = q_ref[0, u, 0]

        def step(j, carry, masked):
            m, l, acc = carry
            start = pl.multiple_of(j * tk, tk)
            k = k_ref[0, u, pl.ds(start, tk), :]
            v = v_ref[0, u, pl.ds(start, tk), :]
            s = _dot_nt(q, k)
            if masked:
                key = j * tk + _lane(s.shape)
                tpos = q0 + _row(s.shape) % tpt
                s = jnp.where(key <= tpos, s, NEG_INF)
            mn = jnp.maximum(m, jnp.max(s, axis=-1, keepdims=True))
            a = jnp.exp(m - mn)
            p = jnp.exp(s - mn)
            return mn, a * l + jnp.sum(p, axis=-1, keepdims=True), a * acc + _dot(p.astype(BF16), v)

        init = (jnp.full((rows, 1), -1e38, F32), jnp.zeros((rows, 1), F32), jnp.zeros((rows, LANES), F32))
        carry = lax.fori_loop(0, nfull, lambda j, c: step(j, c, False), init)
        _, l, acc = step(nfull, carry, True)
        o = acc / l
        pieces += [o[r * tpt:(r + 1) * tpt] for r in range(n_stack)]
    for j, piece in enumerate(_pair_heads(pieces)):
        o_ref[0, :, j * LANES:(j + 1) * LANES] = piece


def _flash(q_aug, k_aug, v_dup, *, n_sub, n_stack, tpt, tk, name):
    b, s, nq, rows, _ = q_aug.shape
    seq = k_aug.shape[2]
    assert tk % tpt == 0 and seq % tk == 0 and rows == n_stack * tpt
    wout = n_sub * n_stack * HEAD_DIM
    return pl.pallas_call(
        functools.partial(_flash_kernel, n_sub=n_sub, n_stack=n_stack, tpt=tpt, tk=tk),
        grid=(b, s // n_sub, nq),
        in_specs=[pl.BlockSpec((1, n_sub, 1, rows, LANES), lambda bb, ss, q: (bb, ss, q, 0, 0)),
                  pl.BlockSpec((1, n_sub, seq, LANES), lambda bb, ss, q: (bb, ss, 0, 0)),
                  pl.BlockSpec((1, n_sub, seq, LANES), lambda bb, ss, q: (bb, ss, 0, 0))],
        out_specs=pl.BlockSpec((1, tpt, wout), lambda bb, ss, q: (bb, q, ss)),
        out_shape=jax.ShapeDtypeStruct((b, seq, s * n_stack * HEAD_DIM), F32),
        compiler_params=_cparams(("arbitrary", "arbitrary", "arbitrary")), name=name,
    )(q_aug, k_aug, v_dup)


def _win_prompt_kernel(q_ref, k_ref, v_ref, o_ref, *, tq):
    qi = pl.program_id(2)
    q0 = qi * tq
    start = pl.multiple_of(jnp.maximum(q0 - WINDOW, 0), LANES)
    k = k_ref[0, 0, pl.ds(start, WINDOW + tq), :]
    v = v_ref[0, 0, pl.ds(start, WINDOW + tq), :]
    qs = jnp.concatenate([q_ref[0, r] for r in range(NSA_REP)], axis=0)
    s = _dot_nt(qs, k)
    key = start + _lane(s.shape)
    t = q0 + _row(s.shape) % tq
    p = _masked_softmax(s, (key <= t) & (key > t - WINDOW), -1)
    o = _dot(p.astype(BF16), v)
    for j, piece in enumerate(_pair_heads([o[r * tq:(r + 1) * tq] for r in range(NSA_REP)])):
        o_ref[0, :, j * LANES:(j + 1) * LANES] = piece


def _win_prompt(q8, kwin, vwin, batch, seq):
    tq = LANES
    assert seq >= WINDOW + tq
    kv = pl.BlockSpec((1, 1, seq, LANES), lambda b, g, q: (b, g, 0, 0))
    return pl.pallas_call(
        functools.partial(_win_prompt_kernel, tq=tq),
        grid=(batch, NSA_KV_HEADS, seq // tq),
        in_specs=[pl.BlockSpec((1, NSA_REP, tq, LANES), lambda b, g, q: (b, g, q, 0)), kv, kv],
        out_specs=pl.BlockSpec((1, tq, 256), lambda b, g, q: (b, q, g)),
        out_shape=jax.ShapeDtypeStruct((batch, seq, 512), F32),
        compiler_params=_cparams(("arbitrary", "arbitrary", "arbitrary")), name="nsa_win_prompt",
    )(q8, kwin, vwin)


def _out_kernel(x_ref, d_ref, za_ref, ocmp_ref, osel_ref, owin_ref, zb_ref, oc_ref, zc_ref, gm_ref, sm_ref,
                pw_ref, ps_ref, wa_ref, wb_ref, wc_ref, wo_ref, e3_ref, fn_ref, xo_ref, y_ref):
    oa = _dot(d_ref[...].astype(BF16), pw_ref[...]) * ps_ref[...]
    ya = _dot((oa * jax.nn.silu(za_ref[...])).astype(BF16), wa_ref[...])
    gx = _dot3(jax.nn.sigmoid(sm_ref[...]), e3_ref[...])
    ob = gx[:, 0:512] * ocmp_ref[...] + gx[:, 512:1024] * osel_ref[...] + gx[:, 1024:1536] * owin_ref[...]
    yb = _dot((ob * jax.nn.silu(zb_ref[...])).astype(BF16), wb_ref[...])
    yc = _dot((oc_ref[...] * jax.nn.silu(zc_ref[...])).astype(BF16), wc_ref[...])
    d = ya.shape[-1]
    mixed = (jax.nn.sigmoid(gm_ref[:, 0:d]) * ya + jax.nn.sigmoid(gm_ref[:, d:2 * d]) * yb
             + jax.nn.sigmoid(gm_ref[:, 2 * d:3 * d]) * yc)
    xo = x_ref[...] + _dot(mixed.astype(BF16), wo_ref[...])
    xo_ref[...] = xo
    ms = jnp.mean(xo * xo, axis=-1, keepdims=True)
    y_ref[...] = (xo * lax.rsqrt(ms + RMS_EPS)) * fn_ref[...]


def _gate_expand_matrix():
    e = np.zeros((LANES, 3 * NSA_HEADS * HEAD_DIM), np.float32)
    for hd in range(NSA_HEADS):
        for c in range(3):
            e[SM_GB + hd * 3 + c, c * 512 + hd * HEAD_DIM:c * 512 + (hd + 1) * HEAD_DIM] = 1.0
    return jnp.asarray(e, BF16)


def _output(x2d, d, uz, ocmp, osel, owin, zb, oc, zc, gm, sm, lw, final_norm):
    n, dm = x2d.shape
    tm = min(256, n)
    row = lambda w, c=0: pl.BlockSpec((tm, w), lambda i: (i, c))
    full = lambda a: pl.BlockSpec(a.shape, lambda i: (0,) * a.ndim)
    consts = [lw["pool_bd"], lw["pool_scale"], lw["w_out_a"], lw["w_out_b"], lw["w_out_c"], lw["w_o"],
              _gate_expand_matrix(), final_norm.reshape(1, dm)]
    return pl.pallas_call(
        _out_kernel, grid=(n // tm,),
        in_specs=[row(dm), row(256), row(256, 1), row(512), row(512), row(512), row(512), row(256), row(256),
                  row(3072), row(LANES)] + [full(c) for c in consts],
        out_specs=[row(dm), row(dm)],
        out_shape=[jax.ShapeDtypeStruct((n, dm), F32), jax.ShapeDtypeStruct((n, dm), F32)],
        compiler_params=_cparams(("arbitrary",)), name="out_proj",
    )(x2d, d, uz, ocmp, osel, owin, zb, oc, zc, gm, sm, *consts)


def _page_copies(cache_ref, pt_ref, buf_ref, sem_ref, step, slot, *, n_pages, nch, ppc, col0, cw):
    base = (step // nch) * n_pages + (step % nch) * ppc
    return [pltpu.make_async_copy(cache_ref.at[pt_ref[base + i], :, pl.ds(col0, cw)],
                                  buf_ref.at[slot, i], sem_ref.at[slot]) for i in range(ppc)]


def _stream_pages(cache_ref, pt_ref, buf_ref, sem_ref, **kw):
    nch = kw["nch"]
    step = pl.program_id(0) * nch + pl.program_id(1)
    total = pl.num_programs(0) * nch
    slot = step % 2

    @pl.when(step == 0)
    def _():
        for cp in _page_copies(cache_ref, pt_ref, buf_ref, sem_ref, step, slot, **kw):
            cp.start()

    @pl.when(step + 1 < total)
    def _():
        for cp in _page_copies(cache_ref, pt_ref, buf_ref, sem_ref, step + 1, 1 - slot, **kw):
            cp.start()

    for cp in _page_copies(cache_ref, pt_ref, buf_ref, sem_ref, step, slot, **kw):
        cp.wait()
    return slot


def _topk_extract(val, k_top):
    lane = _lane(val.shape)
    sel = jnp.zeros(val.shape, jnp.bool_)
    for _ in range(k_top):
        mx = jnp.max(val, axis=-1, keepdims=True)
        idx = jnp.min(jnp.where(val == mx, lane, val.shape[-1]), axis=-1, keepdims=True)
        pick = lane == idx
        sel = sel | pick
        val = jnp.where(pick, -3e38, val)
    return sel


def _cmp_decode_kernel(pt_ref, cache_ref, q_ref, o_ref, sel_ref, buf_ref, sem_ref, kcs_ref,
                       *, n_pages, nch, ppc, past):
    c = pl.program_id(1)
    slot = _stream_pages(cache_ref, pt_ref, buf_ref, sem_ref, n_pages=n_pages, nch=nch, ppc=ppc, col0=0, cw=256)
    bpp = 128 // CMP_BLOCK
    means = buf_ref[slot].reshape(ppc * bpp, CMP_BLOCK, 256).sum(axis=1) * (1.0 / CMP_BLOCK)
    rows = pl.ds(pl.multiple_of(c * ppc * bpp, 8), ppc * bpp)
    kcs_ref[0, rows, :] = means[:, 0:LANES]
    kcs_ref[1, rows, :] = means[:, LANES:2 * LANES]

    @pl.when(c == nch - 1)
    def _():
        nb = n_pages * bpp
        nsd = nb // 2
        kperm = _even_odd_rows(kcs_ref.at[0], nsd)
        vperm = _even_odd_rows(kcs_ref.at[1], nsd)
        for g in range(2):
            kc = _dup_half(kperm, g).astype(BF16)
            vc = _dup_half(vperm, g).astype(BF16)
            s = _dot_nt(q_ref[0, g], kc)
            lane = _lane(s.shape)
            nblk = jnp.where(lane < nsd, 2 * lane, 2 * (lane - nsd) + 1)
            qpos = past + _row(s.shape) % 8
            p = _masked_softmax(s, (nblk + 1) * CMP_BLOCK - 1 <= qpos, -1)
            o_ref[0, g] = _dot(p.astype(BF16), vc)
            imp = p[0:8]
            for r in range(1, NSA_REP):
                imp = imp + p[r * 8:(r + 1) * 8]
            imp = jnp.concatenate([imp[:, :nsd] + imp[:, nsd:], jnp.zeros((8, LANES), F32)], axis=1)
            blk = _lane(imp.shape)
            cur = (past + _row(imp.shape)) // SEL_BLOCK
            forced = (blk == 0) | (blk == cur) | (blk == cur - 1)
            val = jnp.where(blk <= cur, imp + jnp.where(forced, FORCE_SCORE, 0.0), -1.0)
            val = jnp.where(blk <= nsd, val, -3e38)
            sel = _topk_extract(val, min(SEL_TOPK, nsd + 1)) & (blk <= cur)
            sel_ref[0, g] = jnp.where(sel, 0.0, -1.0)


def _cmp_decode(pt_flat, cache, qs, *, batch, n_pages, past):
    ppc = min(16, n_pages)
    nch = n_pages // ppc
    nb = n_pages * (128 // CMP_BLOCK)
    nsd = nb // 2
    grid_spec = pltpu.PrefetchScalarGridSpec(
        num_scalar_prefetch=1, grid=(batch, nch),
        in_specs=[pl.BlockSpec(memory_space=pl.ANY),
                  pl.BlockSpec((1, 2, 32, LANES), lambda b, c, pt: (b, 0, 0, 0))],
        out_specs=[pl.BlockSpec((1, 2, 32, LANES), lambda b, c, pt: (b, 0, 0, 0)),
                   pl.BlockSpec((1, 2, 8, nsd + LANES), lambda b, c, pt: (b, 0, 0, 0))],
        scratch_shapes=[pltpu.VMEM((2, ppc, 128, 256), F32), pltpu.SemaphoreType.DMA((2,)),
                        pltpu.VMEM((2, nb, LANES), F32)])
    return pl.pallas_call(
        functools.partial(_cmp_decode_kernel, n_pages=n_pages, nch=nch, ppc=ppc, past=past),
        grid_spec=grid_spec,
        out_shape=[jax.ShapeDtypeStruct((batch, 2, 32, LANES), F32),
                   jax.ShapeDtypeStruct((batch, 2, 8, nsd + LANES), F32)],
        compiler_params=_cparams(("arbitrary", "arbitrary")), name="nsa_cmp_decode",
    )(pt_flat, cache, qs)


def _online_update(s, v, m_ref, l_ref, acc_ref):
    m = m_ref[...]
    mn = jnp.maximum(m, jnp.max(s, axis=-1, keepdims=True))
    a = jnp.exp(m - mn)
    p = jnp.exp(s - mn)
    l_ref[...] = a * l_ref[...] + jnp.sum(p, axis=-1, keepdims=True)
    acc_ref[...] = a * acc_ref[...] + _dot(p.astype(BF16), v)
    m_ref[...] = mn


def _attn_decode_kernel(pt_ref, cache_ref, q_ref, bias_ref, bnew_ref, aux_ref, new_ref, o_ref,
                        buf_ref, sem_ref, m_ref, l_ref, acc_ref,
                        *, n_pages, nch, ppc, col0, kw, fox):
    c = pl.program_id(1)
    slot = _stream_pages(cache_ref, pt_ref, buf_ref, sem_ref, n_pages=n_pages, nch=nch, ppc=ppc,
                         col0=col0, cw=2 * kw)

    @pl.when(c == 0)
    def _():
        m_ref[...] = jnp.full(m_ref.shape, -1e38, F32)
        l_ref[...] = jnp.zeros(l_ref.shape, F32)
        acc_ref[...] = jnp.zeros(acc_ref.shape, F32)

    q = q_ref[0]
    qadd = aux_ref[0] if fox else 0.0

    def bias32(b8):
        return jnp.concatenate([b8] * 4, axis=0)

    kv = buf_ref[slot].reshape(ppc * 128, 2 * kw)
    if fox:
        b8 = bias_ref[0]
    else:
        b8 = _dot(bias_ref[0, 0], aux_ref[...]) * (-NEG_INF)
    s = _dot_nt(q, kv[:, :kw].astype(BF16)) + bias32(b8) + qadd
    _online_update(s, kv[:, kw:].astype(BF16), m_ref, l_ref, acc_ref)

    @pl.when(c == nch - 1)
    def _():
        kvn = new_ref[0]
        sn = _dot_nt(q, kvn[:, :kw].astype(BF16)) + bias32(bnew_ref[0] if fox else bnew_ref[0] * (-NEG_INF)) + qadd
        j = _lane(sn.shape)
        t = _row(sn.shape) // 8 if fox else _row(sn.shape) % 4
        sn = jnp.where(j <= t, sn, NEG_INF)
        _online_update(sn, kvn[:, kw:].astype(BF16), m_ref, l_ref, acc_ref)
        o_ref[0] = acc_ref[...] / l_ref[...]


def _attn_decode(pt_flat, cache, q, bias, bnew, aux, new_rows, *, batch, n_pages, col0, kw, fox, name):
    ppc = min(16, n_pages)
    nch = n_pages // ppc
    tk = ppc * 128
    if fox:
        bias_spec = pl.BlockSpec((1, 8, tk), lambda b, c, pt: (b, 0, c))
        aux_spec = pl.BlockSpec((1, 32, 1), lambda b, c, pt: (b, 0, 0))
    else:
        bias_spec = pl.BlockSpec((1, 1, 8, LANES), lambda b, c, pt: (b, c, 0, 0))
        aux_spec = pl.BlockSpec(aux.shape, lambda b, c, pt: (0, 0))
    grid_spec = pltpu.PrefetchScalarGridSpec(
        num_scalar_prefetch=1, grid=(batch, nch),
        in_specs=[pl.BlockSpec(memory_space=pl.ANY),
                  pl.BlockSpec((1, 32, kw), lambda b, c, pt: (b, 0, 0)),
                  bias_spec,
                  pl.BlockSpec((1, 8, LANES), lambda b, c, pt: (b, 0, 0)),
                  aux_spec,
                  pl.BlockSpec((1, 128, 2 * kw), lambda b, c, pt: (b, 0, 0))],
        out_specs=pl.BlockSpec((1, 32, kw), lambda b, c, pt: (b, 0, 0)),
        scratch_shapes=[pltpu.VMEM((2, ppc, 128, 2 * kw), F32), pltpu.SemaphoreType.DMA((2,)),
                        pltpu.VMEM((32, 1), F32), pltpu.VMEM((32, 1), F32), pltpu.VMEM((32, kw), F32)])
    return pl.pallas_call(
        functools.partial(_attn_decode_kernel, n_pages=n_pages, nch=nch, ppc=ppc, col0=col0, kw=kw, fox=fox),
        grid_spec=grid_spec, out_shape=jax.ShapeDtypeStruct((batch, 32, kw), F32),
        compiler_params=_cparams(("arbitrary", "arbitrary")), name=name,
    )(pt_flat, cache, q, bias, bnew, aux, new_rows)


def _win_decode_kernel(q_ref, st_ref, new_ref, o_ref):
    st = st_ref[0]
    nw = new_ref[0]
    k = jnp.concatenate([st[:, 0:LANES], nw[:, 0:LANES]], axis=0).astype(BF16)
    v = jnp.concatenate([st[:, LANES:2 * LANES], nw[:, LANES:2 * LANES]], axis=0).astype(BF16)
    lw = st.shape[0]
    for g in range(2):
        s = _dot_nt(q_ref[0, g], k)
        i = _lane(s.shape)
        t = _row(s.shape) % 8
        p = _masked_softmax(s, (i > t + lw - WINDOW) & (i <= t + lw) & (i < lw + 8), -1)
        o_ref[0, g] = _dot(p.astype(BF16), v)


def _win_decode(qsw, state_win, win_new, batch):
    lw = state_win.shape[1]
    return pl.pallas_call(
        _win_decode_kernel, grid=(batch,),
        in_specs=[pl.BlockSpec((1, 2, 32, LANES), lambda b: (b, 0, 0, 0)),
                  pl.BlockSpec((1, lw, 256), lambda b: (b, 0, 0)),
                  pl.BlockSpec((1, 128, 256), lambda b: (b, 0, 0))],
        out_specs=pl.BlockSpec((1, 2, 32, LANES), lambda b: (b, 0, 0, 0)),
        out_shape=jax.ShapeDtypeStruct((batch, 2, 32, LANES), F32),
        compiler_params=_cparams(("arbitrary",)), name="nsa_win_decode",
    )(qsw, state_win, win_new)


def _logf_decode_kernel(pt_ref, cache_ref, new_ref, m1_ref, s_ref, tri_ref, c_ref, buf_ref, sem_ref, *, n_pages):
    b = pl.program_id(0)
    nb = pl.num_programs(0)

    def copies(bb, slot):
        return [pltpu.make_async_copy(cache_ref.at[pl.ds(pt_ref[bb * n_pages + i], 1), :],
                                      buf_ref.at[slot, pl.ds(i, 1), :], sem_ref.at[slot]) for i in range(n_pages)]

    slot = b % 2

    @pl.when(b == 0)
    def _():
        for cp in copies(b, slot):
            cp.start()

    @pl.when(b + 1 < nb)
    def _():
        for cp in copies(b + 1, 1 - slot):
            cp.start()

    for cp in copies(b, slot):
        cp.wait()
    lp = jnp.concatenate([buf_ref[slot], new_ref[0]], axis=0)
    cs = _dot3(lp, m1_ref[...])
    tot = _dot3(cs, s_ref[...])
    c_ref[0] = cs + _dot3_left(tri_ref[...], tot)


def _logf_decode(pt_flat, cache2d, new_row, *, batch, n_pages):
    page = 128
    w = page * FOX_HEADS
    i_in = np.arange(w)
    r_in, h_in = i_in // FOX_HEADS, i_in % FOX_HEADS
    i_out = np.arange(w)
    h_out, r_out = i_out // page, i_out % page
    m1 = ((h_in[:, None] == h_out[None, :]) & (r_in[:, None] <= r_out[None, :])).astype(np.float32)
    ssel = ((h_out[:, None] == h_out[None, :]) & (r_out[:, None] == page - 1)).astype(np.float32)
    np_tot = n_pages + 8
    tri = np.tril(np.ones((np_tot, np_tot), np.float32), -1)
    consts = [jnp.asarray(m1, BF16), jnp.asarray(ssel, BF16), jnp.asarray(tri, BF16)]
    grid_spec = pltpu.PrefetchScalarGridSpec(
        num_scalar_prefetch=1, grid=(batch,),
        in_specs=[pl.BlockSpec(memory_space=pl.ANY),
                  pl.BlockSpec((1, 8, w), lambda b, pt: (b, 0, 0))]
                 + [pl.BlockSpec(c.shape, lambda b, pt: (0, 0)) for c in consts],
        out_specs=pl.BlockSpec((1, np_tot, w), lambda b, pt: (b, 0, 0)),
        scratch_shapes=[pltpu.VMEM((2, n_pages, w), F32), pltpu.SemaphoreType.DMA((2,))])
    return pl.pallas_call(
        functools.partial(_logf_decode_kernel, n_pages=n_pages),
        grid_spec=grid_spec, out_shape=jax.ShapeDtypeStruct((batch, np_tot, w), F32),
        compiler_params=_cparams(("arbitrary",)), name="fox_logf_decode",
    )(pt_flat, cache2d, new_row, *consts)


def _prep_layer_weights(w_in, pool_w, pool_scale, w_out_a, w_out_b, w_out_c, w_o):
    d = w_in.shape[0]
    w = jnp.concatenate([w_in[:, 0:1792], w_in[:, 1816:3096], w_in[:, 3100:6428], w_in[:, 1792:1816],
                         w_in[:, 3096:3100], jnp.zeros((d, C_END - 6428), w_in.dtype)], axis=1).astype(BF16)
    gd = pool_w.shape[-1]
    bd = jnp.zeros((POOL_WIDTH, POOL_WIDTH), F32)
    for g in range(pool_w.shape[0]):
        bd = bd.at[g * gd:(g + 1) * gd, g * gd:(g + 1) * gd].set(pool_w[g])
    return dict(w_in=w, pool_bd=bd.astype(BF16), pool_scale=pool_scale.reshape(1, -1),
                w_out_a=w_out_a.astype(BF16), w_out_b=w_out_b.astype(BF16), w_out_c=w_out_c.astype(BF16),
                w_o=w_o.astype(BF16))


def _layer_prompt(x2d, lw, norm_w, fox_bf, final_norm, batch, seq):
    pos = jnp.arange(seq)
    (uz, nsa, win, fox, zb, zc, gm, sm, q8, ksel, vsel, kwin, vwin, fq, fk, fv) = _project(
        x2d, norm_w, lw["w_in"], fox_bf, pos, prompt=True, batch=batch, seq=seq)
    d = _pool_prompt(uz, seq)
    ocmp, qaug = _cmp_prompt(nsa, q8, batch, seq)
    osel = _flash(qaug, ksel, vsel, n_sub=1, n_stack=NSA_REP, tpt=LANES, tk=512, name="nsa_sel_prompt")
    owin = _win_prompt(q8, kwin, vwin, batch, seq)
    tpt = 512
    oc = _flash(fq.reshape(batch, FOX_HEADS, seq // tpt, tpt, LANES), fk, fv,
                n_sub=2, n_stack=1, tpt=tpt, tk=512, name="fox_prompt")
    n = batch * seq
    xo, y = _output(x2d, d, uz, ocmp.reshape(n, 512), osel.reshape(n, 512), owin.reshape(n, 512), zb,
                    oc.reshape(n, 256), zc, gm, sm, lw, final_norm)
    states = (nsa.reshape(batch, seq, 4, NSA_KV_HEADS, HEAD_DIM),
              fox.reshape(batch, seq, 2, FOX_HEADS, HEAD_DIM),
              sm[:, SM_FC:SM_FC + FOX_HEADS].reshape(batch, seq, FOX_HEADS),
              win.reshape(batch, seq, 2, NSA_KV_HEADS, HEAD_DIM)[:, -min(WINDOW, seq):],
              uz[:, :POOL_WIDTH].reshape(batch, seq, POOL_WIDTH)[:, -POOL_STATE:])
    return xo, y, states


def _pad_rows(a, rows):
    return jnp.concatenate([a, jnp.zeros((a.shape[0], rows - a.shape[1]) + a.shape[2:], a.dtype)], axis=1)


def _layer_decode(x2d, lw, norm_w, fox_bf, final_norm, pt_flat, nsa_cache, fox_cache, logf_cache,
                  state_win, state_pool, batch, tdec, n_pages, page):
    past = n_pages * page
    n = batch * tdec
    pos = past + jnp.tile(jnp.arange(tdec), batch)
    (uz, nsa, win, fox, zb, zc, gm, sm, q8, qc) = _project(
        x2d, norm_w, lw["w_in"], fox_bf, pos, prompt=False, batch=batch, seq=tdec)
    d = _pool_decode(state_pool, uz[:, :POOL_WIDTH].reshape(batch, tdec, POOL_WIDTH), past).reshape(n, POOL_WIDTH)

    q5 = q8[0].reshape(NSA_KV_HEADS, NSA_REP, batch, tdec, LANES)
    qs = q5.transpose(2, 0, 1, 3, 4).reshape(batch, 2 * NSA_REP, tdec, LANES)
    qs = jnp.concatenate([qs, jnp.zeros((batch, 2 * NSA_REP, 8 - tdec, LANES), BF16)], axis=2)
    qs = qs.reshape(batch, 2, NSA_REP * 8, LANES)
    ocmp_k, selm = _cmp_decode(pt_flat, nsa_cache, qs, batch=batch, n_pages=n_pages, past=past)

    def unstack_groups(o):
        o = o.reshape(batch, 2, NSA_REP, 8, LANES)[:, :, :, :tdec, :HEAD_DIM]
        return o.transpose(0, 3, 1, 2, 4).reshape(n, NSA_HEADS * HEAD_DIM)

    ocmp = unstack_groups(ocmp_k)

    qsw = jnp.stack([qs[:, 0], jnp.roll(qs[:, 1], HALF, axis=-1)], axis=1)
    win_new = _pad_rows(win.reshape(batch, tdec, 256), 128)
    owin_k = _win_decode(qsw, state_win, win_new, batch)
    ow = owin_k.reshape(batch, 2, NSA_REP, 8, 2, HEAD_DIM)[:, :, :, :tdec]
    owin = jnp.stack([ow[:, 0, :, :, 0], ow[:, 1, :, :, 1]], axis=1).transpose(0, 3, 1, 2, 4).reshape(n, 512)

    ppc = min(16, n_pages)
    nch = n_pages // ppc
    nsd = n_pages * page // SEL_BLOCK
    q6 = q8[0].reshape(NSA_KV_HEADS, NSA_REP, batch, tdec, LANES)[..., :HEAD_DIM]
    zeros = jnp.zeros_like(q6[0])
    qbd = jnp.stack([jnp.concatenate([q6[0], zeros], -1), jnp.concatenate([zeros, q6[1]], -1)], axis=0)
    qbd = qbd.transpose(2, 1, 0, 3, 4).reshape(batch, 32, LANES)
    sel8 = selm[:, :, :tdec, :].reshape(batch, 8, nsd + LANES)
    bpc = ppc * page // SEL_BLOCK
    selc = sel8[:, :, :nsd].reshape(batch, 8, nch, bpc).transpose(0, 2, 1, 3)
    selc = jnp.concatenate([selc, jnp.zeros((batch, nch, 8, LANES - bpc), F32)], axis=-1).astype(BF16)
    selnew = jnp.broadcast_to(sel8[:, :, nsd:nsd + 1], (batch, 8, LANES))
    key = np.arange(ppc * page)
    expand = (np.arange(LANES)[:, None] == (key // SEL_BLOCK)[None, :]).astype(np.float32)
    nsa_new = _pad_rows(nsa.reshape(batch, tdec, 512)[:, :, 256:], 128)
    osel_k = _attn_decode(pt_flat, nsa_cache, qbd, selc, selnew, jnp.asarray(expand, BF16), nsa_new,
                          batch=batch, n_pages=n_pages, col0=256, kw=LANES, fox=False, name="nsa_sel_decode")
    os_ = osel_k.reshape(batch, NSA_REP, 2, tdec, 2, HEAD_DIM)
    osel = jnp.stack([os_[:, :, 0, :, 0], os_[:, :, 1, :, 1]], axis=1).transpose(0, 3, 1, 2, 4).reshape(n, 512)

    logf_new = sm[:, SM_FC:SM_FC + FOX_HEADS].reshape(batch, tdec * FOX_HEADS)
    new_row = jnp.zeros((batch, 8, page * FOX_HEADS), F32).at[:, 0, :tdec * FOX_HEADS].set(logf_new)
    csum = _logf_decode(pt_flat, logf_cache, new_row, batch=batch, n_pages=n_pages)
    c4 = csum.reshape(batch, n_pages + 8, FOX_HEADS, page).transpose(0, 2, 1, 3).reshape(batch, FOX_HEADS, -1)
    ck8 = jnp.concatenate([-c4, jnp.zeros_like(c4)], axis=1)
    cq = jnp.concatenate([c4[:, :, past:past + tdec].transpose(0, 2, 1),
                          jnp.zeros((batch, tdec, 8 - FOX_HEADS), F32)], axis=2).reshape(batch, 32, 1)
    qf = (qc * SCALE).reshape(batch, tdec, FOX_HEADS, HEAD_DIM)
    eye = jnp.eye(8, FOX_HEADS, dtype=F32)
    qfbd = (qf[:, :, None, :, :] * eye[None, None, :, :, None]).reshape(batch, 32, FOX_HEADS * HEAD_DIM).astype(BF16)
    fox_new = _pad_rows(fox.reshape(batch, tdec, 512), 128)
    oc_k = _attn_decode(pt_flat, fox_cache, qfbd, ck8, ck8[:, :, past:past + LANES], cq, fox_new,
                        batch=batch, n_pages=n_pages, col0=0, kw=256, fox=True, name="fox_decode")
    o6 = oc_k.reshape(batch, tdec, 8, FOX_HEADS, HEAD_DIM)
    oc = jnp.stack([o6[:, :, h, h] for h in range(FOX_HEADS)], axis=2).reshape(n, 256)

    xo, y = _output(x2d, d, uz, ocmp, osel, owin, zb, oc, zc, gm, sm, lw, final_norm)
    win_seq = jnp.concatenate([state_win, win.reshape(batch, tdec, 256)], axis=1)
    pool_seq = jnp.concatenate([state_pool, uz[:, :POOL_WIDTH].reshape(batch, tdec, POOL_WIDTH)], axis=1)
    states = (nsa.reshape(batch, tdec, 4, NSA_KV_HEADS, HEAD_DIM),
              fox.reshape(batch, tdec, 2, FOX_HEADS, HEAD_DIM),
              sm[:, SM_FC:SM_FC + FOX_HEADS].reshape(batch, tdec, FOX_HEADS),
              win_seq[:, -min(WINDOW, win_seq.shape[1]):].reshape(batch, -1, 2, NSA_KV_HEADS, HEAD_DIM),
              pool_seq[:, -POOL_STATE:])
    return xo, y, states


def kernel(x_prompt, x_sample, cache_nsa_kv, cache_fox_kv, cache_fox_logf, state_win_kv, state_pool, page_table,
           norm_w, w_in, pool_w, pool_scale, fox_bf, w_out_a, w_out_b, w_out_c, w_o, final_norm):
    depth = w_in.shape[0]
    batch, seq, dm = x_prompt.shape
    dbatch, tdec, _ = x_sample.shape
    n_phys, page = cache_nsa_kv.shape[1], cache_nsa_kv.shape[2]
    n_pages = page_table.shape[1]
    assert page == 128 and tdec <= 4 and n_pages % min(16, n_pages) == 0
    assert state_win_kv.shape[2] == WINDOW

    nsa_cache = cache_nsa_kv.reshape(depth * n_phys, page, 512)
    fox_cache = cache_fox_kv.reshape(depth * n_phys, page, 512)
    logf_cache = cache_fox_logf.reshape(depth * n_phys, page * FOX_HEADS)

    hp = x_prompt.reshape(batch * seq, dm)
    hs = x_sample.reshape(dbatch * tdec, dm)
    st_p, st_s = [], []
    for layer in range(depth):
        lw = _prep_layer_weights(w_in[layer], pool_w[layer], pool_scale[layer], w_out_a[layer],
                                 w_out_b[layer], w_out_c[layer], w_o[layer])
        hp, yp, sp = _layer_prompt(hp, lw, norm_w[layer], fox_bf[layer], final_norm, batch, seq)
        pt_flat = (page_table + layer * n_phys).reshape(-1).astype(jnp.int32)
        hs, ys, ss = _layer_decode(hs, lw, norm_w[layer], fox_bf[layer], final_norm, pt_flat, nsa_cache, fox_cache,
                                   logf_cache, state_win_kv[layer].reshape(dbatch, WINDOW, 256), state_pool[layer],
                                   dbatch, tdec, n_pages, page)
        st_p.append(sp)
        st_s.append(ss)
    outs = [yp.reshape(batch, seq, dm), ys.reshape(dbatch, tdec, dm)]
    for k in range(5):
        outs.append(jnp.stack([s[k] for s in st_p], 0))
        outs.append(jnp.stack([s[k] for s in st_s], 0))
    return tuple(outs)
```

```python
import functools

import numpy as np
import jax
import jax.numpy as jnp
from jax import lax
from jax.experimental import pallas as pl
from jax.experimental.pallas import tpu as pltpu

F32 = jnp.float32
BF16 = jnp.bfloat16

HEAD_DIM = 64
POOL_WIDTH = 256
POOL_WINDOWS = (2, 4, 8, 16)
POOL_STATE = 15
NSA_HEADS = 8
NSA_KV_HEADS = 2
NSA_REP = NSA_HEADS // NSA_KV_HEADS
CMP_BLOCK = 32
SEL_BLOCK = 64
SEL_TOPK = 16
WINDOW = 512
FORCE_SCORE = 1e4
FOX_HEADS = 4
ROPE_THETA = 10000.0
RMS_EPS = 1e-6
NEG_INF = -1e30
SCALE = HEAD_DIM ** -0.5

LANES = 128
HALF = LANES // 2
PAGE = 128
VMEM_LIMIT = 56 * 1024 * 1024

C_UZ, C_QB, C_ZB, C_QC, C_ZC, C_GM, C_SM, C_END = 0, 512, 1024, 1536, 1792, 2048, 5120, 5248
R_KV, R_KC, R_VC, R_FC, R_END = 0, 768, 1024, 1280, 1288
SM_GB = 0
TKV = 256


def _cparams(sem):
    return pltpu.CompilerParams(dimension_semantics=sem, vmem_limit_bytes=VMEM_LIMIT)


def _lane(shape):
    return lax.broadcasted_iota(jnp.int32, shape, len(shape) - 1)


def _row(shape):
    return lax.broadcasted_iota(jnp.int32, shape, len(shape) - 2)


def _split3(x):
    hi = x.astype(BF16)
    r1 = x - hi.astype(F32)
    mid = r1.astype(BF16)
    lo = (r1 - mid.astype(F32)).astype(BF16)
    return hi, mid, lo


def _dot(a, b):
    return jnp.dot(a, b, preferred_element_type=F32)


def _dot_nt(a, b):
    return lax.dot_general(a, b, (((1,), (1,)), ((), ())), preferred_element_type=F32)


def _dot_tn(a, b):
    return lax.dot_general(a, b, (((0,), (0,)), ((), ())), preferred_element_type=F32)


def _dot3(x, m):
    n = x.shape[0]
    r = _dot(jnp.concatenate(_split3(x), axis=0), m)
    return r[0:n] + r[n:2 * n] + r[2 * n:3 * n]


def _dot3_left(m, x):
    n = x.shape[1]
    r = _dot(m, jnp.concatenate(_split3(x), axis=1))
    return r[:, 0:n] + r[:, n:2 * n] + r[:, 2 * n:3 * n]


def _swap_halves(x):
    return pltpu.roll(x, HALF, x.ndim - 1)


def _to_low_half(x, g, fill):
    lo = _lane(x.shape) < HALF
    return jnp.where(lo, x if g == 0 else _swap_halves(x), fill)


def _pair_heads(pieces):
    out = []
    for j in range(0, len(pieces), 2):
        lo = _lane(pieces[j].shape) < HALF
        out.append(jnp.where(lo, pieces[j], pieces[j + 1]))
    return out


def _masked_softmax(s, ok, axis):
    s = jnp.where(ok, s, NEG_INF)
    m = jnp.max(s, axis=axis, keepdims=True)
    e = jnp.where(ok, jnp.exp(s - m), 0.0)
    return e / jnp.maximum(jnp.sum(e, axis=axis, keepdims=True), 1e-30)


def _wprep_kernel(wt_ref, w_ref):
    w_ref[...] = wt_ref[...].T.astype(BF16)


def _wprep(wt):
    cols, d = wt.shape
    return pl.pallas_call(
        _wprep_kernel, grid=(cols // LANES,),
        in_specs=[pl.BlockSpec((LANES, d), lambda i: (i, 0))],
        out_specs=pl.BlockSpec((d, LANES), lambda i: (0, i)),
        out_shape=jax.ShapeDtypeStruct((d, cols), BF16),
        compiler_params=_cparams(("arbitrary",)), name="w_in_transpose",
    )(wt)


def _rope_tile(x, cosf, sinf):
    lane = _lane(x.shape)
    first = (lane % HEAD_DIM) < (HEAD_DIM // 2)
    rot = jnp.where(first, pltpu.roll(x, LANES - HEAD_DIM // 2, 1), pltpu.roll(x, HEAD_DIM // 2, 1))
    return x * cosf + rot * sinf


def _rope_rows(x, cos_t, sin_t):
    half = HEAD_DIM // 2
    out = []
    for h in range(x.shape[0] // HEAD_DIM):
        x1 = x[h * HEAD_DIM:h * HEAD_DIM + half]
        x2 = x[h * HEAD_DIM + half:(h + 1) * HEAD_DIM]
        out += [x1 * cos_t - x2 * sin_t, x2 * cos_t + x1 * sin_t]
    return jnp.concatenate(out, axis=0)


def _proj_kernel(*refs, prompt, tiles_per_seq):
    if prompt:
        (x_ref, nw_ref, wt_ref, wk_ref, cos_ref, sin_ref, cost_ref, sint_ref, bfc_ref,
         ohb_ref, amean_ref, triu_ref, sq_ref,
         uz_ref, zb_ref, zc_ref, gm_ref, sm_ref, q8_ref, nsat_ref, wint_ref, foxt_ref, logft_ref,
         fq_ref, kcm_ref, ksel_ref, vsel_ref, kwin_ref, vwin_ref, fk_ref, fv_ref, carry_ref) = refs
    else:
        (x_ref, nw_ref, wt_ref, wk_ref, cos_ref, sin_ref, cost_ref, sint_ref, bfc_ref,
         uz_ref, zb_ref, zc_ref, gm_ref, sm_ref, q8_ref, nsat_ref, wint_ref, foxt_ref, logft_ref, qc_ref) = refs

    x = x_ref[...]
    ms = jnp.mean(x * x, axis=-1, keepdims=True)
    hb = ((x * lax.rsqrt(ms + RMS_EPS)) * nw_ref[...]).astype(BF16)
    tm = x.shape[0]

    def seg(c0, n):
        return _dot(hb, wt_ref[:, c0:c0 + n])

    def seg_t(r0, n):
        return _dot_nt(wk_ref[r0:r0 + n, :], hb)

    uz_ref[...] = seg(C_UZ, 512)
    cosf = cos_ref[...]
    sinf = sin_ref[...]
    for c in range(4):
        y = _rope_tile(seg(C_QB + c * LANES, LANES), cosf, sinf) * SCALE
        q8_ref[0, 2 * c] = _to_low_half(y, 0, 0.0).astype(BF16)
        q8_ref[0, 2 * c + 1] = _to_low_half(y, 1, 0.0).astype(BF16)
    zb_ref[...] = seg(C_ZB, 512)
    qc = seg(C_QC, 256)
    zc_ref[...] = seg(C_ZC, 256)
    for c in range(3):
        gm_ref[:, c * 1024:(c + 1) * 1024] = seg(C_GM + c * 1024, 1024)
    sm_ref[...] = seg(C_SM, LANES)

    cos_t = cost_ref[...]
    sin_t = sint_ref[...]
    kcmp = _rope_rows(seg_t(R_KV, LANES), cos_t, sin_t)
    vcmp = seg_t(R_KV + 128, LANES)
    ksel = _rope_rows(seg_t(R_KV + 256, LANES), cos_t, sin_t)
    vsel = seg_t(R_KV + 384, LANES)
    kwin = _rope_rows(seg_t(R_KV + 512, LANES), cos_t, sin_t)
    vwin = seg_t(R_KV + 640, LANES)
    nsat_ref[0, 0:128] = kcmp
    nsat_ref[0, 128:256] = vcmp
    nsat_ref[0, 256:384] = ksel
    nsat_ref[0, 384:512] = vsel
    wint_ref[0, 0:128] = kwin
    wint_ref[0, 128:256] = vwin
    kc = seg_t(R_KC, 256)
    vc = seg_t(R_VC, 256)
    foxt_ref[0, 0:256] = kc
    foxt_ref[0, 256:512] = vc
    f_t = seg_t(R_FC, 8)
    logf = jnp.where(_row(f_t.shape) < FOX_HEADS, jax.nn.log_sigmoid(f_t + bfc_ref[...]), 0.0)
    logft_ref[0] = logf

    if not prompt:
        qc_ref[...] = qc
        return

    kcm_ref[0, 0] = _dot3(jnp.concatenate([kcmp, vcmp], axis=0), amean_ref[...])
    ohb = ohb_ref[...]
    for g in range(2):
        gs = slice(g * HEAD_DIM, (g + 1) * HEAD_DIM)
        ksel_ref[0, g, 0] = jnp.concatenate([ksel[gs].astype(BF16), ohb], axis=0)
        vsel_ref[0, g, 0] = jnp.concatenate([vsel[gs]] * 2, axis=0).astype(BF16)
        kwin_ref[0, g, 0] = jnp.concatenate([kwin[gs]] * 2, axis=0).astype(BF16)
        vwin_ref[0, g, 0] = jnp.concatenate([vwin[gs]] * 2, axis=0).astype(BF16)

    i = pl.program_id(0)
    carry = jnp.where(i % tiles_per_seq == 0, 0.0, carry_ref[:, 0:1])
    csum = _dot3(logf, triu_ref[...]) + carry
    carry_ref[...] = jnp.broadcast_to(csum[:, tm - 1:tm], carry_ref.shape)
    c_hi, c_mid, c_lo = [p.astype(F32) for p in _split3(csum)]
    r8 = _row((8, tm))
    c_tok = jnp.concatenate([csum, jnp.zeros((LANES - 8, tm), F32)], axis=0).T
    cq = _dot(jnp.concatenate(_split3(c_tok), axis=1), sq_ref[...])
    lane = _lane((tm, LANES))
    one_q = ((lane >= HALF + 3) & (lane < HALF + 6)).astype(F32)
    for h in range(FOX_HEADS):
        t, g = h // 2, h % 2
        hs = slice(h * HEAD_DIM, (h + 1) * HEAD_DIM)
        aug = jnp.where(r8 < 3, 1.0, jnp.where(r8 == 3, -c_hi[h:h + 1], jnp.where(
            r8 == 4, -c_mid[h:h + 1], jnp.where(r8 == 5, -c_lo[h:h + 1], 0.0))))
        fk_ref[0, h, 0] = jnp.concatenate([kc[hs], aug, jnp.zeros((HALF - 8, tm), F32)], axis=0).astype(BF16)
        fv_ref[0, h, 0] = jnp.concatenate([vc[hs]] * 2, axis=0).astype(BF16)
        fq_ref[0, h] = _to_low_half(qc[:, t * LANES:(t + 1) * LANES] * SCALE, g,
                                    cq[:, h * LANES:(h + 1) * LANES] + one_q).astype(BF16)


def _rope_tables(pos):
    half = HEAD_DIM // 2
    inv = ROPE_THETA ** (-jnp.arange(half, dtype=F32) / half)
    ang = pos.astype(F32)[:, None] * inv[None, :]
    cos, sin = jnp.cos(ang), jnp.sin(ang)
    return jnp.tile(cos, (1, 4)), jnp.tile(jnp.concatenate([-sin, sin], 1), (1, 2)), cos.T, sin.T


def _fox_query_aug_matrix():
    sq = np.zeros((3 * LANES, FOX_HEADS * LANES), np.float32)
    for h in range(FOX_HEADS):
        for p in range(3):
            sq[p * LANES + h, h * LANES + HALF + p] = 1.0
    return jnp.asarray(sq, BF16)


def _project(x2d, norm_w, w_tok, w_feat, fox_bf, pos, *, prompt, batch, seq):
    n, d = x2d.shape
    tm = TKV if prompt else n
    tiles_per_seq = seq // tm if prompt else 1
    cosf, sinf, cos_t, sin_t = _rope_tables(pos)
    bf_col = jnp.zeros((8, 1), F32).at[0:FOX_HEADS, 0].set(fox_bf)
    row = lambda w: pl.BlockSpec((tm, w), lambda i: (i, 0))
    full = lambda a: pl.BlockSpec(a.shape, lambda i: (0,) * a.ndim)
    if prompt:
        tab = pl.BlockSpec((tm, LANES), lambda i: (i % tiles_per_seq, 0))
        tab_t = pl.BlockSpec((HEAD_DIM // 2, tm), lambda i: (0, i % tiles_per_seq))
    else:
        tab = row(LANES)
        tab_t = pl.BlockSpec((HEAD_DIM // 2, tm), lambda i: (0, i))
    in_arrays = [x2d, norm_w.reshape(1, d), w_tok, w_feat, cosf, sinf, cos_t, sin_t, bf_col]
    in_specs = [row(d), full(in_arrays[1]), full(w_tok), full(w_feat), tab, tab, tab_t, tab_t, full(bf_col)]
    nb = batch if prompt else 1
    bidx = lambda i: (i // tiles_per_seq, 0, i % tiles_per_seq)
    feat = lambda r: pl.BlockSpec((1, r, tm), bidx)
    hb = lambda nh: pl.BlockSpec((1, nh, tm, LANES), lambda i: (i // tiles_per_seq, 0, i % tiles_per_seq, 0))
    nt = (seq if prompt else n)
    out_shapes = [
        jax.ShapeDtypeStruct((n, 512), F32), jax.ShapeDtypeStruct((n, 512), F32),
        jax.ShapeDtypeStruct((n, 256), F32), jax.ShapeDtypeStruct((n, 3072), F32),
        jax.ShapeDtypeStruct((n, LANES), F32), jax.ShapeDtypeStruct((nb, NSA_HEADS, nt, LANES), BF16),
        jax.ShapeDtypeStruct((nb, 512, nt), F32), jax.ShapeDtypeStruct((nb, 256, nt), F32),
        jax.ShapeDtypeStruct((nb, 512, nt), F32), jax.ShapeDtypeStruct((nb, 8, nt), F32),
    ]
    out_specs = [row(512), row(512), row(256), row(3072), row(LANES), hb(NSA_HEADS),
                 feat(512), feat(256), feat(512), feat(8)]
    scratch = []
    if prompt:
        t = np.arange(seq)
        ohb = np.zeros((HALF, seq), np.float32)
        ohb[t // SEL_BLOCK, t] = -NEG_INF
        tt = np.arange(tm)
        amean = np.zeros((tm, LANES), np.float32)
        amean[tt, tt // CMP_BLOCK] = 1.0 / CMP_BLOCK
        triu = np.triu(np.ones((tm, tm), np.float32))
        extra = [jnp.asarray(ohb, BF16), jnp.asarray(amean, BF16), jnp.asarray(triu, BF16), _fox_query_aug_matrix()]
        in_arrays += extra
        in_specs += [pl.BlockSpec((HALF, tm), lambda i: (0, i % tiles_per_seq)),
                     full(extra[1]), full(extra[2]), full(extra[3])]
        nkt = seq // tm
        kvt = lambda ns: pl.BlockSpec((1, ns, 1, LANES, tm),
                                      lambda i: (i // tiles_per_seq, 0, i % tiles_per_seq, 0, 0))
        out_shapes.append(jax.ShapeDtypeStruct((batch, FOX_HEADS, seq, LANES), BF16))
        out_specs.append(hb(FOX_HEADS))
        out_shapes.append(jax.ShapeDtypeStruct((batch, nkt, 256, LANES), F32))
        out_specs.append(pl.BlockSpec((1, 1, 256, LANES), lambda i: (i // tiles_per_seq, i % tiles_per_seq, 0, 0)))
        for ns in (2, 2, 2, 2, FOX_HEADS, FOX_HEADS):
            out_shapes.append(jax.ShapeDtypeStruct((batch, ns, nkt, LANES, tm), BF16))
            out_specs.append(kvt(ns))
        scratch = [pltpu.VMEM((8, LANES), F32)]
    else:
        out_shapes.append(jax.ShapeDtypeStruct((n, 256), F32))
        out_specs.append(row(256))
    return pl.pallas_call(
        functools.partial(_proj_kernel, prompt=prompt, tiles_per_seq=tiles_per_seq),
        grid=(n // tm,), in_specs=in_specs, out_specs=out_specs, out_shape=out_shapes,
        scratch_shapes=scratch, compiler_params=_cparams(("arbitrary",)),
        name="proj_prompt" if prompt else "proj_decode",
    )(*in_arrays)


def _pool_mix(u, prev, pos):
    grp = _lane(u.shape) // (POOL_WIDTH // len(POOL_WINDOWS))
    acc = u
    win = None
    for j in range(1, POOL_WINDOWS[-1]):
        acc = acc + prev(j)
        if j + 1 in POOL_WINDOWS:
            k = POOL_WINDOWS.index(j + 1)
            win = acc if win is None else jnp.where(grp >= k, acc, win)
    w = jnp.zeros(u.shape, jnp.int32)
    for k, wk in enumerate(POOL_WINDOWS):
        w = jnp.where(grp == k, wk, w)
    cnt = jnp.minimum(pos + 1, w).astype(F32)
    return win / cnt - u


def _pool_prompt_kernel(u_ref, halo_ref, d_ref, s_ref, *, tiles_per_seq, tm):
    i = pl.program_id(0) % tiles_per_seq
    s_ref[0:16, :] = jnp.where(i == 0, 0.0, halo_ref[...])
    s_ref[16:, :] = u_ref[...]
    pos = i * tm + _row((tm, POOL_WIDTH))
    d_ref[...] = _pool_mix(u_ref[...], lambda j: s_ref[16 - j:16 - j + tm, :], pos)


def _pool_prompt(uz, seq):
    n = uz.shape[0]
    tm = 512
    tiles_per_seq = seq // tm
    return pl.pallas_call(
        functools.partial(_pool_prompt_kernel, tiles_per_seq=tiles_per_seq, tm=tm),
        grid=(n // tm,),
        in_specs=[pl.BlockSpec((tm, POOL_WIDTH), lambda i: (i, 0)),
                  pl.BlockSpec((16, POOL_WIDTH), lambda i: (jnp.maximum(i * (tm // 16) - 1, 0), 0))],
        out_specs=pl.BlockSpec((tm, POOL_WIDTH), lambda i: (i, 0)),
        out_shape=jax.ShapeDtypeStruct((n, POOL_WIDTH), F32),
        scratch_shapes=[pltpu.VMEM((tm + 16, POOL_WIDTH), F32)],
        compiler_params=_cparams(("arbitrary",)), name="pool_prompt",
    )(uz, uz)


def _pool_decode_kernel(seq_ref, d_ref, *, pos0, tdec):
    for t in range(tdec):
        pos = jnp.full(seq_ref.shape[1:], pos0 + t, jnp.int32)
        d_ref[t] = _pool_mix(seq_ref[POOL_STATE + t], lambda j: seq_ref[POOL_STATE + t - j], pos)


def _pool_decode(seq_tb, pos0, tdec):
    _, b, w = seq_tb.shape
    return pl.pallas_call(
        functools.partial(_pool_decode_kernel, pos0=pos0, tdec=tdec),
        grid=(1,),
        in_specs=[pl.BlockSpec(seq_tb.shape, lambda i: (0, 0, 0))],
        out_specs=pl.BlockSpec((tdec, b, w), lambda i: (0, 0, 0)),
        out_shape=jax.ShapeDtypeStruct((tdec, b, w), F32),
        compiler_params=_cparams(("arbitrary",)), name="pool_decode",
    )(seq_tb)


def _select_rank(imp, blk, cur, k_top):
    forced = (blk == 0) | (blk == cur) | (blk == cur - 1)
    val = jnp.where(blk <= cur, imp + jnp.where(forced, FORCE_SCORE, 0.0), -1.0)
    cnt = jnp.zeros(val.shape, jnp.int32)
    for i in range(val.shape[0]):
        vi = val[i:i + 1, :]
        cnt = cnt + ((vi > val) | ((vi == val) & (blk > i))).astype(jnp.int32)
    return (cnt < k_top) & (blk <= cur)


def _cmp_prompt_kernel(kc_ref, vc_ref, q_ref, o_ref, qaug_ref, *, tq, seq):
    qi = pl.program_id(2)
    ns = seq // SEL_BLOCK
    qs = jnp.concatenate([q_ref[0, r] for r in range(NSA_REP)], axis=0)
    s = _dot_nt(kc_ref[0, 0], qs)
    row = _row(s.shape)
    nblk = jnp.where(row < ns, 2 * row, 2 * (row - ns) + 1)
    t = qi * tq + _lane(s.shape) % tq
    p = _masked_softmax(s, (nblk + 1) * CMP_BLOCK - 1 <= t, 0)
    o = _dot_tn(p.astype(BF16), vc_ref[0, 0])
    pieces = _pair_heads([o[r * tq:(r + 1) * tq] for r in range(NSA_REP)])
    for j, piece in enumerate(pieces):
        o_ref[0, :, j * LANES:(j + 1) * LANES] = piece

    imp = p[:, 0:tq]
    for r in range(1, NSA_REP):
        imp = imp + p[:, r * tq:(r + 1) * tq]
    imp = imp[:ns] + imp[ns:]
    blk = _row(imp.shape)
    cur = (qi * tq + _lane(imp.shape)) // SEL_BLOCK
    sel = _select_rank(imp, blk, cur, min(SEL_TOPK, ns))
    selm = jnp.where(sel, 0.0, -1.0)
    pad = [jnp.zeros((HALF, tq), F32), selm]
    if ns < HALF:
        pad.append(jnp.zeros((HALF - ns, tq), F32))
    sel_t = jnp.concatenate(pad, axis=0).T
    for r in range(NSA_REP):
        qaug_ref[0, 0, 0, r * tq:(r + 1) * tq, :] = (q_ref[0, r].astype(F32) + sel_t).astype(BF16)


def _cmp_prompt(kc_dup, vc_dup, q8, batch, seq):
    tq = LANES
    nq = seq // tq
    nb = seq // CMP_BLOCK
    assert seq % (2 * LANES) == 0 and seq // SEL_BLOCK <= HALF
    kv = pl.BlockSpec((1, 1, nb, LANES), lambda b, g, q: (b, g, 0, 0))
    return pl.pallas_call(
        functools.partial(_cmp_prompt_kernel, tq=tq, seq=seq),
        grid=(batch, NSA_KV_HEADS, nq),
        in_specs=[kv, kv, pl.BlockSpec((1, NSA_REP, tq, LANES), lambda b, g, q: (b, g, q, 0))],
        out_specs=[pl.BlockSpec((1, tq, 256), lambda b, g, q: (b, q, g)),
                   pl.BlockSpec((1, 1, 1, NSA_REP * tq, LANES), lambda b, g, q: (b, g, q, 0, 0))],
        out_shape=[jax.ShapeDtypeStruct((batch, seq, 512), F32),
                   jax.ShapeDtypeStruct((batch, NSA_KV_HEADS, nq, NSA_REP * tq, LANES), BF16)],
        compiler_params=_cparams(("arbitrary", "arbitrary", "arbitrary")), name="nsa_cmp_prompt",
    )(kc_dup, vc_dup, q8)


def _flash_kernel(q_ref, k_ref, v_ref, o_ref, *, n_sub, n_stack, tpt, tk):
    qi = pl.program_id(2)
    q0 = qi * tpt
    nfull = q0 // tk
    rows = n_stack * tpt
    tps = tk // TKV
    pieces = []
    for u in range(n_sub):
        q = q_ref[0, u, 0]

        def step(j, carry, masked):
            m, l, acc = carry
            s = jnp.concatenate([_dot(q, k_ref[0, u, j * tps + i]) for i in range(tps)], axis=1)
            if masked:
                key = j * tk + _lane(s.shape)
                tpos = q0 + _row(s.shape) % tpt
                s = jnp.where(key <= tpos, s, NEG_INF)
            mn = jnp.maximum(m, jnp.max(s, axis=-1, keepdims=True))
            a = jnp.exp(m - mn)
            p = jnp.exp(s - mn)
            pb = p.astype(BF16)
            pv = _dot_nt(pb[:, 0:TKV], v_ref[0, u, j * tps])
            for i in range(1, tps):
                pv = pv + _dot_nt(pb[:, i * TKV:(i + 1) * TKV], v_ref[0, u, j * tps + i])
            return mn, a * l + jnp.sum(p, axis=-1, keepdims=True), a * acc + pv

        init = (jnp.full((rows, 1), -1e38, F32), jnp.zeros((rows, 1), F32), jnp.zeros((rows, LANES), F32))
        carry = lax.fori_loop(0, nfull, lambda j, c: step(j, c, False), init)
        _, l, acc = step(nfull, carry, True)
        o = acc / l
        pieces += [o[r * tpt:(r + 1) * tpt] for r in range(n_stack)]
    for j, piece in enumerate(_pair_heads(pieces)):
        o_ref[0, :, j * LANES:(j + 1) * LANES] = piece


def _flash(q_aug, k_aug, v_dup, *, n_sub, n_stack, tpt, tk, name):
    b, s, nq, rows, _ = q_aug.shape
    nkt = k_aug.shape[2]
    seq = nkt * TKV
    assert tk % tpt == 0 and seq % tk == 0 and tk % TKV == 0 and rows == n_stack * tpt
    wout = n_sub * n_stack * HEAD_DIM
    kv = pl.BlockSpec((1, n_sub, nkt, LANES, TKV), lambda bb, ss, q: (bb, ss, 0, 0, 0))
    return pl.pallas_call(
        functools.partial(_flash_kernel, n_sub=n_sub, n_stack=n_stack, tpt=tpt, tk=tk),
        grid=(b, s // n_sub, nq),
        in_specs=[pl.BlockSpec((1, n_sub, 1, rows, LANES), lambda bb, ss, q: (bb, ss, q, 0, 0)), kv, kv],
        out_specs=pl.BlockSpec((1, tpt, wout), lambda bb, ss, q: (bb, q, ss)),
        out_shape=jax.ShapeDtypeStruct((b, seq, s * n_stack * HEAD_DIM), F32),
        compiler_params=_cparams(("arbitrary", "arbitrary", "arbitrary")), name=name,
    )(q_aug, k_aug, v_dup)


def _win_prompt_kernel(q_ref, k_ref, v_ref, o_ref, *, tq, nwt):
    qi = pl.program_id(2)
    q0 = qi * tq
    ts = jnp.maximum(qi - (nwt - 1), 0)
    qs = jnp.concatenate([q_ref[0, r] for r in range(NSA_REP)], axis=0)
    s = jnp.concatenate([_dot(qs, k_ref[0, 0, ts + i]) for i in range(nwt)], axis=1)
    key = ts * TKV + _lane(s.shape)
    t = q0 + _row(s.shape) % tq
    p = _masked_softmax(s, (key <= t) & (key > t - WINDOW), -1).astype(BF16)
    o = _dot_nt(p[:, 0:TKV], v_ref[0, 0, ts])
    for i in range(1, nwt):
        o = o + _dot_nt(p[:, i * TKV:(i + 1) * TKV], v_ref[0, 0, ts + i])
    for j, piece in enumerate(_pair_heads([o[r * tq:(r + 1) * tq] for r in range(NSA_REP)])):
        o_ref[0, :, j * LANES:(j + 1) * LANES] = piece


def _win_prompt(q8, kwin, vwin, batch, seq):
    tq = TKV
    nkt = seq // TKV
    nwt = WINDOW // TKV + 1
    assert nkt >= nwt
    kv = pl.BlockSpec((1, 1, nkt, LANES, TKV), lambda b, g, q: (b, g, 0, 0, 0))
    return pl.pallas_call(
        functools.partial(_win_prompt_kernel, tq=tq, nwt=nwt),
        grid=(batch, NSA_KV_HEADS, seq // tq),
        in_specs=[pl.BlockSpec((1, NSA_REP, tq, LANES), lambda b, g, q: (b, g, q, 0)), kv, kv],
        out_specs=pl.BlockSpec((1, tq, 256), lambda b, g, q: (b, q, g)),
        out_shape=jax.ShapeDtypeStruct((batch, seq, 512), F32),
        compiler_params=_cparams(("arbitrary", "arbitrary", "arbitrary")), name="nsa_win_prompt",
    )(q8, kwin, vwin)


def _out_kernel(x_ref, d_ref, za_ref, ocmp_ref, osel_ref, owin_ref, zb_ref, oc_ref, zc_ref, gm_ref, sm_ref,
                pw_ref, ps_ref, wa_ref, wb_ref, wc_ref, wo_ref, e3_ref, fn_ref, xo_ref, y_ref):
    oa = _dot(d_ref[...].astype(BF16), pw_ref[...]) * ps_ref[...]
    ya = _dot((oa * jax.nn.silu(za_ref[...])).astype(BF16), wa_ref[...])
    gx = _dot3(jax.nn.sigmoid(sm_ref[...]), e3_ref[...])
    ob = gx[:, 0:512] * ocmp_ref[...] + gx[:, 512:1024] * osel_ref[...] + gx[:, 1024:1536] * owin_ref[...]
    yb = _dot((ob * jax.nn.silu(zb_ref[...])).astype(BF16), wb_ref[...])
    yc = _dot((oc_ref[...] * jax.nn.silu(zc_ref[...])).astype(BF16), wc_ref[...])
    d = ya.shape[-1]
    mixed = (jax.nn.sigmoid(gm_ref[:, 0:d]) * ya + jax.nn.sigmoid(gm_ref[:, d:2 * d]) * yb
             + jax.nn.sigmoid(gm_ref[:, 2 * d:3 * d]) * yc)
    xo = x_ref[...] + _dot(mixed.astype(BF16), wo_ref[...])
    xo_ref[...] = xo
    ms = jnp.mean(xo * xo, axis=-1, keepdims=True)
    y_ref[...] = (xo * lax.rsqrt(ms + RMS_EPS)) * fn_ref[...]


def _gate_expand_matrix():
    e = np.zeros((LANES, 3 * NSA_HEADS * HEAD_DIM), np.float32)
    for hd in range(NSA_HEADS):
        for c in range(3):
            e[SM_GB + hd * 3 + c, c * 512 + hd * HEAD_DIM:c * 512 + (hd + 1) * HEAD_DIM] = 1.0
    return jnp.asarray(e, BF16)


def _output(x2d, d, uz, ocmp, osel, owin, zb, oc, zc, gm, sm, lw, final_norm):
    n, dm = x2d.shape
    tm = min(256, n)
    row = lambda w, c=0: pl.BlockSpec((tm, w), lambda i: (i, c))
    full = lambda a: pl.BlockSpec(a.shape, lambda i: (0,) * a.ndim)
    consts = [lw["pool_bd"], lw["pool_scale"], lw["w_out_a"], lw["w_out_b"], lw["w_out_c"], lw["w_o"],
              _gate_expand_matrix(), final_norm.reshape(1, dm)]
    return pl.pallas_call(
        _out_kernel, grid=(n // tm,),
        in_specs=[row(dm), row(256), row(256, 1), row(512), row(512), row(512), row(512), row(256), row(256),
                  row(3072), row(LANES)] + [full(c) for c in consts],
        out_specs=[row(dm), row(dm)],
        out_shape=[jax.ShapeDtypeStruct((n, dm), F32), jax.ShapeDtypeStruct((n, dm), F32)],
        compiler_params=_cparams(("arbitrary",)), name="out_proj",
    )(x2d, d, uz, ocmp, osel, owin, zb, oc, zc, gm, sm, *consts)


def _page_copies(cache_ref, pt_ref, buf_ref, sem_ref, b, c, slot, *, n_pages, ppc, row0, nrows):
    base = b * n_pages + c * ppc
    return [pltpu.make_async_copy(cache_ref.at[pt_ref[base + i], pl.ds(row0, nrows), :],
                                  buf_ref.at[slot, pl.ds(0, nrows), pl.ds(i * PAGE, PAGE)],
                                  sem_ref.at[slot]) for i in range(ppc)]


def _stream_pages(cache_ref, pt_ref, buf_ref, sem_ref, phases, *, n_pages, nch, ppc):
    nph = len(phases)
    steps_per_b = nph * nch
    step = pl.program_id(0) * steps_per_b + pl.program_id(1)
    total = pl.num_programs(0) * steps_per_b
    slot = step % 2

    def issue(st, sl, wait):
        b = st // steps_per_b
        r = st % steps_per_b
        for ph, (row0, nrows) in enumerate(phases):
            @pl.when(r // nch == ph)
            def _():
                for cp in _page_copies(cache_ref, pt_ref, buf_ref, sem_ref, b, r % nch, sl,
                                       n_pages=n_pages, ppc=ppc, row0=row0, nrows=nrows):
                    cp.wait() if wait else cp.start()

    @pl.when(step == 0)
    def _():
        issue(step, slot, False)

    @pl.when(step + 1 < total)
    def _():
        issue(step + 1, 1 - slot, False)

    issue(step, slot, True)
    return slot


def _topk_extract(val, k_top):
    lane = _lane(val.shape)
    sel = jnp.zeros(val.shape, jnp.bool_)
    for _ in range(k_top):
        mx = jnp.max(val, axis=-1, keepdims=True)
        idx = jnp.min(jnp.where(val == mx, lane, val.shape[-1]), axis=-1, keepdims=True)
        pick = lane == idx
        sel = sel | pick
        val = jnp.where(pick, -3e38, val)
    return sel


def _cmp_decode_kernel(pt_ref, cache_ref, q_ref, amean_ref, pair_ref, o_ref, sel_ref, buf_ref, sem_ref, kcs_ref,
                       *, n_pages, nch, ppc, past):
    c = pl.program_id(1)
    slot = _stream_pages(cache_ref, pt_ref, buf_ref, sem_ref, [(0, 256)], n_pages=n_pages, nch=nch, ppc=ppc)
    kcs_ref[c] = _dot3(buf_ref[slot], amean_ref[...])

    @pl.when(c == nch - 1)
    def _():
        nb = n_pages * (PAGE // CMP_BLOCK)
        nsd = nb // 2
        bpc = ppc * (PAGE // CMP_BLOCK)
        means = jnp.concatenate([kcs_ref[cc][:, 0:bpc] for cc in range(nch)], axis=1).astype(BF16)
        for g in range(2):
            s = _dot(q_ref[0, g], means[0:LANES])
            nblk = _lane(s.shape)
            qpos = past + _row(s.shape) % 8
            p = _masked_softmax(s, (nblk + 1) * CMP_BLOCK - 1 <= qpos, -1)
            o_ref[0, g] = _dot_nt(p.astype(BF16), means[LANES:2 * LANES])
            imp = p[0:8]
            for r in range(1, NSA_REP):
                imp = imp + p[r * 8:(r + 1) * 8]
            imp = jnp.concatenate([_dot3(imp, pair_ref[...]), jnp.zeros((8, LANES), F32)], axis=1)
            blk = _lane(imp.shape)
            cur = (past + _row(imp.shape)) // SEL_BLOCK
            forced = (blk == 0) | (blk == cur) | (blk == cur - 1)
            val = jnp.where(blk <= cur, imp + jnp.where(forced, FORCE_SCORE, 0.0), -1.0)
            val = jnp.where(blk <= nsd, val, -3e38)
            sel = _topk_extract(val, min(SEL_TOPK, nsd + 1)) & (blk <= cur)
            sel_ref[0, g] = jnp.where(sel, 0.0, -1.0)


def _cmp_decode(pt_flat, cache, qsw, *, batch, n_pages, past):
    ppc = min(32, n_pages)
    nch = n_pages // ppc
    bpc = ppc * (PAGE // CMP_BLOCK)
    assert bpc % LANES == 0 or nch == 1
    nb = n_pages * (PAGE // CMP_BLOCK)
    nsd = nb // 2
    tok = np.arange(ppc * PAGE)
    amean = np.zeros((ppc * PAGE, LANES), np.float32)
    amean[tok, tok // CMP_BLOCK] = 1.0 / CMP_BLOCK
    pair = (np.arange(nb)[:, None] // 2 == np.arange(nsd)[None, :]).astype(np.float32)
    consts = [jnp.asarray(amean, BF16), jnp.asarray(pair, BF16)]
    grid_spec = pltpu.PrefetchScalarGridSpec(
        num_scalar_prefetch=1, grid=(batch, nch),
        in_specs=[pl.BlockSpec(memory_space=pl.ANY),
                  pl.BlockSpec((1, 2, 32, LANES), lambda b, c, pt: (b, 0, 0, 0))]
                 + [pl.BlockSpec(a.shape, lambda b, c, pt: (0, 0)) for a in consts],
        out_specs=[pl.BlockSpec((1, 2, 32, LANES), lambda b, c, pt: (b, 0, 0, 0)),
                   pl.BlockSpec((1, 2, 8, nsd + LANES), lambda b, c, pt: (b, 0, 0, 0))],
        scratch_shapes=[pltpu.VMEM((2, 256, ppc * PAGE), F32), pltpu.SemaphoreType.DMA((2,)),
                        pltpu.VMEM((nch, 256, LANES), F32)])
    return pl.pallas_call(
        functools.partial(_cmp_decode_kernel, n_pages=n_pages, nch=nch, ppc=ppc, past=past),
        grid_spec=grid_spec,
        out_shape=[jax.ShapeDtypeStruct((batch, 2, 32, LANES), F32),
                   jax.ShapeDtypeStruct((batch, 2, 8, nsd + LANES), F32)],
        compiler_params=_cparams(("arbitrary", "arbitrary")), name="nsa_cmp_decode",
    )(pt_flat, cache, qsw, *consts)


def _attn_decode_kernel(pt_ref, cache_ref, q_ref, bias_ref, bnew_ref, aux_ref, new_ref, o_ref,
                        buf_ref, sem_ref, s_ref, acc_ref, *, n_pages, nch, ppc, krow, vrow, kw, tdec, fox):
    c = pl.program_id(1)
    slot = _stream_pages(cache_ref, pt_ref, buf_ref, sem_ref, [(krow, kw), (vrow, kw)],
                         n_pages=n_pages, nch=nch, ppc=ppc)
    tk = ppc * PAGE
    past = n_pages * PAGE
    q = q_ref[0]

    def scores(kt, bias_rows):
        s = _dot_tn(kt.astype(BF16), q)
        if fox:
            s = s + _dot_tn(jnp.concatenate(_split3(bias_rows), axis=0), aux_ref[...]) + bnew_ref[0, 8:9, :]
        return s

    @pl.when(c < nch)
    def _():
        if fox:
            s = scores(buf_ref[slot], bias_ref[0])
        else:
            bpc = tk // SEL_BLOCK
            m = bias_ref[0, pl.ds(pl.multiple_of(c * bpc, 8), bpc), :]
            m = jnp.broadcast_to(m[:, None, :], (bpc, SEL_BLOCK, LANES)).reshape(tk, LANES)
            s = jnp.where(m > -0.5, scores(buf_ref[slot], None), NEG_INF)
        s_ref[pl.ds(pl.multiple_of(c * tk, tk), tk), :] = s

    @pl.when(c == nch - 1)
    def _():
        kt = new_ref[0, 0:kw, :]
        if fox:
            sn = scores(kt, bnew_ref[0, 0:8, :])
            tq = _lane(sn.shape) // 8
        else:
            sn = jnp.where(bnew_ref[0, 0:1, :] > -0.5, scores(kt, None), NEG_INF)
            tq = _lane(sn.shape) % 8
        sn = jnp.where(_row(sn.shape) <= tq, sn, NEG_INF)
        s_ref[pl.ds(past, PAGE), :] = sn
        s = s_ref[...]
        ok = s > 0.5 * NEG_INF
        mx = jnp.max(s, axis=0, keepdims=True)
        e = jnp.where(ok, jnp.exp(s - mx), 0.0)
        s_ref[...] = e / jnp.maximum(jnp.sum(e, axis=0, keepdims=True), 1e-30)
        acc_ref[...] = jnp.zeros(acc_ref.shape, F32)

    @pl.when(c >= nch)
    def _():
        p = s_ref[pl.ds(pl.multiple_of((c - nch) * tk, tk), tk), :].astype(BF16)
        acc_ref[...] += _dot(buf_ref[slot].astype(BF16), p)

    @pl.when(c == 2 * nch - 1)
    def _():
        p = s_ref[pl.ds(past, PAGE), :].astype(BF16)
        o_ref[0] = acc_ref[...] + _dot(new_ref[0, kw:2 * kw, :].astype(BF16), p)


def _attn_decode(pt_flat, cache, q, bias, bnew, aux, new_rows, *, batch, n_pages, krow, vrow, kw, tdec, fox, name):
    ppc = min(16, n_pages)
    nch = n_pages // ppc
    tk = ppc * PAGE
    if fox:
        bias_spec = pl.BlockSpec((1, 8, tk), lambda b, c, pt: (b, 0, jnp.minimum(c, nch - 1)))
    else:
        bias_spec = pl.BlockSpec((1,) + bias.shape[1:], lambda b, c, pt: (b, 0, 0))
    grid_spec = pltpu.PrefetchScalarGridSpec(
        num_scalar_prefetch=1, grid=(batch, 2 * nch),
        in_specs=[pl.BlockSpec(memory_space=pl.ANY),
                  pl.BlockSpec((1, kw, LANES), lambda b, c, pt: (b, 0, 0)),
                  bias_spec,
                  pl.BlockSpec((1,) + bnew.shape[1:], lambda b, c, pt: (b, 0, 0)),
                  pl.BlockSpec(aux.shape, lambda b, c, pt: (0, 0)),
                  pl.BlockSpec((1, 2 * kw, PAGE), lambda b, c, pt: (b, 0, 0))],
        out_specs=pl.BlockSpec((1, kw, LANES), lambda b, c, pt: (b, 0, 0)),
        scratch_shapes=[pltpu.VMEM((2, kw, tk), F32), pltpu.SemaphoreType.DMA((2,)),
                        pltpu.VMEM((n_pages * PAGE + PAGE, LANES), F32), pltpu.VMEM((kw, LANES), F32)])
    return pl.pallas_call(
        functools.partial(_attn_decode_kernel, n_pages=n_pages, nch=nch, ppc=ppc, krow=krow, vrow=vrow,
                          kw=kw, tdec=tdec, fox=fox),
        grid_spec=grid_spec, out_shape=jax.ShapeDtypeStruct((batch, kw, LANES), F32),
        compiler_params=_cparams(("arbitrary", "arbitrary")), name=name,
    )(pt_flat, cache, q, bias, bnew, aux, new_rows)


def _win_decode_kernel(q_ref, st_ref, new_ref, o_ref):
    st = st_ref[0].astype(BF16)
    nw = new_ref[0].astype(BF16)
    lw = st.shape[1]
    for g in range(2):
        q = q_ref[0, g]
        s = jnp.concatenate([_dot(q, st[0:LANES]), _dot(q, nw[0:LANES])], axis=1)
        i = _lane(s.shape)
        t = _row(s.shape) % 8
        p = _masked_softmax(s, (i > t + lw - WINDOW) & (i <= t + lw), -1).astype(BF16)
        o_ref[0, g] = _dot_nt(p[:, 0:lw], st[LANES:2 * LANES]) + _dot_nt(p[:, lw:], nw[LANES:2 * LANES])


def _win_decode(qsw, state_t, new_t, batch):
    lw = state_t.shape[2]
    return pl.pallas_call(
        _win_decode_kernel, grid=(batch,),
        in_specs=[pl.BlockSpec((1, 2, 32, LANES), lambda b: (b, 0, 0, 0)),
                  pl.BlockSpec((1, 256, lw), lambda b: (b, 0, 0)),
                  pl.BlockSpec((1, 256, PAGE), lambda b: (b, 0, 0))],
        out_specs=pl.BlockSpec((1, 2, 32, LANES), lambda b: (b, 0, 0, 0)),
        out_shape=jax.ShapeDtypeStruct((batch, 2, 32, LANES), F32),
        compiler_params=_cparams(("arbitrary",)), name="nsa_win_decode",
    )(qsw, state_t, new_t)


def _logf_decode_kernel(pt_ref, cache_ref, new_ref, u_ref, s_ref, perm_ref, tri_ref, c_ref, buf_ref, sem_ref,
                        *, n_pages):
    b = pl.program_id(0)
    nb = pl.num_programs(0)

    def copies(bb, slot):
        return [pltpu.make_async_copy(cache_ref.at[pl.ds(pt_ref[bb * n_pages + i] * FOX_HEADS, FOX_HEADS), :],
                                      buf_ref.at[slot, pl.ds(i * FOX_HEADS, FOX_HEADS), :], sem_ref.at[slot])
                for i in range(n_pages)]

    slot = b % 2

    @pl.when(b == 0)
    def _():
        for cp in copies(b, slot):
            cp.start()

    @pl.when(b + 1 < nb)
    def _():
        for cp in copies(b + 1, 1 - slot):
            cp.start()

    for cp in copies(b, slot):
        cp.wait()
    lp = jnp.concatenate([buf_ref[slot], new_ref[0]], axis=0)
    cs = _dot3(lp, u_ref[...])
    tot = _dot3(cs, s_ref[...])
    c_ref[0] = _dot3_left(perm_ref[...], cs) + _dot3_left(tri_ref[...], tot)


def _logf_decode(pt_flat, cache2d, new_rows, *, batch, n_pages):
    nr = (n_pages + 2) * FOX_HEADS
    r = np.arange(PAGE)
    u = (r[:, None] <= r[None, :]).astype(np.float32)
    last = np.zeros((PAGE, PAGE), np.float32)
    last[PAGE - 1, :] = 1.0
    rr = np.arange(nr)
    tri = ((rr[:, None] % FOX_HEADS == rr[None, :] % FOX_HEADS)
           & (rr[None, :] // FOX_HEADS < rr[:, None] // FOX_HEADS)).astype(np.float32)
    perm = np.zeros((nr, nr), np.float32)
    perm[(rr % FOX_HEADS) * (n_pages + 2) + rr // FOX_HEADS, rr] = 1.0
    consts = [jnp.asarray(u, BF16), jnp.asarray(last, BF16), jnp.asarray(perm, BF16), jnp.asarray(perm @ tri, BF16)]
    grid_spec = pltpu.PrefetchScalarGridSpec(
        num_scalar_prefetch=1, grid=(batch,),
        in_specs=[pl.BlockSpec(memory_space=pl.ANY),
                  pl.BlockSpec((1, 2 * FOX_HEADS, PAGE), lambda b, pt: (b, 0, 0))]
                 + [pl.BlockSpec(c.shape, lambda b, pt: (0, 0)) for c in consts],
        out_specs=pl.BlockSpec((1, nr, PAGE), lambda b, pt: (b, 0, 0)),
        scratch_shapes=[pltpu.VMEM((2, n_pages * FOX_HEADS, PAGE), F32), pltpu.SemaphoreType.DMA((2,))])
    return pl.pallas_call(
        functools.partial(_logf_decode_kernel, n_pages=n_pages),
        grid_spec=grid_spec, out_shape=jax.ShapeDtypeStruct((batch, nr, PAGE), F32),
        compiler_params=_cparams(("arbitrary",)), name="fox_logf_decode",
    )(pt_flat, cache2d, new_rows, *consts)


def _prep_layer_weights(w_in_t, pool_w, pool_scale, w_out_a, w_out_b, w_out_c, w_o):
    d = w_in_t.shape[1]
    tok = jnp.concatenate([w_in_t[0:1024], w_in_t[1816:2584], w_in_t[3100:6428], w_in_t[1792:1816],
                           jnp.zeros((LANES - 24, d), F32)], axis=0)
    feat = jnp.concatenate([w_in_t[1024:1792], w_in_t[2584:3100], jnp.zeros((8 - FOX_HEADS, d), F32)],
                           axis=0).astype(BF16)
    gd = pool_w.shape[-1]
    bd = jnp.zeros((POOL_WIDTH, POOL_WIDTH), F32)
    for g in range(pool_w.shape[0]):
        bd = bd.at[g * gd:(g + 1) * gd, g * gd:(g + 1) * gd].set(pool_w[g])
    return dict(w_tok=_wprep(tok), w_feat=feat, pool_bd=bd.astype(BF16), pool_scale=pool_scale.reshape(1, -1),
                w_out_a=w_out_a.astype(BF16), w_out_b=w_out_b.astype(BF16), w_out_c=w_out_c.astype(BF16),
                w_o=w_o.astype(BF16))


def _to_token_last(a_t, lead):
    b, _, t = a_t.shape
    nd = len(lead)
    return a_t.reshape((b,) + lead + (t,)).transpose((0, nd + 1) + tuple(range(1, nd + 1)))


def _layer_prompt(x2d, lw, norm_w, fox_bf, final_norm, batch, seq):
    pos = jnp.arange(seq)
    (uz, zb, zc, gm, sm, q8, nsat, wint, foxt, logft, fq, kcm, ksel, vsel, kwin, vwin, fk, fv) = _project(
        x2d, norm_w, lw["w_tok"], lw["w_feat"], fox_bf, pos, prompt=True, batch=batch, seq=seq)
    d = _pool_prompt(uz, seq)
    bpt = TKV // CMP_BLOCK
    nb = seq // CMP_BLOCK
    means = kcm[..., :bpt].transpose(0, 1, 3, 2).reshape(batch, nb, 256)
    means = jnp.concatenate([means[:, 0::2], means[:, 1::2]], axis=1)
    dup = lambda m: jnp.stack([jnp.concatenate([m[..., g * HEAD_DIM:(g + 1) * HEAD_DIM]] * 2, axis=-1)
                               for g in range(2)], axis=1).astype(BF16)
    ocmp, qaug = _cmp_prompt(dup(means[..., 0:LANES]), dup(means[..., LANES:2 * LANES]), q8, batch, seq)
    osel = _flash(qaug, ksel, vsel, n_sub=1, n_stack=NSA_REP, tpt=LANES, tk=512, name="nsa_sel_prompt")
    owin = _win_prompt(q8, kwin, vwin, batch, seq)
    tpt = 512
    oc = _flash(fq.reshape(batch, FOX_HEADS, seq // tpt, tpt, LANES), fk, fv,
                n_sub=2, n_stack=1, tpt=tpt, tk=512, name="fox_prompt")
    n = batch * seq
    xo, y = _output(x2d, d, uz, ocmp.reshape(n, 512), osel.reshape(n, 512), owin.reshape(n, 512), zb,
                    oc.reshape(n, 256), zc, gm, sm, lw, final_norm)
    wl = min(WINDOW, seq)
    states = (_to_token_last(nsat, (4, NSA_KV_HEADS, HEAD_DIM)),
              _to_token_last(foxt, (2, FOX_HEADS, HEAD_DIM)),
              logft[:, :FOX_HEADS].transpose(0, 2, 1),
              _to_token_last(wint[:, :, seq - wl:], (2, NSA_KV_HEADS, HEAD_DIM)),
              uz[:, :POOL_WIDTH].reshape(batch, seq, POOL_WIDTH)[:, -POOL_STATE:])
    return xo, y, states


def _new_page(a_t, batch, tdec):
    f = a_t.shape[0]
    a = a_t.reshape(f, batch, tdec).transpose(1, 0, 2)
    return jnp.concatenate([a, jnp.zeros((batch, f, PAGE - tdec), a.dtype)], axis=2)


def _layer_decode(x2d, lw, norm_w, fox_bf, final_norm, pt_flat, nsa_cache, fox_cache, logf_cache,
                  state_win_t, state_pool_tb, batch, tdec, n_pages):
    past = n_pages * PAGE
    n = batch * tdec
    pos = past + jnp.tile(jnp.arange(tdec), batch)
    (uz, zb, zc, gm, sm, q8, nsat, wint, foxt, logft, qc) = _project(
        x2d, norm_w, lw["w_tok"], lw["w_feat"], fox_bf, pos, prompt=False, batch=batch, seq=tdec)
    nsat, wint, foxt, logft = nsat[0], wint[0], foxt[0], logft[0]

    u_tb = uz[:, :POOL_WIDTH].reshape(batch, tdec, POOL_WIDTH).transpose(1, 0, 2)
    pool_seq = jnp.concatenate([state_pool_tb, u_tb], axis=0)
    d = _pool_decode(pool_seq, past, tdec).transpose(1, 0, 2).reshape(n, POOL_WIDTH)

    q5 = q8[0].reshape(NSA_KV_HEADS, NSA_REP, batch, tdec, LANES)
    qs = q5.transpose(2, 0, 1, 3, 4)
    qs = jnp.concatenate([qs, jnp.zeros((batch, 2, NSA_REP, 8 - tdec, LANES), BF16)], axis=3)
    qs = qs.reshape(batch, 2, NSA_REP * 8, LANES)
    qsw = jnp.stack([qs[:, 0], jnp.roll(qs[:, 1], HALF, axis=-1)], axis=1)
    ocmp_k, selm = _cmp_decode(pt_flat, nsa_cache, qsw, batch=batch, n_pages=n_pages, past=past)

    def unstack_groups(o):
        o = o.reshape(batch, 2, NSA_REP, 8, 2, HEAD_DIM)[:, :, :, :tdec]
        o = jnp.stack([o[:, 0, :, :, 0], o[:, 1, :, :, 1]], axis=1)
        return o.transpose(0, 3, 1, 2, 4).reshape(n, NSA_HEADS * HEAD_DIM)

    ocmp = unstack_groups(ocmp_k)
    owin = unstack_groups(_win_decode(qsw, state_win_t, _new_page(wint, batch, tdec), batch))

    nsd = past // SEL_BLOCK
    q6 = q5[..., :HEAD_DIM].transpose(2, 1, 0, 3, 4)
    q6 = jnp.concatenate([q6, jnp.zeros((batch, NSA_REP, 2, 8 - tdec, HEAD_DIM), BF16)], axis=3)
    eye2 = jnp.eye(2, dtype=BF16)
    qbd = q6[:, :, :, :, None, :] * eye2[None, None, :, None, :, None]
    qbd_t = qbd.reshape(batch, 64, LANES).transpose(0, 2, 1)
    qbd_t = jnp.concatenate([qbd_t, jnp.zeros((batch, LANES, LANES - 64), BF16)], axis=2)
    sel_t = jnp.tile(selm.reshape(batch, 16, nsd + LANES).transpose(0, 2, 1), (1, 1, NSA_REP))
    sel_t = jnp.concatenate([sel_t, jnp.zeros((batch, nsd + LANES, LANES - 64), F32)], axis=2)
    osel_k = _attn_decode(pt_flat, nsa_cache, qbd_t, sel_t[:, :nsd], sel_t[:, nsd:nsd + 8],
                          jnp.zeros((8, LANES), BF16), _new_page(nsat[256:512], batch, tdec),
                          batch=batch, n_pages=n_pages, krow=256, vrow=384, kw=LANES, tdec=tdec, fox=False,
                          name="nsa_sel_decode")
    os_ = osel_k[:, :, :64].reshape(batch, 2, HEAD_DIM, NSA_REP, 2, 8)[..., :tdec]
    osel = jnp.stack([os_[:, 0, :, :, 0], os_[:, 1, :, :, 1]], axis=1)
    osel = osel.transpose(0, 4, 1, 3, 2).reshape(n, 512)

    lnew = _new_page(logft, batch, tdec)
    lnew = jnp.concatenate([lnew[:, :FOX_HEADS], jnp.zeros((batch, FOX_HEADS, PAGE), F32)], axis=1)
    csum = _logf_decode(pt_flat, logf_cache, lnew, batch=batch, n_pages=n_pages)
    c4 = csum.reshape(batch, FOX_HEADS, (n_pages + 2) * PAGE)
    ck8 = jnp.concatenate([-c4, jnp.zeros_like(c4)], axis=1)
    cq = jnp.concatenate([c4[:, :, past:past + tdec].transpose(0, 2, 1),
                          jnp.zeros((batch, tdec, 8 - FOX_HEADS), F32)], axis=2).reshape(batch, 1, tdec * 8)
    cq = jnp.concatenate([cq, jnp.zeros((batch, 1, LANES - tdec * 8), F32)], axis=2)
    bnew = jnp.concatenate([ck8[:, :, past:past + PAGE], cq, jnp.zeros((batch, 7, LANES), F32)], axis=1)
    qf = (qc * SCALE).reshape(batch, tdec, FOX_HEADS, HEAD_DIM)
    eye = jnp.eye(8, FOX_HEADS, dtype=F32)
    qfbd = (qf[:, :, None, :, :] * eye[None, None, :, :, None]).reshape(batch, tdec * 8, FOX_HEADS * HEAD_DIM)
    qf_t = jnp.concatenate([qfbd.transpose(0, 2, 1), jnp.zeros((batch, 256, LANES - tdec * 8), F32)],
                           axis=2).astype(BF16)
    col = np.arange(LANES)
    head_cols = np.zeros((32, LANES), np.float32)
    for p in range(3):
        head_cols[p * 8 + col[:tdec * 8] % 8, col[:tdec * 8]] = 1.0
    oc_k = _attn_decode(pt_flat, fox_cache, qf_t, ck8, bnew, jnp.asarray(head_cols[:24], BF16),
                        _new_page(foxt, batch, tdec),
                        batch=batch, n_pages=n_pages, krow=0, vrow=256, kw=256, tdec=tdec, fox=True,
                        name="fox_decode")
    o6 = oc_k[:, :, :tdec * 8].reshape(batch, FOX_HEADS, HEAD_DIM, tdec, 8)
    oc = jnp.stack([o6[:, h, :, :, h] for h in range(FOX_HEADS)], axis=1)
    oc = oc.transpose(0, 3, 1, 2).reshape(n, 256)

    xo, y = _output(x2d, d, uz, ocmp, osel, owin, zb, oc, zc, gm, sm, lw, final_norm)
    win_seq_t = jnp.concatenate([state_win_t, wint.reshape(256, batch, tdec).transpose(1, 0, 2)], axis=2)
    states = (nsat.T.reshape(batch, tdec, 4, NSA_KV_HEADS, HEAD_DIM),
              foxt.T.reshape(batch, tdec, 2, FOX_HEADS, HEAD_DIM),
              logft[:FOX_HEADS].T.reshape(batch, tdec, FOX_HEADS),
              _to_token_last(win_seq_t[:, :, -WINDOW:], (2, NSA_KV_HEADS, HEAD_DIM)),
              pool_seq[-POOL_STATE:].transpose(1, 0, 2))
    return xo, y, states


def kernel(x_prompt, x_sample, cache_nsa_kv, cache_fox_kv, cache_fox_logf, state_win_kv, state_pool, page_table,
           norm_w, w_in, pool_w, pool_scale, fox_bf, w_out_a, w_out_b, w_out_c, w_o, final_norm):
    depth = w_in.shape[0]
    batch, seq, dm = x_prompt.shape
    dbatch, tdec, _ = x_sample.shape
    n_phys, page = cache_nsa_kv.shape[1], cache_nsa_kv.shape[2]
    n_pages = page_table.shape[1]
    assert page == PAGE and tdec <= 4 and n_pages % min(32, n_pages) == 0
    assert state_win_kv.shape[2] == WINDOW and n_pages * PAGE >= WINDOW

    nsa_cache = cache_nsa_kv.transpose(0, 1, 3, 4, 5, 2).reshape(depth * n_phys, 512, PAGE)
    fox_cache = cache_fox_kv.transpose(0, 1, 3, 4, 5, 2).reshape(depth * n_phys, 512, PAGE)
    logf_cache = cache_fox_logf.transpose(0, 1, 3, 2).reshape(depth * n_phys * FOX_HEADS, PAGE)
    win_state_t = state_win_kv.transpose(0, 1, 3, 4, 5, 2).reshape(depth, dbatch, 256, WINDOW)
    pool_state_tb = state_pool.transpose(0, 2, 1, 3)
    w_in_t = w_in.transpose(0, 2, 1)

    hp = x_prompt.reshape(batch * seq, dm)
    hs = x_sample.reshape(dbatch * tdec, dm)
    st_p, st_s = [], []
    for layer in range(depth):
        lw = _prep_layer_weights(w_in_t[layer], pool_w[layer], pool_scale[layer], w_out_a[layer],
                                 w_out_b[layer], w_out_c[layer], w_o[layer])
        hp, yp, sp = _layer_prompt(hp, lw, norm_w[layer], fox_bf[layer], final_norm, batch, seq)
        pt_flat = (page_table + layer * n_phys).reshape(-1).astype(jnp.int32)
        hs, ys, ss = _layer_decode(hs, lw, norm_w[layer], fox_bf[layer], final_norm, pt_flat, nsa_cache, fox_cache,
                                   logf_cache, win_state_t[layer], pool_state_tb[layer], dbatch, tdec, n_pages)
        st_p.append(sp)
        st_s.append(ss)
    outs = [yp.reshape(batch, seq, dm), ys.reshape(dbatch, tdec, dm)]
    for k in range(5):
        outs.append(jnp.stack([s[k] for s in st_p], 0))
        outs.append(jnp.stack([s[k] for s in st_s], 0))
    return tuple(outs)
```

```python
import functools

import numpy as np
import jax
import jax.numpy as jnp
from jax import lax
from jax.experimental import pallas as pl
from jax.experimental.pallas import tpu as pltpu

F32 = jnp.float32
BF16 = jnp.bfloat16

HEAD_DIM = 64
POOL_WIDTH = 256
POOL_WINDOWS = (2, 4, 8, 16)
POOL_STATE = 15
NSA_HEADS = 8
NSA_KV_HEADS = 2
NSA_REP = NSA_HEADS // NSA_KV_HEADS
CMP_BLOCK = 32
SEL_BLOCK = 64
SEL_TOPK = 16
WINDOW = 512
FORCE_SCORE = 1e4
FOX_HEADS = 4
ROPE_THETA = 10000.0
RMS_EPS = 1e-6
NEG_INF = -1e30
SCALE = HEAD_DIM ** -0.5

LANES = 128
HALF = LANES // 2
PAGE = 128
VMEM_LIMIT = 56 * 1024 * 1024

C_U, C_QB, C_QC, C_END = 0, 256, 768, 1024
G_ZA, G_ZB, G_ZC, G_GM, G_SM, G_END = 0, 256, 768, 1024, 4096, 4224
R_KV, R_KC, R_VC, R_FC, R_END = 0, 768, 1024, 1280, 1288
SM_GB = 0
TKV = 256
FT = 512


def _cparams(sem):
    return pltpu.CompilerParams(dimension_semantics=sem, vmem_limit_bytes=VMEM_LIMIT)


def _lane(shape):
    return lax.broadcasted_iota(jnp.int32, shape, len(shape) - 1)


def _row(shape):
    return lax.broadcasted_iota(jnp.int32, shape, len(shape) - 2)


def _split3(x):
    hi = x.astype(BF16)
    r1 = x - hi.astype(F32)
    mid = r1.astype(BF16)
    lo = (r1 - mid.astype(F32)).astype(BF16)
    return hi, mid, lo


def _dot(a, b):
    return jnp.dot(a, b, preferred_element_type=F32)


def _dot_nt(a, b):
    return lax.dot_general(a, b, (((1,), (1,)), ((), ())), preferred_element_type=F32)


def _dot_tn(a, b):
    return lax.dot_general(a, b, (((0,), (0,)), ((), ())), preferred_element_type=F32)


def _dot3(x, m):
    n = x.shape[0]
    r = _dot(jnp.concatenate(_split3(x), axis=0), m)
    return r[0:n] + r[n:2 * n] + r[2 * n:3 * n]


def _dot2(x, m):
    n = x.shape[0]
    hi = x.astype(BF16)
    lo = (x - hi.astype(F32)).astype(BF16)
    r = _dot(jnp.concatenate([hi, lo], axis=0), m)
    return r[0:n] + r[n:2 * n]


def _dot3_left(m, x):
    n = x.shape[1]
    r = _dot(m, jnp.concatenate(_split3(x), axis=1))
    return r[:, 0:n] + r[:, n:2 * n] + r[:, 2 * n:3 * n]


def _swap_halves(x):
    return pltpu.roll(x, HALF, x.ndim - 1)


def _to_low_half(x, g, fill):
    lo = _lane(x.shape) < HALF
    return jnp.where(lo, x if g == 0 else _swap_halves(x), fill)


def _pair_heads(pieces):
    out = []
    for j in range(0, len(pieces), 2):
        lo = _lane(pieces[j].shape) < HALF
        out.append(jnp.where(lo, pieces[j], pieces[j + 1]))
    return out


def _masked_softmax(s, ok, axis):
    s = jnp.where(ok, s, NEG_INF)
    m = jnp.max(s, axis=axis, keepdims=True)
    e = jnp.where(ok, jnp.exp(s - m), 0.0)
    return e / jnp.maximum(jnp.sum(e, axis=axis, keepdims=True), 1e-30)


def _wprep_kernel(wt_ref, w_ref):
    w_ref[...] = wt_ref[...].T.astype(BF16)


def _wprep(wt):
    cols, d = wt.shape
    return pl.pallas_call(
        _wprep_kernel, grid=(cols // LANES,),
        in_specs=[pl.BlockSpec((LANES, d), lambda i: (i, 0))],
        out_specs=pl.BlockSpec((d, LANES), lambda i: (0, i)),
        out_shape=jax.ShapeDtypeStruct((d, cols), BF16),
        compiler_params=_cparams(("arbitrary",)), name="w_in_transpose",
    )(wt)


def _rope_tile(x, cosf, sinf):
    lane = _lane(x.shape)
    first = (lane % HEAD_DIM) < (HEAD_DIM // 2)
    rot = jnp.where(first, pltpu.roll(x, LANES - HEAD_DIM // 2, 1), pltpu.roll(x, HEAD_DIM // 2, 1))
    return x * cosf + rot * sinf


def _rope_rows(x, cos_t, sin_t):
    half = HEAD_DIM // 2
    out = []
    for h in range(x.shape[0] // HEAD_DIM):
        x1 = x[h * HEAD_DIM:h * HEAD_DIM + half]
        x2 = x[h * HEAD_DIM + half:(h + 1) * HEAD_DIM]
        out += [x1 * cos_t - x2 * sin_t, x2 * cos_t + x1 * sin_t]
    return jnp.concatenate(out, axis=0)


def _proj_kernel(*refs, prompt, tiles_per_seq):
    if prompt:
        (x_ref, nw_ref, wt_ref, wk_ref, cos_ref, sin_ref, cost_ref, sint_ref, bfc_ref,
         ohb_ref, amean_ref, triu_ref, sq_ref,
         u_ref, q8_ref, nsat_ref, wint_ref, foxt_ref, logft_ref,
         fq_ref, kcm_ref, ksel_ref, vsel_ref, kwin_ref, vwin_ref, fk_ref, fv_ref, carry_ref) = refs
    else:
        (x_ref, nw_ref, wt_ref, wk_ref, cos_ref, sin_ref, cost_ref, sint_ref, bfc_ref,
         u_ref, q8_ref, nsat_ref, wint_ref, foxt_ref, logft_ref, qc_ref) = refs

    x = x_ref[...]
    ms = jnp.mean(x * x, axis=-1, keepdims=True)
    hb = ((x * lax.rsqrt(ms + RMS_EPS)) * nw_ref[...]).astype(BF16)
    tm = x.shape[0]

    def seg(c0, n):
        return _dot(hb, wt_ref[:, c0:c0 + n])

    def seg_t(r0, n):
        return _dot_nt(wk_ref[r0:r0 + n, :], hb)

    u_ref[...] = seg(C_U, POOL_WIDTH)
    cosf = cos_ref[...]
    sinf = sin_ref[...]
    for c in range(4):
        y = _rope_tile(seg(C_QB + c * LANES, LANES), cosf, sinf) * SCALE
        q8_ref[0, 2 * c] = _to_low_half(y, 0, 0.0).astype(BF16)
        q8_ref[0, 2 * c + 1] = _to_low_half(y, 1, 0.0).astype(BF16)
    qc = seg(C_QC, 256)

    cos_t = cost_ref[...]
    sin_t = sint_ref[...]
    kcmp = _rope_rows(seg_t(R_KV, LANES), cos_t, sin_t)
    vcmp = seg_t(R_KV + 128, LANES)
    ksel = _rope_rows(seg_t(R_KV + 256, LANES), cos_t, sin_t)
    vsel = seg_t(R_KV + 384, LANES)
    kwin = _rope_rows(seg_t(R_KV + 512, LANES), cos_t, sin_t)
    vwin = seg_t(R_KV + 640, LANES)
    nsat_ref[0, 0:128] = kcmp
    nsat_ref[0, 128:256] = vcmp
    nsat_ref[0, 256:384] = ksel
    nsat_ref[0, 384:512] = vsel
    wint_ref[0, 0:128] = kwin
    wint_ref[0, 128:256] = vwin
    kc = seg_t(R_KC, 256)
    vc = seg_t(R_VC, 256)
    foxt_ref[0, 0:256] = kc
    foxt_ref[0, 256:512] = vc
    f_t = seg_t(R_FC, 8)
    logf = jnp.where(_row(f_t.shape) < FOX_HEADS, jax.nn.log_sigmoid(f_t + bfc_ref[...]), 0.0)
    logft_ref[0] = logf

    if not prompt:
        qc_ref[...] = qc
        return

    kcm_ref[0, 0] = _dot3(jnp.concatenate([kcmp, vcmp], axis=0), amean_ref[...])
    ohb = ohb_ref[...]
    for g in range(2):
        gs = slice(g * HEAD_DIM, (g + 1) * HEAD_DIM)
        ksel_ref[0, g, 0] = jnp.concatenate([ksel[gs].astype(BF16), ohb], axis=0)
        vsel_ref[0, g, 0] = jnp.concatenate([vsel[gs]] * 2, axis=0).astype(BF16)
        kwin_ref[0, g, 0] = jnp.concatenate([kwin[gs]] * 2, axis=0).astype(BF16)
        vwin_ref[0, g, 0] = jnp.concatenate([vwin[gs]] * 2, axis=0).astype(BF16)

    i = pl.program_id(0)
    carry = jnp.where(i % tiles_per_seq == 0, 0.0, carry_ref[:, 0:1])
    csum = _dot3(logf, triu_ref[...]) + carry
    carry_ref[...] = jnp.broadcast_to(csum[:, tm - 1:tm], carry_ref.shape)
    c_hi, c_mid, c_lo = [p.astype(F32) for p in _split3(csum)]
    r8 = _row((8, tm))
    c_tok = jnp.concatenate([csum, jnp.zeros((LANES - 8, tm), F32)], axis=0).T
    cq = _dot(jnp.concatenate(_split3(c_tok), axis=1), sq_ref[...])
    lane = _lane((tm, LANES))
    one_q = ((lane >= HALF + 3) & (lane < HALF + 6)).astype(F32)
    for h in range(FOX_HEADS):
        t, g = h // 2, h % 2
        hs = slice(h * HEAD_DIM, (h + 1) * HEAD_DIM)
        aug = jnp.where(r8 < 3, 1.0, jnp.where(r8 == 3, -c_hi[h:h + 1], jnp.where(
            r8 == 4, -c_mid[h:h + 1], jnp.where(r8 == 5, -c_lo[h:h + 1], 0.0))))
        fk_ref[0, h, 0] = jnp.concatenate([kc[hs], aug, jnp.zeros((HALF - 8, tm), F32)], axis=0).astype(BF16)
        fv_ref[0, h, 0] = jnp.concatenate([vc[hs]] * 2, axis=0).astype(BF16)
        fq_ref[0, h] = _to_low_half(qc[:, t * LANES:(t + 1) * LANES] * SCALE, g,
                                    cq[:, h * LANES:(h + 1) * LANES] + one_q).astype(BF16)


def _rope_tables(pos):
    half = HEAD_DIM // 2
    inv = ROPE_THETA ** (-jnp.arange(half, dtype=F32) / half)
    ang = pos.astype(F32)[:, None] * inv[None, :]
    cos, sin = jnp.cos(ang), jnp.sin(ang)
    return jnp.tile(cos, (1, 4)), jnp.tile(jnp.concatenate([-sin, sin], 1), (1, 2)), cos.T, sin.T


def _fox_query_aug_matrix():
    sq = np.zeros((3 * LANES, FOX_HEADS * LANES), np.float32)
    for h in range(FOX_HEADS):
        for p in range(3):
            sq[p * LANES + h, h * LANES + HALF + p] = 1.0
    return jnp.asarray(sq, BF16)


def _project(x2d, norm_w, w_tok, w_feat, fox_bf, pos, *, prompt, batch, seq):
    n, d = x2d.shape
    tm = TKV if prompt else n
    tiles_per_seq = seq // tm if prompt else 1
    cosf, sinf, cos_t, sin_t = _rope_tables(pos)
    bf_col = jnp.zeros((8, 1), F32).at[0:FOX_HEADS, 0].set(fox_bf)
    row = lambda w: pl.BlockSpec((tm, w), lambda i: (i, 0))
    full = lambda a: pl.BlockSpec(a.shape, lambda i: (0,) * a.ndim)
    if prompt:
        tab = pl.BlockSpec((tm, LANES), lambda i: (i % tiles_per_seq, 0))
        tab_t = pl.BlockSpec((HEAD_DIM // 2, tm), lambda i: (0, i % tiles_per_seq))
    else:
        tab = row(LANES)
        tab_t = pl.BlockSpec((HEAD_DIM // 2, tm), lambda i: (0, i))
    in_arrays = [x2d, norm_w.reshape(1, d), w_tok, w_feat, cosf, sinf, cos_t, sin_t, bf_col]
    in_specs = [row(d), full(in_arrays[1]), full(w_tok), full(w_feat), tab, tab, tab_t, tab_t, full(bf_col)]
    nb = batch if prompt else 1
    bidx = lambda i: (i // tiles_per_seq, 0, i % tiles_per_seq)
    feat = lambda r: pl.BlockSpec((1, r, tm), bidx)
    hb = lambda nh: pl.BlockSpec((1, nh, tm, LANES), lambda i: (i // tiles_per_seq, 0, i % tiles_per_seq, 0))
    nt = (seq if prompt else n)
    out_shapes = [
        jax.ShapeDtypeStruct((n, POOL_WIDTH), F32), jax.ShapeDtypeStruct((nb, NSA_HEADS, nt, LANES), BF16),
        jax.ShapeDtypeStruct((nb, 512, nt), F32), jax.ShapeDtypeStruct((nb, 256, nt), F32),
        jax.ShapeDtypeStruct((nb, 512, nt), F32), jax.ShapeDtypeStruct((nb, 8, nt), F32),
    ]
    out_specs = [row(POOL_WIDTH), hb(NSA_HEADS), feat(512), feat(256), feat(512), feat(8)]
    scratch = []
    if prompt:
        t = np.arange(seq)
        ohb = np.zeros((HALF, seq), np.float32)
        ohb[t // SEL_BLOCK, t] = -NEG_INF
        tt = np.arange(tm)
        amean = np.zeros((tm, LANES), np.float32)
        amean[tt, tt // CMP_BLOCK] = 1.0 / CMP_BLOCK
        triu = np.triu(np.ones((tm, tm), np.float32))
        extra = [jnp.asarray(ohb, BF16), jnp.asarray(amean, BF16), jnp.asarray(triu, BF16), _fox_query_aug_matrix()]
        in_arrays += extra
        in_specs += [pl.BlockSpec((HALF, tm), lambda i: (0, i % tiles_per_seq)),
                     full(extra[1]), full(extra[2]), full(extra[3])]
        nkt = seq // tm
        kvt = lambda ns: pl.BlockSpec((1, ns, 1, LANES, tm),
                                      lambda i: (i // tiles_per_seq, 0, i % tiles_per_seq, 0, 0))
        out_shapes.append(jax.ShapeDtypeStruct((batch, FOX_HEADS, seq, LANES), BF16))
        out_specs.append(hb(FOX_HEADS))
        out_shapes.append(jax.ShapeDtypeStruct((batch, nkt, 256, LANES), F32))
        out_specs.append(pl.BlockSpec((1, 1, 256, LANES), lambda i: (i // tiles_per_seq, i % tiles_per_seq, 0, 0)))
        tpf = FT // tm
        kvf = lambda ns: pl.BlockSpec(
            (1, ns, 1, LANES, tm),
            lambda i: (i // tiles_per_seq, 0, (i % tiles_per_seq) // tpf, 0, (i % tiles_per_seq) % tpf))
        for ns, flash in ((2, True), (2, True), (2, False), (2, False), (FOX_HEADS, True), (FOX_HEADS, True)):
            if flash:
                out_shapes.append(jax.ShapeDtypeStruct((batch, ns, seq // FT, LANES, FT), BF16))
                out_specs.append(kvf(ns))
            else:
                out_shapes.append(jax.ShapeDtypeStruct((batch, ns, nkt, LANES, tm), BF16))
                out_specs.append(kvt(ns))
        scratch = [pltpu.VMEM((8, LANES), F32)]
    else:
        out_shapes.append(jax.ShapeDtypeStruct((n, 256), F32))
        out_specs.append(row(256))
    return pl.pallas_call(
        functools.partial(_proj_kernel, prompt=prompt, tiles_per_seq=tiles_per_seq),
        grid=(n // tm,), in_specs=in_specs, out_specs=out_specs, out_shape=out_shapes,
        scratch_shapes=scratch, compiler_params=_cparams(("arbitrary",)),
        name="proj_prompt" if prompt else "proj_decode",
    )(*in_arrays)


def _pool_mix(u, prev, pos):
    grp = _lane(u.shape) // (POOL_WIDTH // len(POOL_WINDOWS))
    acc = u
    win = None
    for j in range(1, POOL_WINDOWS[-1]):
        acc = acc + prev(j)
        if j + 1 in POOL_WINDOWS:
            k = POOL_WINDOWS.index(j + 1)
            win = acc if win is None else jnp.where(grp >= k, acc, win)
    w = jnp.zeros(u.shape, jnp.int32)
    for k, wk in enumerate(POOL_WINDOWS):
        w = jnp.where(grp == k, wk, w)
    cnt = jnp.minimum(pos + 1, w).astype(F32)
    return win / cnt - u


def _pool_prompt_kernel(u_ref, halo_ref, d_ref, s_ref, *, tiles_per_seq, tm):
    i = pl.program_id(0) % tiles_per_seq
    s_ref[0:16, :] = jnp.where(i == 0, 0.0, halo_ref[...])
    s_ref[16:, :] = u_ref[...]
    pos = i * tm + _row((tm, POOL_WIDTH))
    d_ref[...] = _pool_mix(u_ref[...], lambda j: s_ref[16 - j:16 - j + tm, :], pos)


def _pool_prompt(uz, seq):
    n = uz.shape[0]
    tm = 512
    tiles_per_seq = seq // tm
    return pl.pallas_call(
        functools.partial(_pool_prompt_kernel, tiles_per_seq=tiles_per_seq, tm=tm),
        grid=(n // tm,),
        in_specs=[pl.BlockSpec((tm, POOL_WIDTH), lambda i: (i, 0)),
                  pl.BlockSpec((16, POOL_WIDTH), lambda i: (jnp.maximum(i * (tm // 16) - 1, 0), 0))],
        out_specs=pl.BlockSpec((tm, POOL_WIDTH), lambda i: (i, 0)),
        out_shape=jax.ShapeDtypeStruct((n, POOL_WIDTH), F32),
        scratch_shapes=[pltpu.VMEM((tm + 16, POOL_WIDTH), F32)],
        compiler_params=_cparams(("arbitrary",)), name="pool_prompt",
    )(uz, uz)


def _pool_decode_kernel(seq_ref, d_ref, *, pos0, tdec):
    for t in range(tdec):
        pos = jnp.full(seq_ref.shape[1:], pos0 + t, jnp.int32)
        d_ref[t] = _pool_mix(seq_ref[POOL_STATE + t], lambda j: seq_ref[POOL_STATE + t - j], pos)


def _pool_decode(seq_tb, pos0, tdec):
    _, b, w = seq_tb.shape
    return pl.pallas_call(
        functools.partial(_pool_decode_kernel, pos0=pos0, tdec=tdec),
        grid=(1,),
        in_specs=[pl.BlockSpec(seq_tb.shape, lambda i: (0, 0, 0))],
        out_specs=pl.BlockSpec((tdec, b, w), lambda i: (0, 0, 0)),
        out_shape=jax.ShapeDtypeStruct((tdec, b, w), F32),
        compiler_params=_cparams(("arbitrary",)), name="pool_decode",
    )(seq_tb)


def _select_rank(imp, blk, cur, k_top):
    forced = (blk == 0) | (blk == cur) | (blk == cur - 1)
    val = jnp.where(blk <= cur, imp + jnp.where(forced, FORCE_SCORE, 0.0), -1.0)
    cnt = jnp.zeros(val.shape, jnp.int32)
    for i in range(val.shape[0]):
        vi = val[i:i + 1, :]
        cnt = cnt + ((vi > val) | ((vi == val) & (blk > i))).astype(jnp.int32)
    return (cnt < k_top) & (blk <= cur)


def _cmp_prompt_kernel(kc_ref, vc_ref, q_ref, o_ref, qaug_ref, *, tq, seq):
    qi = pl.program_id(2)
    ns = seq // SEL_BLOCK
    qs = jnp.concatenate([q_ref[0, r] for r in range(NSA_REP)], axis=0)
    s = _dot_nt(kc_ref[0, 0], qs)
    row = _row(s.shape)
    nblk = jnp.where(row < ns, 2 * row, 2 * (row - ns) + 1)
    t = qi * tq + _lane(s.shape) % tq
    p = _masked_softmax(s, (nblk + 1) * CMP_BLOCK - 1 <= t, 0)
    o = _dot_tn(p.astype(BF16), vc_ref[0, 0])
    pieces = _pair_heads([o[r * tq:(r + 1) * tq] for r in range(NSA_REP)])
    for j, piece in enumerate(pieces):
        o_ref[0, :, j * LANES:(j + 1) * LANES] = piece

    imp = p[:, 0:tq]
    for r in range(1, NSA_REP):
        imp = imp + p[:, r * tq:(r + 1) * tq]
    imp = imp[:ns] + imp[ns:]
    blk = _row(imp.shape)
    cur = (qi * tq + _lane(imp.shape)) // SEL_BLOCK
    sel = _select_rank(imp, blk, cur, min(SEL_TOPK, ns))
    selm = jnp.where(sel, 0.0, -1.0)
    pad = [jnp.zeros((HALF, tq), F32), selm]
    if ns < HALF:
        pad.append(jnp.zeros((HALF - ns, tq), F32))
    sel_t = jnp.concatenate(pad, axis=0).T
    for r in range(NSA_REP):
        qaug_ref[0, 0, 0, r * tq:(r + 1) * tq, :] = (q_ref[0, r].astype(F32) + sel_t).astype(BF16)


def _cmp_prompt(kc_dup, vc_dup, q8, batch, seq):
    tq = LANES
    nq = seq // tq
    nb = seq // CMP_BLOCK
    assert seq % (2 * LANES) == 0 and seq // SEL_BLOCK <= HALF
    kv = pl.BlockSpec((1, 1, nb, LANES), lambda b, g, q: (b, g, 0, 0))
    return pl.pallas_call(
        functools.partial(_cmp_prompt_kernel, tq=tq, seq=seq),
        grid=(batch, NSA_KV_HEADS, nq),
        in_specs=[kv, kv, pl.BlockSpec((1, NSA_REP, tq, LANES), lambda b, g, q: (b, g, q, 0))],
        out_specs=[pl.BlockSpec((1, tq, 256), lambda b, g, q: (b, q, g)),
                   pl.BlockSpec((1, 1, 1, NSA_REP * tq, LANES), lambda b, g, q: (b, g, q, 0, 0))],
        out_shape=[jax.ShapeDtypeStruct((batch, seq, 512), F32),
                   jax.ShapeDtypeStruct((batch, NSA_KV_HEADS, nq, NSA_REP * tq, LANES), BF16)],
        compiler_params=_cparams(("arbitrary", "arbitrary", "arbitrary")), name="nsa_cmp_prompt",
    )(kc_dup, vc_dup, q8)


def _flash_kernel(q_ref, k_ref, v_ref, o_ref, *, n_sub, n_stack, tpt):
    qi = pl.program_id(2)
    q0 = qi * tpt
    nfull = q0 // FT
    rows = n_stack * tpt
    pieces = []
    for u in range(n_sub):
        q = q_ref[0, u, 0]

        def step(j, carry, masked):
            m, l, acc = carry
            s = _dot(q, k_ref[0, u, j])
            if masked:
                key = j * FT + _lane(s.shape)
                tpos = q0 + _row(s.shape) % tpt
                s = jnp.where(key <= tpos, s, NEG_INF)
            mn = jnp.maximum(m, jnp.max(s, axis=-1, keepdims=True))
            a = jnp.exp(m - mn)
            p = jnp.exp(s - mn)
            pv = _dot_nt(p.astype(BF16), v_ref[0, u, j])
            return mn, a * l + jnp.sum(p, axis=-1, keepdims=True), a * acc + pv

        init = (jnp.full((rows, 1), -1e38, F32), jnp.zeros((rows, 1), F32), jnp.zeros((rows, LANES), F32))
        carry = lax.fori_loop(0, nfull, lambda j, c: step(j, c, False), init)
        _, l, acc = step(nfull, carry, True)
        o = acc / l
        pieces += [o[r * tpt:(r + 1) * tpt] for r in range(n_stack)]
    for j, piece in enumerate(_pair_heads(pieces)):
        o_ref[0, :, j * LANES:(j + 1) * LANES] = piece


def _flash(q_aug, k_aug, v_dup, *, n_sub, n_stack, tpt, name):
    b, s, nq, rows, _ = q_aug.shape
    nkt = k_aug.shape[2]
    seq = nkt * FT
    assert FT % tpt == 0 and rows == n_stack * tpt
    wout = n_sub * n_stack * HEAD_DIM
    kv = pl.BlockSpec((1, n_sub, nkt, LANES, FT), lambda bb, ss, q: (bb, ss, 0, 0, 0))
    return pl.pallas_call(
        functools.partial(_flash_kernel, n_sub=n_sub, n_stack=n_stack, tpt=tpt),
        grid=(b, s // n_sub, nq),
        in_specs=[pl.BlockSpec((1, n_sub, 1, rows, LANES), lambda bb, ss, q: (bb, ss, q, 0, 0)), kv, kv],
        out_specs=pl.BlockSpec((1, tpt, wout), lambda bb, ss, q: (bb, q, ss)),
        out_shape=jax.ShapeDtypeStruct((b, seq, s * n_stack * HEAD_DIM), F32),
        compiler_params=_cparams(("arbitrary", "arbitrary", "arbitrary")), name=name,
    )(q_aug, k_aug, v_dup)


def _win_prompt_kernel(q_ref, k_ref, v_ref, o_ref, *, tq, nwt):
    qi = pl.program_id(2)
    q0 = qi * tq
    ts = jnp.maximum(qi - (nwt - 1), 0)
    qs = jnp.concatenate([q_ref[0, r] for r in range(NSA_REP)], axis=0)
    s = jnp.concatenate([_dot(qs, k_ref[0, 0, ts + i]) for i in range(nwt)], axis=1)
    key = ts * TKV + _lane(s.shape)
    t = q0 + _row(s.shape) % tq
    p = _masked_softmax(s, (key <= t) & (key > t - WINDOW), -1).astype(BF16)
    o = _dot_nt(p[:, 0:TKV], v_ref[0, 0, ts])
    for i in range(1, nwt):
        o = o + _dot_nt(p[:, i * TKV:(i + 1) * TKV], v_ref[0, 0, ts + i])
    for j, piece in enumerate(_pair_heads([o[r * tq:(r + 1) * tq] for r in range(NSA_REP)])):
        o_ref[0, :, j * LANES:(j + 1) * LANES] = piece


def _win_prompt(q8, kwin, vwin, batch, seq):
    tq = TKV
    nkt = seq // TKV
    nwt = WINDOW // TKV + 1
    assert nkt >= nwt
    kv = pl.BlockSpec((1, 1, nkt, LANES, TKV), lambda b, g, q: (b, g, 0, 0, 0))
    return pl.pallas_call(
        functools.partial(_win_prompt_kernel, tq=tq, nwt=nwt),
        grid=(batch, NSA_KV_HEADS, seq // tq),
        in_specs=[pl.BlockSpec((1, NSA_REP, tq, LANES), lambda b, g, q: (b, g, q, 0)), kv, kv],
        out_specs=pl.BlockSpec((1, tq, 256), lambda b, g, q: (b, q, g)),
        out_shape=jax.ShapeDtypeStruct((batch, seq, 512), F32),
        compiler_params=_cparams(("arbitrary", "arbitrary", "arbitrary")), name="nsa_win_prompt",
    )(q8, kwin, vwin)


def _out_kernel(x_ref, nw_ref, d_ref, ocmp_ref, osel_ref, owin_ref, oc_ref,
                wg_ref, pw_ref, ps_ref, wa_ref, wb_ref, wc_ref, wo_ref, e3_ref, fn_ref, xo_ref, y_ref):
    x = x_ref[...]
    ms = jnp.mean(x * x, axis=-1, keepdims=True)
    hb = ((x * lax.rsqrt(ms + RMS_EPS)) * nw_ref[...]).astype(BF16)

    def gate(c0, n):
        return _dot(hb, wg_ref[:, c0:c0 + n])

    oa = _dot(d_ref[...].astype(BF16), pw_ref[...]) * ps_ref[...]
    ya = _dot((oa * jax.nn.silu(gate(G_ZA, 256))).astype(BF16), wa_ref[...])
    gx = _dot3(jax.nn.sigmoid(gate(G_SM, LANES)), e3_ref[...])
    ob = gx[:, 0:512] * ocmp_ref[...] + gx[:, 512:1024] * osel_ref[...] + gx[:, 1024:1536] * owin_ref[...]
    yb = _dot((ob * jax.nn.silu(gate(G_ZB, 512))).astype(BF16), wb_ref[...])
    yc = _dot((oc_ref[...] * jax.nn.silu(gate(G_ZC, 256))).astype(BF16), wc_ref[...])
    d = ya.shape[-1]
    mixed = (jax.nn.sigmoid(gate(G_GM, d)) * ya + jax.nn.sigmoid(gate(G_GM + d, d)) * yb
             + jax.nn.sigmoid(gate(G_GM + 2 * d, d)) * yc)
    xo = x + _dot(mixed.astype(BF16), wo_ref[...])
    xo_ref[...] = xo
    ms = jnp.mean(xo * xo, axis=-1, keepdims=True)
    y_ref[...] = (xo * lax.rsqrt(ms + RMS_EPS)) * fn_ref[...]


def _gate_expand_matrix():
    e = np.zeros((LANES, 3 * NSA_HEADS * HEAD_DIM), np.float32)
    for hd in range(NSA_HEADS):
        for c in range(3):
            e[SM_GB + hd * 3 + c, c * 512 + hd * HEAD_DIM:c * 512 + (hd + 1) * HEAD_DIM] = 1.0
    return jnp.asarray(e, BF16)


def _output(x2d, norm_w, d, ocmp, osel, owin, oc, lw, final_norm):
    n, dm = x2d.shape
    tm = min(256, n)
    row = lambda w: pl.BlockSpec((tm, w), lambda i: (i, 0))
    full = lambda a: pl.BlockSpec(a.shape, lambda i: (0,) * a.ndim)
    consts = [lw["w_gate"], lw["pool_bd"], lw["pool_scale"], lw["w_out_a"], lw["w_out_b"], lw["w_out_c"],
              lw["w_o"], _gate_expand_matrix(), final_norm.reshape(1, dm)]
    nw = norm_w.reshape(1, dm)
    return pl.pallas_call(
        _out_kernel, grid=(n // tm,),
        in_specs=[row(dm), full(nw), row(256), row(512), row(512), row(512), row(256)] + [full(c) for c in consts],
        out_specs=[row(dm), row(dm)],
        out_shape=[jax.ShapeDtypeStruct((n, dm), F32), jax.ShapeDtypeStruct((n, dm), F32)],
        compiler_params=_cparams(("arbitrary",)), name="out_proj",
    )(x2d, nw, d, ocmp, osel, owin, oc, *consts)


class _PageStream:
    def __init__(self, cache_ref, pt_ref, buf_ref, sem_ref, *, n_pages, nch, ppc, rows, nrows, look):
        self.refs = (cache_ref, pt_ref, buf_ref, sem_ref)
        self.n_pages, self.nch, self.ppc, self.rows, self.nrows, self.look = n_pages, nch, ppc, rows, nrows, look
        self.steps = len(rows) * nch
        self.nslot = look + 1
        self.g0 = pl.program_id(0) * self.steps
        self.total = pl.num_programs(0) * self.steps

    def _copies(self, g):
        cache_ref, pt_ref, buf_ref, sem_ref = self.refs
        f = g % self.steps
        row0 = self.rows[0]
        for p in range(1, len(self.rows)):
            row0 = jnp.where(f // self.nch == p, self.rows[p], row0)
        if len(self.rows) > 1:
            row0 = pl.multiple_of(row0, 8)
        base = (g // self.steps) * self.n_pages + (f % self.nch) * self.ppc
        slot = g % self.nslot
        return [pltpu.make_async_copy(cache_ref.at[pt_ref[base + i], pl.ds(row0, self.nrows), :],
                                      buf_ref.at[slot, :, pl.ds(i * PAGE, PAGE)], sem_ref.at[slot])
                for i in range(self.ppc)]

    def prime(self):
        @pl.when(pl.program_id(0) == 0)
        def _():
            for g in range(self.look):
                for cp in self._copies(g):
                    cp.start()

    def fetch(self, f):
        g = self.g0 + f

        @pl.when(g + self.look < self.total)
        def _():
            for cp in self._copies(g + self.look):
                cp.start()

        for cp in self._copies(g):
            cp.wait()
        return g % self.nslot


def _topk_extract(val, k_top):
    lane = _lane(val.shape)
    sel = jnp.zeros(val.shape, jnp.bool_)
    for _ in range(k_top):
        mx = jnp.max(val, axis=-1, keepdims=True)
        idx = jnp.min(jnp.where(val == mx, lane, val.shape[-1]), axis=-1, keepdims=True)
        pick = lane == idx
        sel = sel | pick
        val = jnp.where(pick, -3e38, val)
    return sel


def _topk_decode_kernel(val_ref, sel_ref, *, k_top):
    val = val_ref[...]
    sel = _topk_extract(val, k_top) & (val > -0.5)
    sel_ref[...] = jnp.where(sel, 0.0, -1.0)


def _topk_decode(val, k_top):
    return pl.pallas_call(
        functools.partial(_topk_decode_kernel, k_top=k_top), grid=(1,),
        in_specs=[pl.BlockSpec(val.shape, lambda i: (0, 0))],
        out_specs=pl.BlockSpec(val.shape, lambda i: (0, 0)),
        out_shape=jax.ShapeDtypeStruct(val.shape, F32),
        compiler_params=_cparams(("arbitrary",)), name="nsa_topk_decode",
    )(val)


def _cmp_decode_kernel(pt_ref, cache_ref, q_ref, amean_ref, pair_ref, o_ref, val_ref, buf_ref, sem_ref, kcs_ref,
                       *, n_pages, nch, ppc, past):
    stream = _PageStream(cache_ref, pt_ref, buf_ref, sem_ref, n_pages=n_pages, nch=nch, ppc=ppc,
                         rows=(0,), nrows=256, look=1)
    stream.prime()

    def chunk(c, carry):
        slot = stream.fetch(c)
        kcs_ref[c] = _dot2(buf_ref[slot], amean_ref[...])
        return carry

    lax.fori_loop(0, nch, chunk, 0)
    nb = n_pages * (PAGE // CMP_BLOCK)
    nsd = nb // 2
    bpc = ppc * (PAGE // CMP_BLOCK)
    means = jnp.concatenate([kcs_ref[cc][:, 0:bpc] for cc in range(nch)], axis=1).astype(BF16)
    for g in range(2):
        s = _dot(q_ref[0, g], means[0:LANES])
        nblk = _lane(s.shape)
        qpos = past + _row(s.shape) % 8
        p = _masked_softmax(s, (nblk + 1) * CMP_BLOCK - 1 <= qpos, -1)
        o_ref[0, g] = _dot_nt(p.astype(BF16), means[LANES:2 * LANES])
        imp = p[0:8]
        for r in range(1, NSA_REP):
            imp = imp + p[r * 8:(r + 1) * 8]
        imp = jnp.concatenate([_dot3(imp, pair_ref[...]), jnp.zeros((8, LANES), F32)], axis=1)
        blk = _lane(imp.shape)
        cur = (past + _row(imp.shape)) // SEL_BLOCK
        forced = (blk == 0) | (blk == cur) | (blk == cur - 1)
        val = jnp.where(blk <= cur, imp + jnp.where(forced, FORCE_SCORE, 0.0), -1.0)
        val_ref[0, g] = jnp.where(blk <= nsd, val, -3e38)


def _cmp_decode(pt_flat, cache, qsw, *, batch, n_pages, past):
    ppc = min(32, n_pages)
    nch = n_pages // ppc
    bpc = ppc * (PAGE // CMP_BLOCK)
    assert bpc % LANES == 0 or nch == 1
    nb = n_pages * (PAGE // CMP_BLOCK)
    nsd = nb // 2
    tok = np.arange(ppc * PAGE)
    amean = np.zeros((ppc * PAGE, LANES), np.float32)
    amean[tok, tok // CMP_BLOCK] = 1.0 / CMP_BLOCK
    pair = (np.arange(nb)[:, None] // 2 == np.arange(nsd)[None, :]).astype(np.float32)
    consts = [jnp.asarray(amean, BF16), jnp.asarray(pair, BF16)]
    grid_spec = pltpu.PrefetchScalarGridSpec(
        num_scalar_prefetch=1, grid=(batch,),
        in_specs=[pl.BlockSpec(memory_space=pl.ANY),
                  pl.BlockSpec((1, 2, 32, LANES), lambda b, pt: (b, 0, 0, 0))]
                 + [pl.BlockSpec(a.shape, lambda b, pt: (0, 0)) for a in consts],
        out_specs=[pl.BlockSpec((1, 2, 32, LANES), lambda b, pt: (b, 0, 0, 0)),
                   pl.BlockSpec((1, 2, 8, nsd + LANES), lambda b, pt: (b, 0, 0, 0))],
        scratch_shapes=[pltpu.VMEM((2, 256, ppc * PAGE), F32), pltpu.SemaphoreType.DMA((2,)),
                        pltpu.VMEM((nch, 256, LANES), F32)])
    return pl.pallas_call(
        functools.partial(_cmp_decode_kernel, n_pages=n_pages, nch=nch, ppc=ppc, past=past),
        grid_spec=grid_spec,
        out_shape=[jax.ShapeDtypeStruct((batch, 2, 32, LANES), F32),
                   jax.ShapeDtypeStruct((batch, 2, 8, nsd + LANES), F32)],
        compiler_params=_cparams(("arbitrary",)), name="nsa_cmp_decode",
    )(pt_flat, cache, qsw, *consts)


ATTN_LOOK = 2


def _attn_decode_kernel(pt_ref, cache_ref, q_ref, bias_ref, bnew_ref, aux_ref, cq_ref, new_ref, o_ref,
                        buf_ref, sem_ref, s_ref, *, n_pages, nch, ppc, krow, vrow, kw, ncols, fox):
    ng = LANES // ncols
    tk = ppc * PAGE
    rows_s = (nch // ng) * tk
    stream = _PageStream(cache_ref, pt_ref, buf_ref, sem_ref, n_pages=n_pages, nch=nch, ppc=ppc,
                         rows=(krow, vrow), nrows=kw, look=ATTN_LOOK)
    stream.prime()
    s_ref[0:rows_s, :] = jnp.zeros((rows_s, LANES), F32)

    def scores(kt, k, bias_rows):
        s = _dot_tn(kt, q_ref[0, k])
        if fox:
            pieces = jnp.concatenate(list(_split3(bias_rows)) + [jnp.zeros(bias_rows.shape, BF16)], axis=0)
            s = s + _dot_tn(pieces, aux_ref[k]) + cq_ref[0, k, 0:1, :]
        return s

    def key_chunk(c, carry):
        slot = stream.fetch(c)
        k = c % ng
        kt = buf_ref[slot].astype(BF16)
        if fox:
            s = scores(kt, k, bias_ref[0, c])
        else:
            bpc = tk // SEL_BLOCK
            m = bias_ref[0, k, pl.ds(pl.multiple_of(c * bpc, 8), bpc), :]
            m = jnp.broadcast_to(m[:, None, :], (bpc, SEL_BLOCK, LANES)).reshape(tk, LANES)
            s = jnp.where(m > -0.5, scores(kt, k, None), NEG_INF)
        rows = pl.ds(pl.multiple_of((c // ng) * tk, tk), tk)
        s_ref[rows, :] += s
        return carry

    lax.fori_loop(0, nch, key_chunk, 0)

    kn = new_ref[0, 0:kw, :].astype(BF16)
    if fox:
        sn = scores(kn, 0, bnew_ref[0])
        tq = _lane(sn.shape) // 8
    else:
        sn = jnp.where(bnew_ref[0, 0:1, :] > -0.5, scores(kn, 0, None), NEG_INF)
        tq = _lane(sn.shape) % 8
    sn = jnp.where((_row(sn.shape) <= tq) & (_lane(sn.shape) < ncols), sn, NEG_INF)
    s_ref[rows_s:rows_s + PAGE, :] = sn

    def all_groups(v, op):
        v8 = jnp.broadcast_to(v, (8, LANES))
        out = v8
        for kk in range(1, ng):
            out = op(out, pltpu.roll(v8, kk * ncols, 1))
        return out[0:1]

    s = s_ref[...]
    ok = s > 0.5 * NEG_INF
    mx = all_groups(jnp.max(s, axis=0, keepdims=True), jnp.maximum)
    e = jnp.where(ok, jnp.exp(s - mx), 0.0)
    den = all_groups(jnp.sum(e, axis=0, keepdims=True), jnp.add)
    s_ref[...] = e / jnp.maximum(den, 1e-30)

    def value_chunk(c, acc):
        slot = stream.fetch(nch + c)
        p = s_ref[pl.ds(pl.multiple_of((c // ng) * tk, tk), tk), :]
        p = jnp.where(_lane(p.shape) // ncols == c % ng, p, 0.0).astype(BF16)
        return acc + _dot(buf_ref[slot].astype(BF16), p)

    acc = lax.fori_loop(0, nch, value_chunk, jnp.zeros((kw, LANES), F32))
    acc = acc + _dot(new_ref[0, kw:2 * kw, :].astype(BF16), s_ref[rows_s:rows_s + PAGE, :].astype(BF16))
    out = acc
    for kk in range(1, ng):
        out = out + pltpu.roll(acc, kk * ncols, 1)
    o_ref[0] = out


def _attn_decode(pt_flat, cache, q, bias, bnew, aux, cq, new_rows, *, batch, n_pages, krow, vrow, kw, ncols, fox,
                 name):
    ng = LANES // ncols
    ppc = min(32, n_pages // ng)
    nch = n_pages // ppc
    assert nch % ng == 0 and n_pages % ppc == 0
    tk = ppc * PAGE
    blk = lambda a: pl.BlockSpec((1,) + a.shape[1:], lambda b, pt: (b,) + (0,) * (a.ndim - 1))
    grid_spec = pltpu.PrefetchScalarGridSpec(
        num_scalar_prefetch=1, grid=(batch,),
        in_specs=[pl.BlockSpec(memory_space=pl.ANY), blk(q), blk(bias), blk(bnew),
                  pl.BlockSpec(aux.shape, lambda b, pt: (0, 0, 0)), blk(cq), blk(new_rows)],
        out_specs=pl.BlockSpec((1, kw, LANES), lambda b, pt: (b, 0, 0)),
        scratch_shapes=[pltpu.VMEM((ATTN_LOOK + 1, kw, tk), F32), pltpu.SemaphoreType.DMA((ATTN_LOOK + 1,)),
                        pltpu.VMEM(((nch // ng) * tk + PAGE, LANES), F32)])
    return pl.pallas_call(
        functools.partial(_attn_decode_kernel, n_pages=n_pages, nch=nch, ppc=ppc, krow=krow, vrow=vrow,
                          kw=kw, ncols=ncols, fox=fox),
        grid_spec=grid_spec, out_shape=jax.ShapeDtypeStruct((batch, kw, LANES), F32),
        compiler_params=_cparams(("arbitrary",)), name=name,
    )(pt_flat, cache, q, bias, bnew, aux, cq, new_rows)


def _win_decode_kernel(q_ref, st_ref, new_ref, o_ref):
    st = st_ref[0].astype(BF16)
    nw = new_ref[0].astype(BF16)
    lw = st.shape[1]
    for g in range(2):
        q = q_ref[0, g]
        s = jnp.concatenate([_dot(q, st[0:LANES]), _dot(q, nw[0:LANES])], axis=1)
        i = _lane(s.shape)
        t = _row(s.shape) % 8
        p = _masked_softmax(s, (i > t + lw - WINDOW) & (i <= t + lw), -1).astype(BF16)
        o_ref[0, g] = _dot_nt(p[:, 0:lw], st[LANES:2 * LANES]) + _dot_nt(p[:, lw:], nw[LANES:2 * LANES])


def _win_decode(qsw, state_t, new_t, batch):
    lw = state_t.shape[2]
    return pl.pallas_call(
        _win_decode_kernel, grid=(batch,),
        in_specs=[pl.BlockSpec((1, 2, 32, LANES), lambda b: (b, 0, 0, 0)),
                  pl.BlockSpec((1, 256, lw), lambda b: (b, 0, 0)),
                  pl.BlockSpec((1, 256, PAGE), lambda b: (b, 0, 0))],
        out_specs=pl.BlockSpec((1, 2, 32, LANES), lambda b: (b, 0, 0, 0)),
        out_shape=jax.ShapeDtypeStruct((batch, 2, 32, LANES), F32),
        compiler_params=_cparams(("arbitrary",)), name="nsa_win_decode",
    )(qsw, state_t, new_t)


def _logf_decode_kernel(pt_ref, cache_ref, new_ref, u_ref, s_ref, perm_ref, tri_ref, c_ref, buf_ref, sem_ref,
                        *, n_pages):
    b = pl.program_id(0)
    nb = pl.num_programs(0)

    def copies(bb, slot):
        return [pltpu.make_async_copy(cache_ref.at[pl.ds(pt_ref[bb * n_pages + i] * FOX_HEADS, FOX_HEADS), :],
                                      buf_ref.at[slot, pl.ds(i * FOX_HEADS, FOX_HEADS), :], sem_ref.at[slot])
                for i in range(n_pages)]

    slot = b % 2

    @pl.when(b == 0)
    def _():
        for cp in copies(b, slot):
            cp.start()

    @pl.when(b + 1 < nb)
    def _():
        for cp in copies(b + 1, 1 - slot):
            cp.start()

    for cp in copies(b, slot):
        cp.wait()
    lp = jnp.concatenate([buf_ref[slot], new_ref[0]], axis=0)
    cs = _dot3(lp, u_ref[...])
    tot = _dot3(cs, s_ref[...])
    c_ref[0] = _dot3_left(perm_ref[...], cs) + _dot3_left(tri_ref[...], tot)


def _logf_decode(pt_flat, cache2d, new_rows, *, batch, n_pages):
    nr = (n_pages + 2) * FOX_HEADS
    r = np.arange(PAGE)
    u = (r[:, None] <= r[None, :]).astype(np.float32)
    last = np.zeros((PAGE, PAGE), np.float32)
    last[PAGE - 1, :] = 1.0
    rr = np.arange(nr)
    tri = ((rr[:, None] % FOX_HEADS == rr[None, :] % FOX_HEADS)
           & (rr[None, :] // FOX_HEADS < rr[:, None] // FOX_HEADS)).astype(np.float32)
    perm = np.zeros((nr, nr), np.float32)
    perm[(rr % FOX_HEADS) * (n_pages + 2) + rr // FOX_HEADS, rr] = 1.0
    consts = [jnp.asarray(u, BF16), jnp.asarray(last, BF16), jnp.asarray(perm, BF16), jnp.asarray(perm @ tri, BF16)]
    grid_spec = pltpu.PrefetchScalarGridSpec(
        num_scalar_prefetch=1, grid=(batch,),
        in_specs=[pl.BlockSpec(memory_space=pl.ANY),
                  pl.BlockSpec((1, 2 * FOX_HEADS, PAGE), lambda b, pt: (b, 0, 0))]
                 + [pl.BlockSpec(c.shape, lambda b, pt: (0, 0)) for c in consts],
        out_specs=pl.BlockSpec((1, nr, PAGE), lambda b, pt: (b, 0, 0)),
        scratch_shapes=[pltpu.VMEM((2, n_pages * FOX_HEADS, PAGE), F32), pltpu.SemaphoreType.DMA((2,))])
    return pl.pallas_call(
        functools.partial(_logf_decode_kernel, n_pages=n_pages),
        grid_spec=grid_spec, out_shape=jax.ShapeDtypeStruct((batch, nr, PAGE), F32),
        compiler_params=_cparams(("arbitrary",)), name="fox_logf_decode",
    )(pt_flat, cache2d, new_rows, *consts)


def _prep_layer_weights(w_in_t, pool_w, pool_scale, w_out_a, w_out_b, w_out_c, w_o):
    d = w_in_t.shape[1]
    tok = jnp.concatenate([w_in_t[0:256], w_in_t[512:1024], w_in_t[2328:2584]], axis=0)
    gate = jnp.concatenate([w_in_t[256:512], w_in_t[1816:2328], w_in_t[3100:6428], w_in_t[1792:1816],
                            jnp.zeros((LANES - 24, d), F32)], axis=0)
    feat = jnp.concatenate([w_in_t[1024:1792], w_in_t[2584:3100], jnp.zeros((8 - FOX_HEADS, d), F32)],
                           axis=0).astype(BF16)
    gd = pool_w.shape[-1]
    bd = jnp.zeros((POOL_WIDTH, POOL_WIDTH), F32)
    for g in range(pool_w.shape[0]):
        bd = bd.at[g * gd:(g + 1) * gd, g * gd:(g + 1) * gd].set(pool_w[g])
    return dict(w_tok=_wprep(tok), w_gate=_wprep(gate), w_feat=feat, pool_bd=bd.astype(BF16),
                pool_scale=pool_scale.reshape(1, -1),
                w_out_a=w_out_a.astype(BF16), w_out_b=w_out_b.astype(BF16), w_out_c=w_out_c.astype(BF16),
                w_o=w_o.astype(BF16))


def _to_token_last(a_t, lead):
    b, _, t = a_t.shape
    nd = len(lead)
    return a_t.reshape((b,) + lead + (t,)).transpose((0, nd + 1) + tuple(range(1, nd + 1)))


def _layer_prompt(x2d, lw, norm_w, fox_bf, final_norm, batch, seq):
    pos = jnp.arange(seq)
    (uz, q8, nsat, wint, foxt, logft, fq, kcm, ksel, vsel, kwin, vwin, fk, fv) = _project(
        x2d, norm_w, lw["w_tok"], lw["w_feat"], fox_bf, pos, prompt=True, batch=batch, seq=seq)
    d = _pool_prompt(uz, seq)
    bpt = TKV // CMP_BLOCK
    nb = seq // CMP_BLOCK
    means = kcm[..., :bpt].transpose(0, 1, 3, 2).reshape(batch, nb, 256)
    means = jnp.concatenate([means[:, 0::2], means[:, 1::2]], axis=1)
    dup = lambda m: jnp.stack([jnp.concatenate([m[..., g * HEAD_DIM:(g + 1) * HEAD_DIM]] * 2, axis=-1)
                               for g in range(2)], axis=1).astype(BF16)
    ocmp, qaug = _cmp_prompt(dup(means[..., 0:LANES]), dup(means[..., LANES:2 * LANES]), q8, batch, seq)
    osel = _flash(qaug, ksel, vsel, n_sub=1, n_stack=NSA_REP, tpt=LANES, name="nsa_sel_prompt")
    owin = _win_prompt(q8, kwin, vwin, batch, seq)
    tpt = 512
    oc = _flash(fq.reshape(batch, FOX_HEADS, seq // tpt, tpt, LANES), fk, fv,
                n_sub=2, n_stack=1, tpt=tpt, name="fox_prompt")
    n = batch * seq
    xo, y = _output(x2d, norm_w, d, ocmp.reshape(n, 512), osel.reshape(n, 512), owin.reshape(n, 512),
                    oc.reshape(n, 256), lw, final_norm)
    wl = min(WINDOW, seq)
    states = (_to_token_last(nsat, (4, NSA_KV_HEADS, HEAD_DIM)),
              _to_token_last(foxt, (2, FOX_HEADS, HEAD_DIM)),
              logft[:, :FOX_HEADS].transpose(0, 2, 1),
              _to_token_last(wint[:, :, seq - wl:], (2, NSA_KV_HEADS, HEAD_DIM)),
              uz[:, :POOL_WIDTH].reshape(batch, seq, POOL_WIDTH)[:, -POOL_STATE:])
    return xo, y, states


def _new_page(a_t, batch, tdec):
    f = a_t.shape[0]
    a = a_t.reshape(f, batch, tdec).transpose(1, 0, 2)
    return jnp.concatenate([a, jnp.zeros((batch, f, PAGE - tdec), a.dtype)], axis=2)


def _layer_decode(x2d, lw, norm_w, fox_bf, final_norm, pt_flat, nsa_cache, fox_cache, logf_cache,
                  state_win_t, state_pool_tb, batch, tdec, n_pages):
    past = n_pages * PAGE
    n = batch * tdec
    pos = past + jnp.tile(jnp.arange(tdec), batch)
    (uz, q8, nsat, wint, foxt, logft, qc) = _project(
        x2d, norm_w, lw["w_tok"], lw["w_feat"], fox_bf, pos, prompt=False, batch=batch, seq=tdec)
    nsat, wint, foxt, logft = nsat[0], wint[0], foxt[0], logft[0]

    u_tb = uz[:, :POOL_WIDTH].reshape(batch, tdec, POOL_WIDTH).transpose(1, 0, 2)
    pool_seq = jnp.concatenate([state_pool_tb, u_tb], axis=0)
    d = _pool_decode(pool_seq, past, tdec).transpose(1, 0, 2).reshape(n, POOL_WIDTH)

    q5 = q8[0].reshape(NSA_KV_HEADS, NSA_REP, batch, tdec, LANES)
    qs = q5.transpose(2, 0, 1, 3, 4)
    qs = jnp.concatenate([qs, jnp.zeros((batch, 2, NSA_REP, 8 - tdec, LANES), BF16)], axis=3)
    qs = qs.reshape(batch, 2, NSA_REP * 8, LANES)
    qsw = jnp.stack([qs[:, 0], jnp.roll(qs[:, 1], HALF, axis=-1)], axis=1)
    ocmp_k, val = _cmp_decode(pt_flat, nsa_cache, qsw, batch=batch, n_pages=n_pages, past=past)
    selm = _topk_decode(val.reshape(batch * 16, -1), min(SEL_TOPK, past // SEL_BLOCK + 1)).reshape(val.shape)

    def unstack_groups(o):
        o = o.reshape(batch, 2, NSA_REP, 8, 2, HEAD_DIM)[:, :, :, :tdec]
        o = jnp.stack([o[:, 0, :, :, 0], o[:, 1, :, :, 1]], axis=1)
        return o.transpose(0, 3, 1, 2, 4).reshape(n, NSA_HEADS * HEAD_DIM)

    ocmp = unstack_groups(ocmp_k)
    owin = unstack_groups(_win_decode(qsw, state_win_t, _new_page(wint, batch, tdec), batch))

    nsd = past // SEL_BLOCK
    q6 = q5[..., :HEAD_DIM].transpose(2, 1, 0, 3, 4)
    q6 = jnp.concatenate([q6, jnp.zeros((batch, NSA_REP, 2, 8 - tdec, HEAD_DIM), BF16)], axis=3)
    eye2 = jnp.eye(2, dtype=BF16)
    qbd = q6[:, :, :, :, None, :] * eye2[None, None, :, None, :, None]
    qbd_t = qbd.reshape(batch, 64, LANES).transpose(0, 2, 1)
    qbd_t = jnp.concatenate([qbd_t, jnp.zeros((batch, LANES, LANES - 64), BF16)], axis=2)
    sel_t = jnp.tile(selm.reshape(batch, 16, nsd + LANES).transpose(0, 2, 1), (1, 1, NSA_REP))
    sel_t = jnp.concatenate([sel_t, jnp.zeros((batch, nsd + LANES, LANES - 64), F32)], axis=2)
    groups = lambda a, ng: jnp.stack([jnp.roll(a, k * (LANES // ng), axis=-1) for k in range(ng)], axis=1)
    osel_k = _attn_decode(pt_flat, nsa_cache, groups(qbd_t, 2), groups(sel_t[:, :nsd], 2), sel_t[:, nsd:nsd + 8],
                          jnp.zeros((2, 32, LANES), BF16), jnp.zeros((batch, 2, 8, LANES), F32),
                          _new_page(nsat[256:512], batch, tdec),
                          batch=batch, n_pages=n_pages, krow=256, vrow=384, kw=LANES, ncols=64, fox=False,
                          name="nsa_sel_decode")
    os_ = osel_k[:, :, :64].reshape(batch, 2, HEAD_DIM, NSA_REP, 2, 8)[..., :tdec]
    osel = jnp.stack([os_[:, 0, :, :, 0], os_[:, 1, :, :, 1]], axis=1)
    osel = osel.transpose(0, 4, 1, 3, 2).reshape(n, 512)

    lnew = _new_page(logft, batch, tdec)
    lnew = jnp.concatenate([lnew[:, :FOX_HEADS], jnp.zeros((batch, FOX_HEADS, PAGE), F32)], axis=1)
    csum = _logf_decode(pt_flat, logf_cache, lnew, batch=batch, n_pages=n_pages)
    c4 = csum.reshape(batch, FOX_HEADS, (n_pages + 2) * PAGE)
    ck8 = jnp.concatenate([-c4, jnp.zeros_like(c4)], axis=1)
    cq = jnp.concatenate([c4[:, :, past:past + tdec].transpose(0, 2, 1),
                          jnp.zeros((batch, tdec, 8 - FOX_HEADS), F32)], axis=2).reshape(batch, 1, tdec * 8)
    cq = jnp.concatenate([cq, jnp.zeros((batch, 7, tdec * 8), F32)], axis=1)
    cq = jnp.concatenate([cq, jnp.zeros((batch, 8, LANES - tdec * 8), F32)], axis=2)
    qf = (qc * SCALE).reshape(batch, tdec, FOX_HEADS, HEAD_DIM)
    eye = jnp.eye(8, FOX_HEADS, dtype=F32)
    qfbd = (qf[:, :, None, :, :] * eye[None, None, :, :, None]).reshape(batch, tdec * 8, FOX_HEADS * HEAD_DIM)
    qf_t = jnp.concatenate([qfbd.transpose(0, 2, 1), jnp.zeros((batch, 256, LANES - tdec * 8), F32)],
                           axis=2).astype(BF16)
    col = np.arange(LANES)
    head_cols = np.zeros((32, LANES), np.float32)
    for p in range(3):
        head_cols[p * 8 + col[:tdec * 8] % 8, col[:tdec * 8]] = 1.0
    fng = LANES // (tdec * 8)
    fppc = min(32, n_pages // fng)
    ck_chunks = ck8[:, :, :past].reshape(batch, 8, n_pages // fppc, fppc * PAGE).transpose(0, 2, 1, 3)
    oc_k = _attn_decode(pt_flat, fox_cache, groups(qf_t, fng), ck_chunks, ck8[:, :, past:past + PAGE],
                        groups(jnp.asarray(head_cols, BF16)[None], fng)[0], groups(cq, fng),
                        _new_page(foxt, batch, tdec),
                        batch=batch, n_pages=n_pages, krow=0, vrow=256, kw=256, ncols=tdec * 8, fox=True,
                        name="fox_decode")
    o6 = oc_k[:, :, :tdec * 8].reshape(batch, FOX_HEADS, HEAD_DIM, tdec, 8)
    oc = jnp.stack([o6[:, h, :, :, h] for h in range(FOX_HEADS)], axis=1)
    oc = oc.transpose(0, 3, 1, 2).reshape(n, 256)

    xo, y = _output(x2d, norm_w, d, ocmp, osel, owin, oc, lw, final_norm)
    win_seq_t = jnp.concatenate([state_win_t, wint.reshape(256, batch, tdec).transpose(1, 0, 2)], axis=2)
    states = (nsat.T.reshape(batch, tdec, 4, NSA_KV_HEADS, HEAD_DIM),
              foxt.T.reshape(batch, tdec, 2, FOX_HEADS, HEAD_DIM),
              logft[:FOX_HEADS].T.reshape(batch, tdec, FOX_HEADS),
              _to_token_last(win_seq_t[:, :, -WINDOW:], (2, NSA_KV_HEADS, HEAD_DIM)),
              pool_seq[-POOL_STATE:].transpose(1, 0, 2))
    return xo, y, states


def kernel(x_prompt, x_sample, cache_nsa_kv, cache_fox_kv, cache_fox_logf, state_win_kv, state_pool, page_table,
           norm_w, w_in, pool_w, pool_scale, fox_bf, w_out_a, w_out_b, w_out_c, w_o, final_norm):
    depth = w_in.shape[0]
    batch, seq, dm = x_prompt.shape
    dbatch, tdec, _ = x_sample.shape
    n_phys, page = cache_nsa_kv.shape[1], cache_nsa_kv.shape[2]
    n_pages = page_table.shape[1]
    assert page == PAGE and tdec == 4 and n_pages % min(32, n_pages) == 0
    assert state_win_kv.shape[2] == WINDOW and n_pages * PAGE >= WINDOW

    nsa_cache = cache_nsa_kv.transpose(0, 1, 3, 4, 5, 2).reshape(depth * n_phys, 512, PAGE)
    fox_cache = cache_fox_kv.transpose(0, 1, 3, 4, 5, 2).reshape(depth * n_phys, 512, PAGE)
    logf_cache = cache_fox_logf.transpose(0, 1, 3, 2).reshape(depth * n_phys * FOX_HEADS, PAGE)
    win_state_t = state_win_kv.transpose(0, 1, 3, 4, 5, 2).reshape(depth, dbatch, 256, WINDOW)
    pool_state_tb = state_pool.transpose(0, 2, 1, 3)
    w_in_t = w_in.transpose(0, 2, 1)

    hp = x_prompt.reshape(batch * seq, dm)
    hs = x_sample.reshape(dbatch * tdec, dm)
    st_p, st_s = [], []
    for layer in range(depth):
        lw = _prep_layer_weights(w_in_t[layer], pool_w[layer], pool_scale[layer], w_out_a[layer],
                                 w_out_b[layer], w_out_c[layer], w_o[layer])
        hp, yp, sp = _layer_prompt(hp, lw, norm_w[layer], fox_bf[layer], final_norm, batch, seq)
        pt_flat = (page_table + layer * n_phys).reshape(-1).astype(jnp.int32)
        hs, ys, ss = _layer_decode(hs, lw, norm_w[layer], fox_bf[layer], final_norm, pt_flat, nsa_cache, fox_cache,
                                   logf_cache, win_state_t[layer], pool_state_tb[layer], dbatch, tdec, n_pages)
        st_p.append(sp)
        st_s.append(ss)
    outs = [yp.reshape(batch, seq, dm), ys.reshape(dbatch, tdec, dm)]
    for k in range(5):
        outs.append(jnp.stack([s[k] for s in st_p], 0))
        outs.append(jnp.stack([s[k] for s in st_s], 0))
    return tuple(outs)
```

```python
import functools

import numpy as np
import jax
import jax.numpy as jnp
from jax import lax
from jax.experimental import pallas as pl
from jax.experimental.pallas import tpu as pltpu

F32 = jnp.float32
BF16 = jnp.bfloat16

HEAD_DIM = 64
POOL_WIDTH = 256
POOL_WINDOWS = (2, 4, 8, 16)
POOL_STATE = 15
NSA_HEADS = 8
NSA_KV_HEADS = 2
NSA_REP = NSA_HEADS // NSA_KV_HEADS
CMP_BLOCK = 32
SEL_BLOCK = 64
SEL_TOPK = 16
WINDOW = 512
FORCE_SCORE = 1e4
FOX_HEADS = 4
ROPE_THETA = 10000.0
RMS_EPS = 1e-6
NEG_INF = -1e30
SCALE = HEAD_DIM ** -0.5

LANES = 128
HALF = LANES // 2
PAGE = 128
VMEM_LIMIT = 56 * 1024 * 1024

C_U, C_QB, C_QC, C_END = 0, 256, 768, 1024
G_ZA, G_ZB, G_ZC, G_GM, G_SM, G_END = 0, 256, 768, 1024, 4096, 4224
R_KV, R_KC, R_VC, R_FC, R_END = 0, 768, 1024, 1280, 1288
SM_GB = 0
TKV = 256
FT = 512


def _cparams(sem):
    return pltpu.CompilerParams(dimension_semantics=sem, vmem_limit_bytes=VMEM_LIMIT)


def _lane(shape):
    return lax.broadcasted_iota(jnp.int32, shape, len(shape) - 1)


def _row(shape):
    return lax.broadcasted_iota(jnp.int32, shape, len(shape) - 2)


def _split3(x):
    hi = x.astype(BF16)
    r1 = x - hi.astype(F32)
    mid = r1.astype(BF16)
    lo = (r1 - mid.astype(F32)).astype(BF16)
    return hi, mid, lo


def _dot(a, b):
    return jnp.dot(a, b, preferred_element_type=F32)


def _dot_nt(a, b):
    return lax.dot_general(a, b, (((1,), (1,)), ((), ())), preferred_element_type=F32)


def _dot_tn(a, b):
    return lax.dot_general(a, b, (((0,), (0,)), ((), ())), preferred_element_type=F32)


def _dot3(x, m):
    n = x.shape[0]
    r = _dot(jnp.concatenate(_split3(x), axis=0), m)
    return r[0:n] + r[n:2 * n] + r[2 * n:3 * n]


def _dot2(x, m):
    n = x.shape[0]
    hi = x.astype(BF16)
    lo = (x - hi.astype(F32)).astype(BF16)
    r = _dot(jnp.concatenate([hi, lo], axis=0), m)
    return r[0:n] + r[n:2 * n]


def _dot3_left(m, x):
    n = x.shape[1]
    r = _dot(m, jnp.concatenate(_split3(x), axis=1))
    return r[:, 0:n] + r[:, n:2 * n] + r[:, 2 * n:3 * n]


def _swap_halves(x):
    return pltpu.roll(x, HALF, x.ndim - 1)


def _to_low_half(x, g, fill):
    lo = _lane(x.shape) < HALF
    return jnp.where(lo, x if g == 0 else _swap_halves(x), fill)


def _pair_heads(pieces):
    out = []
    for j in range(0, len(pieces), 2):
        lo = _lane(pieces[j].shape) < HALF
        out.append(jnp.where(lo, pieces[j], pieces[j + 1]))
    return out


def _masked_softmax(s, ok, axis):
    s = jnp.where(ok, s, NEG_INF)
    m = jnp.max(s, axis=axis, keepdims=True)
    e = jnp.where(ok, jnp.exp(s - m), 0.0)
    return e * (1.0 / jnp.maximum(jnp.sum(e, axis=axis, keepdims=True), 1e-30))


def _wprep_kernel(wt_ref, w_ref):
    w_ref[...] = wt_ref[...].T.astype(BF16)


def _wprep(wt):
    cols, d = wt.shape
    return pl.pallas_call(
        _wprep_kernel, grid=(cols // LANES,),
        in_specs=[pl.BlockSpec((LANES, d), lambda i: (i, 0))],
        out_specs=pl.BlockSpec((d, LANES), lambda i: (0, i)),
        out_shape=jax.ShapeDtypeStruct((d, cols), BF16),
        compiler_params=_cparams(("arbitrary",)), name="w_in_transpose",
    )(wt)


def _rope_tile(x, cosf, sinf):
    lane = _lane(x.shape)
    first = (lane % HEAD_DIM) < (HEAD_DIM // 2)
    rot = jnp.where(first, pltpu.roll(x, LANES - HEAD_DIM // 2, 1), pltpu.roll(x, HEAD_DIM // 2, 1))
    return x * cosf + rot * sinf


def _rope_rows(x, cos_t, sin_t):
    half = HEAD_DIM // 2
    out = []
    for h in range(x.shape[0] // HEAD_DIM):
        x1 = x[h * HEAD_DIM:h * HEAD_DIM + half]
        x2 = x[h * HEAD_DIM + half:(h + 1) * HEAD_DIM]
        out += [x1 * cos_t - x2 * sin_t, x2 * cos_t + x1 * sin_t]
    return jnp.concatenate(out, axis=0)


def _proj_kernel(*refs, prompt, tiles_per_seq):
    if prompt:
        (x_ref, nw_ref, wt_ref, wk_ref, cos_ref, sin_ref, cost_ref, sint_ref, bfc_ref,
         ohb_ref, amean_ref, triu_ref, sq_ref,
         u_ref, q8_ref, nsat_ref, wint_ref, foxt_ref, logft_ref,
         fq_ref, kcm_ref, ksel_ref, vsel_ref, kwin_ref, vwin_ref, fk_ref, fv_ref, carry_ref) = refs
    else:
        (x_ref, nw_ref, wt_ref, wk_ref, cos_ref, sin_ref, cost_ref, sint_ref, bfc_ref,
         u_ref, q8_ref, nsat_ref, wint_ref, foxt_ref, logft_ref, qc_ref) = refs

    x = x_ref[...]
    ms = jnp.mean(x * x, axis=-1, keepdims=True)
    hb = ((x * lax.rsqrt(ms + RMS_EPS)) * nw_ref[...]).astype(BF16)
    tm = x.shape[0]

    def seg(c0, n):
        return _dot(hb, wt_ref[:, c0:c0 + n])

    def seg_t(r0, n):
        return _dot_nt(wk_ref[r0:r0 + n, :], hb)

    u_ref[...] = seg(C_U, POOL_WIDTH)
    cosf = cos_ref[...]
    sinf = sin_ref[...]
    for c in range(4):
        y = _rope_tile(seg(C_QB + c * LANES, LANES), cosf, sinf) * SCALE
        q8_ref[0, 2 * c] = _to_low_half(y, 0, 0.0).astype(BF16)
        q8_ref[0, 2 * c + 1] = _to_low_half(y, 1, 0.0).astype(BF16)
    qc = seg(C_QC, 256)

    cos_t = cost_ref[...]
    sin_t = sint_ref[...]
    kcmp = _rope_rows(seg_t(R_KV, LANES), cos_t, sin_t)
    vcmp = seg_t(R_KV + 128, LANES)
    ksel = _rope_rows(seg_t(R_KV + 256, LANES), cos_t, sin_t)
    vsel = seg_t(R_KV + 384, LANES)
    kwin = _rope_rows(seg_t(R_KV + 512, LANES), cos_t, sin_t)
    vwin = seg_t(R_KV + 640, LANES)
    nsat_ref[0, 0:128] = kcmp
    nsat_ref[0, 128:256] = vcmp
    nsat_ref[0, 256:384] = ksel
    nsat_ref[0, 384:512] = vsel
    wint_ref[0, 0:128] = kwin
    wint_ref[0, 128:256] = vwin
    kc = seg_t(R_KC, 256)
    vc = seg_t(R_VC, 256)
    foxt_ref[0, 0:256] = kc
    foxt_ref[0, 256:512] = vc
    f_t = seg_t(R_FC, 8)
    logf = jnp.where(_row(f_t.shape) < FOX_HEADS, jax.nn.log_sigmoid(f_t + bfc_ref[...]), 0.0)
    logft_ref[0] = logf

    if not prompt:
        qc_ref[...] = qc
        return

    kcm_ref[0, 0] = _dot3(jnp.concatenate([kcmp, vcmp], axis=0), amean_ref[...])
    ohb = ohb_ref[...]
    for g in range(2):
        gs = slice(g * HEAD_DIM, (g + 1) * HEAD_DIM)
        ksel_ref[0, g, 0] = jnp.concatenate([ksel[gs].astype(BF16), ohb], axis=0)
        vsel_ref[0, g, 0] = jnp.concatenate([vsel[gs]] * 2, axis=0).astype(BF16)
        kwin_ref[0, g, 0] = jnp.concatenate([kwin[gs]] * 2, axis=0).astype(BF16)
        vwin_ref[0, g, 0] = jnp.concatenate([vwin[gs]] * 2, axis=0).astype(BF16)

    i = pl.program_id(0)
    carry = jnp.where(i % tiles_per_seq == 0, 0.0, carry_ref[:, 0:1])
    csum = _dot3(logf, triu_ref[...]) + carry
    carry_ref[...] = jnp.broadcast_to(csum[:, tm - 1:tm], carry_ref.shape)
    c_hi, c_mid, c_lo = [p.astype(F32) for p in _split3(csum)]
    r8 = _row((8, tm))
    c_tok = jnp.concatenate([csum, jnp.zeros((LANES - 8, tm), F32)], axis=0).T
    cq = _dot(jnp.concatenate(_split3(c_tok), axis=1), sq_ref[...])
    lane = _lane((tm, LANES))
    one_q = ((lane >= HALF + 3) & (lane < HALF + 6)).astype(F32)
    for h in range(FOX_HEADS):
        t, g = h // 2, h % 2
        hs = slice(h * HEAD_DIM, (h + 1) * HEAD_DIM)
        aug = jnp.where(r8 < 3, 1.0, jnp.where(r8 == 3, -c_hi[h:h + 1], jnp.where(
            r8 == 4, -c_mid[h:h + 1], jnp.where(r8 == 5, -c_lo[h:h + 1], 0.0))))
        fk_ref[0, h, 0] = jnp.concatenate([kc[hs], aug, jnp.zeros((HALF - 8, tm), F32)], axis=0).astype(BF16)
        fv_ref[0, h, 0] = jnp.concatenate([vc[hs]] * 2, axis=0).astype(BF16)
        fq_ref[0, h] = _to_low_half(qc[:, t * LANES:(t + 1) * LANES] * SCALE, g,
                                    cq[:, h * LANES:(h + 1) * LANES] + one_q).astype(BF16)


def _rope_tables(pos):
    half = HEAD_DIM // 2
    inv = ROPE_THETA ** (-jnp.arange(half, dtype=F32) / half)
    ang = pos.astype(F32)[:, None] * inv[None, :]
    cos, sin = jnp.cos(ang), jnp.sin(ang)
    return jnp.tile(cos, (1, 4)), jnp.tile(jnp.concatenate([-sin, sin], 1), (1, 2)), cos.T, sin.T


def _fox_query_aug_matrix():
    sq = np.zeros((3 * LANES, FOX_HEADS * LANES), np.float32)
    for h in range(FOX_HEADS):
        for p in range(3):
            sq[p * LANES + h, h * LANES + HALF + p] = 1.0
    return jnp.asarray(sq, BF16)


def _project(x2d, norm_w, w_tok, w_feat, fox_bf, pos, *, prompt, batch, seq):
    n, d = x2d.shape
    tm = TKV if prompt else n
    tiles_per_seq = seq // tm if prompt else 1
    cosf, sinf, cos_t, sin_t = _rope_tables(pos)
    bf_col = jnp.zeros((8, 1), F32).at[0:FOX_HEADS, 0].set(fox_bf)
    row = lambda w: pl.BlockSpec((tm, w), lambda i: (i, 0))
    full = lambda a: pl.BlockSpec(a.shape, lambda i: (0,) * a.ndim)
    if prompt:
        tab = pl.BlockSpec((tm, LANES), lambda i: (i % tiles_per_seq, 0))
        tab_t = pl.BlockSpec((HEAD_DIM // 2, tm), lambda i: (0, i % tiles_per_seq))
    else:
        tab = row(LANES)
        tab_t = pl.BlockSpec((HEAD_DIM // 2, tm), lambda i: (0, i))
    in_arrays = [x2d, norm_w.reshape(1, d), w_tok, w_feat, cosf, sinf, cos_t, sin_t, bf_col]
    in_specs = [row(d), full(in_arrays[1]), full(w_tok), full(w_feat), tab, tab, tab_t, tab_t, full(bf_col)]
    nb = batch if prompt else 1
    bidx = lambda i: (i // tiles_per_seq, 0, i % tiles_per_seq)
    feat = lambda r: pl.BlockSpec((1, r, tm), bidx)
    hb = lambda nh: pl.BlockSpec((1, nh, tm, LANES), lambda i: (i // tiles_per_seq, 0, i % tiles_per_seq, 0))
    nt = (seq if prompt else n)
    out_shapes = [
        jax.ShapeDtypeStruct((n, POOL_WIDTH), F32), jax.ShapeDtypeStruct((nb, NSA_HEADS, nt, LANES), BF16),
        jax.ShapeDtypeStruct((nb, 512, nt), F32), jax.ShapeDtypeStruct((nb, 256, nt), F32),
        jax.ShapeDtypeStruct((nb, 512, nt), F32), jax.ShapeDtypeStruct((nb, 8, nt), F32),
    ]
    out_specs = [row(POOL_WIDTH), hb(NSA_HEADS), feat(512), feat(256), feat(512), feat(8)]
    scratch = []
    if prompt:
        t = np.arange(seq)
        ohb = np.zeros((HALF, seq), np.float32)
        ohb[t // SEL_BLOCK, t] = -NEG_INF
        tt = np.arange(tm)
        amean = np.zeros((tm, LANES), np.float32)
        amean[tt, tt // CMP_BLOCK] = 1.0 / CMP_BLOCK
        triu = np.triu(np.ones((tm, tm), np.float32))
        extra = [jnp.asarray(ohb, BF16), jnp.asarray(amean, BF16), jnp.asarray(triu, BF16), _fox_query_aug_matrix()]
        in_arrays += extra
        in_specs += [pl.BlockSpec((HALF, tm), lambda i: (0, i % tiles_per_seq)),
                     full(extra[1]), full(extra[2]), full(extra[3])]
        nkt = seq // tm
        kvt = lambda ns: pl.BlockSpec((1, ns, 1, LANES, tm),
                                      lambda i: (i // tiles_per_seq, 0, i % tiles_per_seq, 0, 0))
        out_shapes.append(jax.ShapeDtypeStruct((batch, FOX_HEADS, seq, LANES), BF16))
        out_specs.append(hb(FOX_HEADS))
        out_shapes.append(jax.ShapeDtypeStruct((batch, nkt, 256, LANES), F32))
        out_specs.append(pl.BlockSpec((1, 1, 256, LANES), lambda i: (i // tiles_per_seq, i % tiles_per_seq, 0, 0)))
        tpf = FT // tm
        kvf = lambda ns: pl.BlockSpec(
            (1, ns, 1, LANES, tm),
            lambda i: (i // tiles_per_seq, 0, (i % tiles_per_seq) // tpf, 0, (i % tiles_per_seq) % tpf))
        for ns, flash in ((2, True), (2, True), (2, False), (2, False), (FOX_HEADS, True), (FOX_HEADS, True)):
            if flash:
                out_shapes.append(jax.ShapeDtypeStruct((batch, ns, seq // FT, LANES, FT), BF16))
                out_specs.append(kvf(ns))
            else:
                out_shapes.append(jax.ShapeDtypeStruct((batch, ns, nkt, LANES, tm), BF16))
                out_specs.append(kvt(ns))
        scratch = [pltpu.VMEM((8, LANES), F32)]
    else:
        out_shapes.append(jax.ShapeDtypeStruct((n, 256), F32))
        out_specs.append(row(256))
    return pl.pallas_call(
        functools.partial(_proj_kernel, prompt=prompt, tiles_per_seq=tiles_per_seq),
        grid=(n // tm,), in_specs=in_specs, out_specs=out_specs, out_shape=out_shapes,
        scratch_shapes=scratch, compiler_params=_cparams(("arbitrary",)),
        name="proj_prompt" if prompt else "proj_decode",
    )(*in_arrays)


def _pool_mix(u, prev, pos):
    grp = _lane(u.shape) // (POOL_WIDTH // len(POOL_WINDOWS))
    acc = u
    win = None
    for j in range(1, POOL_WINDOWS[-1]):
        acc = acc + prev(j)
        if j + 1 in POOL_WINDOWS:
            k = POOL_WINDOWS.index(j + 1)
            win = acc if win is None else jnp.where(grp >= k, acc, win)
    w = jnp.zeros(u.shape, jnp.int32)
    for k, wk in enumerate(POOL_WINDOWS):
        w = jnp.where(grp == k, wk, w)
    cnt = jnp.minimum(pos + 1, w).astype(F32)
    return win / cnt - u


def _pool_prompt_kernel(u_ref, halo_ref, d_ref, s_ref, *, tiles_per_seq, tm):
    i = pl.program_id(0) % tiles_per_seq
    s_ref[0:16, :] = jnp.where(i == 0, 0.0, halo_ref[...])
    s_ref[16:, :] = u_ref[...]
    pos = i * tm + _row((tm, POOL_WIDTH))
    d_ref[...] = _pool_mix(u_ref[...], lambda j: s_ref[16 - j:16 - j + tm, :], pos)


def _pool_prompt(uz, seq):
    n = uz.shape[0]
    tm = 512
    tiles_per_seq = seq // tm
    return pl.pallas_call(
        functools.partial(_pool_prompt_kernel, tiles_per_seq=tiles_per_seq, tm=tm),
        grid=(n // tm,),
        in_specs=[pl.BlockSpec((tm, POOL_WIDTH), lambda i: (i, 0)),
                  pl.BlockSpec((16, POOL_WIDTH), lambda i: (jnp.maximum(i * (tm // 16) - 1, 0), 0))],
        out_specs=pl.BlockSpec((tm, POOL_WIDTH), lambda i: (i, 0)),
        out_shape=jax.ShapeDtypeStruct((n, POOL_WIDTH), F32),
        scratch_shapes=[pltpu.VMEM((tm + 16, POOL_WIDTH), F32)],
        compiler_params=_cparams(("arbitrary",)), name="pool_prompt",
    )(uz, uz)


def _pool_decode_kernel(seq_ref, d_ref, *, pos0, tdec):
    for t in range(tdec):
        pos = jnp.full(seq_ref.shape[1:], pos0 + t, jnp.int32)
        d_ref[t] = _pool_mix(seq_ref[POOL_STATE + t], lambda j: seq_ref[POOL_STATE + t - j], pos)


def _pool_decode(seq_tb, pos0, tdec):
    _, b, w = seq_tb.shape
    return pl.pallas_call(
        functools.partial(_pool_decode_kernel, pos0=pos0, tdec=tdec),
        grid=(1,),
        in_specs=[pl.BlockSpec(seq_tb.shape, lambda i: (0, 0, 0))],
        out_specs=pl.BlockSpec((tdec, b, w), lambda i: (0, 0, 0)),
        out_shape=jax.ShapeDtypeStruct((tdec, b, w), F32),
        compiler_params=_cparams(("arbitrary",)), name="pool_decode",
    )(seq_tb)


def _select_rank(imp, blk, cur, k_top):
    forced = (blk == 0) | (blk == cur) | (blk == cur - 1)
    val = jnp.where(blk <= cur, imp + jnp.where(forced, FORCE_SCORE, 0.0), -1.0)
    cnt = jnp.zeros(val.shape, jnp.int32)
    for i in range(val.shape[0]):
        vi = val[i:i + 1, :]
        cnt = cnt + ((vi > val) | ((vi == val) & (blk > i))).astype(jnp.int32)
    return (cnt < k_top) & (blk <= cur)


def _cmp_prompt_kernel(kc_ref, vc_ref, q_ref, o_ref, qaug_ref, *, tq, seq):
    qi = pl.program_id(2)
    ns = seq // SEL_BLOCK
    qs = jnp.concatenate([q_ref[0, r] for r in range(NSA_REP)], axis=0)
    s = _dot_nt(kc_ref[0, 0], qs)
    row = _row(s.shape)
    nblk = jnp.where(row < ns, 2 * row, 2 * (row - ns) + 1)
    t = qi * tq + _lane(s.shape) % tq
    p = _masked_softmax(s, (nblk + 1) * CMP_BLOCK - 1 <= t, 0)
    o = _dot_tn(p.astype(BF16), vc_ref[0, 0])
    pieces = _pair_heads([o[r * tq:(r + 1) * tq] for r in range(NSA_REP)])
    for j, piece in enumerate(pieces):
        o_ref[0, :, j * LANES:(j + 1) * LANES] = piece

    imp = p[:, 0:tq]
    for r in range(1, NSA_REP):
        imp = imp + p[:, r * tq:(r + 1) * tq]
    imp = imp[:ns] + imp[ns:]
    blk = _row(imp.shape)
    cur = (qi * tq + _lane(imp.shape)) // SEL_BLOCK
    sel = _select_rank(imp, blk, cur, min(SEL_TOPK, ns))
    selm = jnp.where(sel, 0.0, -1.0)
    pad = [jnp.zeros((HALF, tq), F32), selm]
    if ns < HALF:
        pad.append(jnp.zeros((HALF - ns, tq), F32))
    sel_t = jnp.concatenate(pad, axis=0).T
    for r in range(NSA_REP):
        qaug_ref[0, 0, 0, r * tq:(r + 1) * tq, :] = (q_ref[0, r].astype(F32) + sel_t).astype(BF16)


def _cmp_prompt(kc_dup, vc_dup, q8, batch, seq):
    tq = LANES
    nq = seq // tq
    nb = seq // CMP_BLOCK
    assert seq % (2 * LANES) == 0 and seq // SEL_BLOCK <= HALF
    kv = pl.BlockSpec((1, 1, nb, LANES), lambda b, g, q: (b, g, 0, 0))
    return pl.pallas_call(
        functools.partial(_cmp_prompt_kernel, tq=tq, seq=seq),
        grid=(batch, NSA_KV_HEADS, nq),
        in_specs=[kv, kv, pl.BlockSpec((1, NSA_REP, tq, LANES), lambda b, g, q: (b, g, q, 0))],
        out_specs=[pl.BlockSpec((1, tq, 256), lambda b, g, q: (b, q, g)),
                   pl.BlockSpec((1, 1, 1, NSA_REP * tq, LANES), lambda b, g, q: (b, g, q, 0, 0))],
        out_shape=[jax.ShapeDtypeStruct((batch, seq, 512), F32),
                   jax.ShapeDtypeStruct((batch, NSA_KV_HEADS, nq, NSA_REP * tq, LANES), BF16)],
        compiler_params=_cparams(("arbitrary", "arbitrary", "arbitrary")), name="nsa_cmp_prompt",
    )(kc_dup, vc_dup, q8)


def _flash_kernel(q_ref, k_ref, v_ref, o_ref, *, n_sub, n_stack, tpt):
    qi = pl.program_id(2)
    q0 = qi * tpt
    nfull = q0 // FT
    rows = n_stack * tpt
    qs = [q_ref[0, u, 0] for u in range(n_sub)]

    def step(j, carry, masked):
        ss = [_dot(qs[u], k_ref[0, u, j]) for u in range(n_sub)]
        out = []
        for u in range(n_sub):
            m, l, acc = carry[u]
            s = ss[u]
            if masked:
                key = j * FT + _lane(s.shape)
                tpos = q0 + _row(s.shape) % tpt
                s = jnp.where(key <= tpos, s, NEG_INF)
            mn = jnp.maximum(m, jnp.max(s, axis=-1, keepdims=True))
            a = jnp.exp(m - mn)
            p = jnp.exp(s - mn)
            pv = _dot_nt(p.astype(BF16), v_ref[0, u, j])
            out.append((mn, a * l + jnp.sum(p, axis=-1, keepdims=True), a * acc + pv))
        return tuple(out)

    init = tuple((jnp.full((rows, 1), -1e38, F32), jnp.zeros((rows, 1), F32), jnp.zeros((rows, LANES), F32))
                 for _ in range(n_sub))
    carry = lax.fori_loop(0, nfull, lambda j, c: step(j, c, False), init)
    pieces = []
    for _, l, acc in step(nfull, carry, True):
        o = acc * (1.0 / l)
        pieces += [o[r * tpt:(r + 1) * tpt] for r in range(n_stack)]
    for j, piece in enumerate(_pair_heads(pieces)):
        o_ref[0, :, j * LANES:(j + 1) * LANES] = piece


def _flash(q_aug, k_aug, v_dup, *, n_sub, n_stack, tpt, name):
    b, s, nq, rows, _ = q_aug.shape
    nkt = k_aug.shape[2]
    seq = nkt * FT
    assert FT % tpt == 0 and rows == n_stack * tpt
    wout = n_sub * n_stack * HEAD_DIM
    kv = pl.BlockSpec((1, n_sub, nkt, LANES, FT), lambda bb, ss, q: (bb, ss, 0, 0, 0))
    return pl.pallas_call(
        functools.partial(_flash_kernel, n_sub=n_sub, n_stack=n_stack, tpt=tpt),
        grid=(b, s // n_sub, nq),
        in_specs=[pl.BlockSpec((1, n_sub, 1, rows, LANES), lambda bb, ss, q: (bb, ss, q, 0, 0)), kv, kv],
        out_specs=pl.BlockSpec((1, tpt, wout), lambda bb, ss, q: (bb, q, ss)),
        out_shape=jax.ShapeDtypeStruct((b, seq, s * n_stack * HEAD_DIM), F32),
        compiler_params=_cparams(("arbitrary", "arbitrary", "arbitrary")), name=name,
    )(q_aug, k_aug, v_dup)


def _win_prompt_kernel(q_ref, k_ref, v_ref, o_ref, *, tq, nwt):
    qi = pl.program_id(2)
    q0 = qi * tq
    ts = jnp.maximum(qi - (nwt - 1), 0)
    qs = jnp.concatenate([q_ref[0, r] for r in range(NSA_REP)], axis=0)
    s = jnp.concatenate([_dot(qs, k_ref[0, 0, ts + i]) for i in range(nwt)], axis=1)
    key = ts * TKV + _lane(s.shape)
    t = q0 + _row(s.shape) % tq
    s = jnp.where((key <= t) & (key > t - WINDOW), s, NEG_INF)
    e = jnp.exp(s - jnp.max(s, axis=-1, keepdims=True))
    den = jnp.sum(e, axis=-1, keepdims=True)
    e = e.astype(BF16)
    o = _dot_nt(e[:, 0:TKV], v_ref[0, 0, ts])
    for i in range(1, nwt):
        o = o + _dot_nt(e[:, i * TKV:(i + 1) * TKV], v_ref[0, 0, ts + i])
    o = o * (1.0 / den)
    for j, piece in enumerate(_pair_heads([o[r * tq:(r + 1) * tq] for r in range(NSA_REP)])):
        o_ref[0, :, j * LANES:(j + 1) * LANES] = piece


def _win_prompt(q8, kwin, vwin, batch, seq):
    tq = TKV
    nkt = seq // TKV
    nwt = WINDOW // TKV + 1
    assert nkt >= nwt
    kv = pl.BlockSpec((1, 1, nkt, LANES, TKV), lambda b, g, q: (b, g, 0, 0, 0))
    return pl.pallas_call(
        functools.partial(_win_prompt_kernel, tq=tq, nwt=nwt),
        grid=(batch, NSA_KV_HEADS, seq // tq),
        in_specs=[pl.BlockSpec((1, NSA_REP, tq, LANES), lambda b, g, q: (b, g, q, 0)), kv, kv],
        out_specs=pl.BlockSpec((1, tq, 256), lambda b, g, q: (b, q, g)),
        out_shape=jax.ShapeDtypeStruct((batch, seq, 512), F32),
        compiler_params=_cparams(("arbitrary", "arbitrary", "arbitrary")), name="nsa_win_prompt",
    )(q8, kwin, vwin)


def _out_kernel(x_ref, nw_ref, d_ref, ocmp_ref, osel_ref, owin_ref, oc_ref,
                wg_ref, pw_ref, ps_ref, wa_ref, wb_ref, wc_ref, wo_ref, e3_ref, fn_ref, xo_ref, y_ref):
    x = x_ref[...]
    ms = jnp.mean(x * x, axis=-1, keepdims=True)
    hb = ((x * lax.rsqrt(ms + RMS_EPS)) * nw_ref[...]).astype(BF16)

    def gate(c0, n):
        return _dot(hb, wg_ref[:, c0:c0 + n])

    oa = _dot(d_ref[...].astype(BF16), pw_ref[...]) * ps_ref[...]
    ya = _dot((oa * jax.nn.silu(gate(G_ZA, 256))).astype(BF16), wa_ref[...])
    gx = _dot2(jax.nn.sigmoid(gate(G_SM, LANES)), e3_ref[...])
    ob = gx[:, 0:512] * ocmp_ref[...] + gx[:, 512:1024] * osel_ref[...] + gx[:, 1024:1536] * owin_ref[...]
    yb = _dot((ob * jax.nn.silu(gate(G_ZB, 512))).astype(BF16), wb_ref[...])
    yc = _dot((oc_ref[...] * jax.nn.silu(gate(G_ZC, 256))).astype(BF16), wc_ref[...])
    d = ya.shape[-1]
    mixed = (jax.nn.sigmoid(gate(G_GM, d)) * ya + jax.nn.sigmoid(gate(G_GM + d, d)) * yb
             + jax.nn.sigmoid(gate(G_GM + 2 * d, d)) * yc)
    xo = x + _dot(mixed.astype(BF16), wo_ref[...])
    xo_ref[...] = xo
    ms = jnp.mean(xo * xo, axis=-1, keepdims=True)
    y_ref[...] = (xo * lax.rsqrt(ms + RMS_EPS)) * fn_ref[...]


def _gate_expand_matrix():
    e = np.zeros((LANES, 3 * NSA_HEADS * HEAD_DIM), np.float32)
    for hd in range(NSA_HEADS):
        for c in range(3):
            e[SM_GB + hd * 3 + c, c * 512 + hd * HEAD_DIM:c * 512 + (hd + 1) * HEAD_DIM] = 1.0
    return jnp.asarray(e, BF16)


def _output(x2d, norm_w, d, ocmp, osel, owin, oc, lw, final_norm):
    n, dm = x2d.shape
    tm = min(256, n)
    row = lambda w: pl.BlockSpec((tm, w), lambda i: (i, 0))
    full = lambda a: pl.BlockSpec(a.shape, lambda i: (0,) * a.ndim)
    consts = [lw["w_gate"], lw["pool_bd"], lw["pool_scale"], lw["w_out_a"], lw["w_out_b"], lw["w_out_c"],
              lw["w_o"], _gate_expand_matrix(), final_norm.reshape(1, dm)]
    nw = norm_w.reshape(1, dm)
    return pl.pallas_call(
        _out_kernel, grid=(n // tm,),
        in_specs=[row(dm), full(nw), row(256), row(512), row(512), row(512), row(256)] + [full(c) for c in consts],
        out_specs=[row(dm), row(dm)],
        out_shape=[jax.ShapeDtypeStruct((n, dm), F32), jax.ShapeDtypeStruct((n, dm), F32)],
        compiler_params=_cparams(("arbitrary",)), name="out_proj",
    )(x2d, nw, d, ocmp, osel, owin, oc, *consts)


class _PageStream:
    def __init__(self, cache_ref, pt_ref, buf_ref, sem_ref, *, n_pages, nch, ppc, rows, nrows, look):
        self.refs = (cache_ref, pt_ref, buf_ref, sem_ref)
        self.n_pages, self.nch, self.ppc, self.rows, self.nrows, self.look = n_pages, nch, ppc, rows, nrows, look
        self.steps = len(rows) * nch
        self.nslot = look + 1
        self.g0 = pl.program_id(0) * self.steps
        self.total = pl.num_programs(0) * self.steps

    def _copies(self, g):
        cache_ref, pt_ref, buf_ref, sem_ref = self.refs
        f = g % self.steps
        row0 = self.rows[0]
        for p in range(1, len(self.rows)):
            row0 = jnp.where(f // self.nch == p, self.rows[p], row0)
        if len(self.rows) > 1:
            row0 = pl.multiple_of(row0, 8)
        base = (g // self.steps) * self.n_pages + (f % self.nch) * self.ppc
        slot = g % self.nslot
        return [pltpu.make_async_copy(cache_ref.at[pt_ref[base + i], pl.ds(row0, self.nrows), :],
                                      buf_ref.at[slot, :, pl.ds(i * PAGE, PAGE)], sem_ref.at[slot])
                for i in range(self.ppc)]

    def prime(self):
        @pl.when(pl.program_id(0) == 0)
        def _():
            for g in range(self.look):
                for cp in self._copies(g):
                    cp.start()

    def fetch(self, f):
        g = self.g0 + f

        @pl.when(g + self.look < self.total)
        def _():
            for cp in self._copies(g + self.look):
                cp.start()

        for cp in self._copies(g):
            cp.wait()
        return g % self.nslot


def _topk_extract(val, k_top):
    lane = _lane(val.shape)
    sel = jnp.zeros(val.shape, jnp.bool_)
    for _ in range(k_top):
        mx = jnp.max(val, axis=-1, keepdims=True)
        idx = jnp.min(jnp.where(val == mx, lane, val.shape[-1]), axis=-1, keepdims=True)
        pick = lane == idx
        sel = sel | pick
        val = jnp.where(pick, -3e38, val)
    return sel


def _topk_decode_kernel(val_ref, sel_ref, *, k_top):
    val = val_ref[...]
    sel = _topk_extract(val, k_top) & (val > -0.5)
    sel_ref[...] = jnp.where(sel, 0.0, -1.0)


def _topk_decode(val, k_top):
    return pl.pallas_call(
        functools.partial(_topk_decode_kernel, k_top=k_top), grid=(1,),
        in_specs=[pl.BlockSpec(val.shape, lambda i: (0, 0))],
        out_specs=pl.BlockSpec(val.shape, lambda i: (0, 0)),
        out_shape=jax.ShapeDtypeStruct(val.shape, F32),
        compiler_params=_cparams(("arbitrary",)), name="nsa_topk_decode",
    )(val)


def _cmp_decode_kernel(pt_ref, cache_ref, q_ref, amean_ref, pair_ref, o_ref, val_ref, buf_ref, sem_ref, kcs_ref,
                       *, n_pages, nch, ppc, past):
    stream = _PageStream(cache_ref, pt_ref, buf_ref, sem_ref, n_pages=n_pages, nch=nch, ppc=ppc,
                         rows=(0,), nrows=256, look=1)
    stream.prime()

    def chunk(c, carry):
        slot = stream.fetch(c)
        kcs_ref[c] = _dot2(buf_ref[slot], amean_ref[...])
        return carry

    lax.fori_loop(0, nch, chunk, 0)
    nb = n_pages * (PAGE // CMP_BLOCK)
    nsd = nb // 2
    bpc = ppc * (PAGE // CMP_BLOCK)
    means = jnp.concatenate([kcs_ref[cc][:, 0:bpc] for cc in range(nch)], axis=1).astype(BF16)
    for g in range(2):
        s = _dot(q_ref[0, g], means[0:LANES])
        nblk = _lane(s.shape)
        qpos = past + _row(s.shape) % 8
        p = _masked_softmax(s, (nblk + 1) * CMP_BLOCK - 1 <= qpos, -1)
        o_ref[0, g] = _dot_nt(p.astype(BF16), means[LANES:2 * LANES])
        imp = p[0:8]
        for r in range(1, NSA_REP):
            imp = imp + p[r * 8:(r + 1) * 8]
        imp = jnp.concatenate([_dot3(imp, pair_ref[...]), jnp.zeros((8, LANES), F32)], axis=1)
        blk = _lane(imp.shape)
        cur = (past + _row(imp.shape)) // SEL_BLOCK
        forced = (blk == 0) | (blk == cur) | (blk == cur - 1)
        val = jnp.where(blk <= cur, imp + jnp.where(forced, FORCE_SCORE, 0.0), -1.0)
        val_ref[0, g] = jnp.where(blk <= nsd, val, -3e38)


def _cmp_decode(pt_flat, cache, qsw, *, batch, n_pages, past):
    ppc = min(32, n_pages)
    nch = n_pages // ppc
    bpc = ppc * (PAGE // CMP_BLOCK)
    assert bpc % LANES == 0 or nch == 1
    nb = n_pages * (PAGE // CMP_BLOCK)
    nsd = nb // 2
    tok = np.arange(ppc * PAGE)
    amean = np.zeros((ppc * PAGE, LANES), np.float32)
    amean[tok, tok // CMP_BLOCK] = 1.0 / CMP_BLOCK
    pair = (np.arange(nb)[:, None] // 2 == np.arange(nsd)[None, :]).astype(np.float32)
    consts = [jnp.asarray(amean, BF16), jnp.asarray(pair, BF16)]
    grid_spec = pltpu.PrefetchScalarGridSpec(
        num_scalar_prefetch=1, grid=(batch,),
        in_specs=[pl.BlockSpec(memory_space=pl.ANY),
                  pl.BlockSpec((1, 2, 32, LANES), lambda b, pt: (b, 0, 0, 0))]
                 + [pl.BlockSpec(a.shape, lambda b, pt: (0, 0)) for a in consts],
        out_specs=[pl.BlockSpec((1, 2, 32, LANES), lambda b, pt: (b, 0, 0, 0)),
                   pl.BlockSpec((1, 2, 8, nsd + LANES), lambda b, pt: (b, 0, 0, 0))],
        scratch_shapes=[pltpu.VMEM((2, 256, ppc * PAGE), F32), pltpu.SemaphoreType.DMA((2,)),
                        pltpu.VMEM((nch, 256, LANES), F32)])
    return pl.pallas_call(
        functools.partial(_cmp_decode_kernel, n_pages=n_pages, nch=nch, ppc=ppc, past=past),
        grid_spec=grid_spec,
        out_shape=[jax.ShapeDtypeStruct((batch, 2, 32, LANES), F32),
                   jax.ShapeDtypeStruct((batch, 2, 8, nsd + LANES), F32)],
        compiler_params=_cparams(("arbitrary",)), name="nsa_cmp_decode",
    )(pt_flat, cache, qsw, *consts)


ATTN_LOOK = 2


def _attn_decode_kernel(pt_ref, cache_ref, q_ref, bias_ref, bnew_ref, aux_ref, cq_ref, new_ref, o_ref,
                        buf_ref, sem_ref, s_ref, *, n_pages, nch, ppc, krow, vrow, kw, ncols, fox):
    ng = LANES // ncols
    tk = ppc * PAGE
    rows_s = (nch // ng) * tk
    stream = _PageStream(cache_ref, pt_ref, buf_ref, sem_ref, n_pages=n_pages, nch=nch, ppc=ppc,
                         rows=(krow, vrow), nrows=kw, look=ATTN_LOOK)
    stream.prime()
    s_ref[0:rows_s, :] = jnp.zeros((rows_s, LANES), F32)

    def scores(kt, k, bias_rows):
        s = _dot_tn(kt, q_ref[0, k])
        if fox:
            pieces = jnp.concatenate(list(_split3(bias_rows)) + [jnp.zeros(bias_rows.shape, BF16)], axis=0)
            s = s + _dot_tn(pieces, aux_ref[k]) + cq_ref[0, k, 0:1, :]
        return s

    def key_chunk(c, carry):
        slot = stream.fetch(c)
        k = c % ng
        kt = buf_ref[slot].astype(BF16)
        if fox:
            s = scores(kt, k, bias_ref[0, c])
        else:
            bpc = tk // SEL_BLOCK
            m = bias_ref[0, k, pl.ds(pl.multiple_of(c * bpc, 8), bpc), :]
            m = jnp.broadcast_to(m[:, None, :], (bpc, SEL_BLOCK, LANES)).reshape(tk, LANES)
            s = jnp.where(m > -0.5, scores(kt, k, None), NEG_INF)
        rows = pl.ds(pl.multiple_of((c // ng) * tk, tk), tk)
        s_ref[rows, :] += s
        return carry

    lax.fori_loop(0, nch, key_chunk, 0)

    kn = new_ref[0, 0:kw, :].astype(BF16)
    if fox:
        sn = scores(kn, 0, bnew_ref[0])
        tq = _lane(sn.shape) // 8
    else:
        sn = jnp.where(bnew_ref[0, 0:1, :] > -0.5, scores(kn, 0, None), NEG_INF)
        tq = _lane(sn.shape) % 8
    sn = jnp.where((_row(sn.shape) <= tq) & (_lane(sn.shape) < ncols), sn, NEG_INF)
    s_ref[rows_s:rows_s + PAGE, :] = sn

    def all_groups(v, op):
        v8 = jnp.broadcast_to(v, (8, LANES))
        out = v8
        for kk in range(1, ng):
            out = op(out, pltpu.roll(v8, kk * ncols, 1))
        return out[0:1]

    s = s_ref[...]
    ok = s > 0.5 * NEG_INF
    mx = all_groups(jnp.max(s, axis=0, keepdims=True), jnp.maximum)
    e = jnp.where(ok, jnp.exp(s - mx), 0.0)
    den = all_groups(jnp.sum(e, axis=0, keepdims=True), jnp.add)
    s_ref[...] = e * (1.0 / jnp.maximum(den, 1e-30))

    def value_chunk(c, acc):
        slot = stream.fetch(nch + c)
        p = s_ref[pl.ds(pl.multiple_of((c // ng) * tk, tk), tk), :]
        p = jnp.where(_lane(p.shape) // ncols == c % ng, p, 0.0).astype(BF16)
        return acc + _dot(buf_ref[slot].astype(BF16), p)

    acc = lax.fori_loop(0, nch, value_chunk, jnp.zeros((kw, LANES), F32))
    acc = acc + _dot(new_ref[0, kw:2 * kw, :].astype(BF16), s_ref[rows_s:rows_s + PAGE, :].astype(BF16))
    out = acc
    for kk in range(1, ng):
        out = out + pltpu.roll(acc, kk * ncols, 1)
    o_ref[0] = out


def _attn_decode(pt_flat, cache, q, bias, bnew, aux, cq, new_rows, *, batch, n_pages, krow, vrow, kw, ncols, fox,
                 name):
    ng = LANES // ncols
    ppc = min(32, n_pages // ng)
    nch = n_pages // ppc
    assert nch % ng == 0 and n_pages % ppc == 0
    tk = ppc * PAGE
    blk = lambda a: pl.BlockSpec((1,) + a.shape[1:], lambda b, pt: (b,) + (0,) * (a.ndim - 1))
    grid_spec = pltpu.PrefetchScalarGridSpec(
        num_scalar_prefetch=1, grid=(batch,),
        in_specs=[pl.BlockSpec(memory_space=pl.ANY), blk(q), blk(bias), blk(bnew),
                  pl.BlockSpec(aux.shape, lambda b, pt: (0, 0, 0)), blk(cq), blk(new_rows)],
        out_specs=pl.BlockSpec((1, kw, LANES), lambda b, pt: (b, 0, 0)),
        scratch_shapes=[pltpu.VMEM((ATTN_LOOK + 1, kw, tk), F32), pltpu.SemaphoreType.DMA((ATTN_LOOK + 1,)),
                        pltpu.VMEM(((nch // ng) * tk + PAGE, LANES), F32)])
    return pl.pallas_call(
        functools.partial(_attn_decode_kernel, n_pages=n_pages, nch=nch, ppc=ppc, krow=krow, vrow=vrow,
                          kw=kw, ncols=ncols, fox=fox),
        grid_spec=grid_spec, out_shape=jax.ShapeDtypeStruct((batch, kw, LANES), F32),
        compiler_params=_cparams(("arbitrary",)), name=name,
    )(pt_flat, cache, q, bias, bnew, aux, cq, new_rows)


def _win_decode_kernel(q_ref, st_ref, new_ref, o_ref, ns_ref, *, tdec):
    st32 = st_ref[0]
    nw32 = new_ref[0]
    lw = st32.shape[1]
    shifted = pltpu.roll(st32, lw - tdec, 1)
    tail = jnp.where(_lane(nw32.shape) >= PAGE - tdec, pltpu.roll(nw32, PAGE - tdec, 1), shifted[:, lw - PAGE:])
    ns_ref[0] = jnp.concatenate([shifted[:, 0:lw - PAGE], tail], axis=1)
    st = st32.astype(BF16)
    nw = nw32.astype(BF16)
    for g in range(2):
        q = q_ref[0, g]
        s = jnp.concatenate([_dot(q, st[0:LANES]), _dot(q, nw[0:LANES])], axis=1)
        i = _lane(s.shape)
        t = _row(s.shape) % 8
        p = _masked_softmax(s, (i > t + lw - WINDOW) & (i <= t + lw), -1).astype(BF16)
        o_ref[0, g] = _dot_nt(p[:, 0:lw], st[LANES:2 * LANES]) + _dot_nt(p[:, lw:], nw[LANES:2 * LANES])


def _win_decode(qsw, state_t, new_t, batch, tdec):
    lw = state_t.shape[2]
    return pl.pallas_call(
        functools.partial(_win_decode_kernel, tdec=tdec), grid=(batch,),
        in_specs=[pl.BlockSpec((1, 2, 32, LANES), lambda b: (b, 0, 0, 0)),
                  pl.BlockSpec((1, 256, lw), lambda b: (b, 0, 0)),
                  pl.BlockSpec((1, 256, PAGE), lambda b: (b, 0, 0))],
        out_specs=[pl.BlockSpec((1, 2, 32, LANES), lambda b: (b, 0, 0, 0)),
                   pl.BlockSpec((1, 256, lw), lambda b: (b, 0, 0))],
        out_shape=[jax.ShapeDtypeStruct((batch, 2, 32, LANES), F32), jax.ShapeDtypeStruct((batch, 256, lw), F32)],
        compiler_params=_cparams(("arbitrary",)), name="nsa_win_decode",
    )(qsw, state_t, new_t)


def _logf_decode_kernel(pt_ref, cache_ref, new_ref, u_ref, s_ref, perm_ref, tri_ref, c_ref, buf_ref, sem_ref,
                        *, n_pages):
    b = pl.program_id(0)
    nb = pl.num_programs(0)

    def copies(bb, slot):
        return [pltpu.make_async_copy(cache_ref.at[pl.ds(pt_ref[bb * n_pages + i] * FOX_HEADS, FOX_HEADS), :],
                                      buf_ref.at[slot, pl.ds(i * FOX_HEADS, FOX_HEADS), :], sem_ref.at[slot])
                for i in range(n_pages)]

    slot = b % 2

    @pl.when(b == 0)
    def _():
        for cp in copies(b, slot):
            cp.start()

    @pl.when(b + 1 < nb)
    def _():
        for cp in copies(b + 1, 1 - slot):
            cp.start()

    for cp in copies(b, slot):
        cp.wait()
    lp = jnp.concatenate([buf_ref[slot], new_ref[0]], axis=0)
    cs = _dot3(lp, u_ref[...])
    tot = _dot3(cs, s_ref[...])
    c_ref[0] = _dot3_left(perm_ref[...], cs) + _dot3_left(tri_ref[...], tot)


def _logf_decode(pt_flat, cache2d, new_rows, *, batch, n_pages):
    nr = (n_pages + 2) * FOX_HEADS
    r = np.arange(PAGE)
    u = (r[:, None] <= r[None, :]).astype(np.float32)
    last = np.zeros((PAGE, PAGE), np.float32)
    last[PAGE - 1, :] = 1.0
    rr = np.arange(nr)
    tri = ((rr[:, None] % FOX_HEADS == rr[None, :] % FOX_HEADS)
           & (rr[None, :] // FOX_HEADS < rr[:, None] // FOX_HEADS)).astype(np.float32)
    perm = np.zeros((nr, nr), np.float32)
    perm[(rr % FOX_HEADS) * (n_pages + 2) + rr // FOX_HEADS, rr] = 1.0
    consts = [jnp.asarray(u, BF16), jnp.asarray(last, BF16), jnp.asarray(perm, BF16), jnp.asarray(perm @ tri, BF16)]
    grid_spec = pltpu.PrefetchScalarGridSpec(
        num_scalar_prefetch=1, grid=(batch,),
        in_specs=[pl.BlockSpec(memory_space=pl.ANY),
                  pl.BlockSpec((1, 2 * FOX_HEADS, PAGE), lambda b, pt: (b, 0, 0))]
                 + [pl.BlockSpec(c.shape, lambda b, pt: (0, 0)) for c in consts],
        out_specs=pl.BlockSpec((1, nr, PAGE), lambda b, pt: (b, 0, 0)),
        scratch_shapes=[pltpu.VMEM((2, n_pages * FOX_HEADS, PAGE), F32), pltpu.SemaphoreType.DMA((2,))])
    return pl.pallas_call(
        functools.partial(_logf_decode_kernel, n_pages=n_pages),
        grid_spec=grid_spec, out_shape=jax.ShapeDtypeStruct((batch, nr, PAGE), F32),
        compiler_params=_cparams(("arbitrary",)), name="fox_logf_decode",
    )(pt_flat, cache2d, new_rows, *consts)


def _prep_layer_weights(w_in_t, pool_w, pool_scale, w_out_a, w_out_b, w_out_c, w_o):
    d = w_in_t.shape[1]
    tok = jnp.concatenate([w_in_t[0:256], w_in_t[512:1024], w_in_t[2328:2584]], axis=0)
    gate = jnp.concatenate([w_in_t[256:512], w_in_t[1816:2328], w_in_t[3100:6428], w_in_t[1792:1816],
                            jnp.zeros((LANES - 24, d), F32)], axis=0)
    feat = jnp.concatenate([w_in_t[1024:1792], w_in_t[2584:3100], jnp.zeros((8 - FOX_HEADS, d), F32)],
                           axis=0).astype(BF16)
    gd = pool_w.shape[-1]
    bd = jnp.zeros((POOL_WIDTH, POOL_WIDTH), F32)
    for g in range(pool_w.shape[0]):
        bd = bd.at[g * gd:(g + 1) * gd, g * gd:(g + 1) * gd].set(pool_w[g])
    return dict(w_tok=_wprep(tok), w_gate=_wprep(gate), w_feat=feat, pool_bd=bd.astype(BF16),
                pool_scale=pool_scale.reshape(1, -1),
                w_out_a=w_out_a.astype(BF16), w_out_b=w_out_b.astype(BF16), w_out_c=w_out_c.astype(BF16),
                w_o=w_o.astype(BF16))


def _to_token_last(a_t, lead):
    b, _, t = a_t.shape
    nd = len(lead)
    return a_t.reshape((b,) + lead + (t,)).transpose((0, nd + 1) + tuple(range(1, nd + 1)))


def _layer_prompt(x2d, lw, norm_w, fox_bf, final_norm, batch, seq):
    pos = jnp.arange(seq)
    (uz, q8, nsat, wint, foxt, logft, fq, kcm, ksel, vsel, kwin, vwin, fk, fv) = _project(
        x2d, norm_w, lw["w_tok"], lw["w_feat"], fox_bf, pos, prompt=True, batch=batch, seq=seq)
    d = _pool_prompt(uz, seq)
    bpt = TKV // CMP_BLOCK
    nb = seq // CMP_BLOCK
    means = kcm[..., :bpt].transpose(0, 1, 3, 2).reshape(batch, nb, 256)
    means = jnp.concatenate([means[:, 0::2], means[:, 1::2]], axis=1)
    dup = lambda m: jnp.stack([jnp.concatenate([m[..., g * HEAD_DIM:(g + 1) * HEAD_DIM]] * 2, axis=-1)
                               for g in range(2)], axis=1).astype(BF16)
    ocmp, qaug = _cmp_prompt(dup(means[..., 0:LANES]), dup(means[..., LANES:2 * LANES]), q8, batch, seq)
    osel = _flash(qaug, ksel, vsel, n_sub=2, n_stack=NSA_REP, tpt=LANES, name="nsa_sel_prompt")
    owin = _win_prompt(q8, kwin, vwin, batch, seq)
    tpt = 512
    oc = _flash(fq.reshape(batch, FOX_HEADS, seq // tpt, tpt, LANES), fk, fv,
                n_sub=2, n_stack=1, tpt=tpt, name="fox_prompt")
    n = batch * seq
    xo, y = _output(x2d, norm_w, d, ocmp.reshape(n, 512), osel.reshape(n, 512), owin.reshape(n, 512),
                    oc.reshape(n, 256), lw, final_norm)
    wl = min(WINDOW, seq)
    states = (_to_token_last(nsat, (4, NSA_KV_HEADS, HEAD_DIM)),
              _to_token_last(foxt, (2, FOX_HEADS, HEAD_DIM)),
              logft[:, :FOX_HEADS].transpose(0, 2, 1),
              _to_token_last(wint[:, :, seq - wl:], (2, NSA_KV_HEADS, HEAD_DIM)),
              uz[:, :POOL_WIDTH].reshape(batch, seq, POOL_WIDTH)[:, -POOL_STATE:])
    return xo, y, states


def _new_page(a_t, batch, tdec):
    f = a_t.shape[0]
    a = a_t.reshape(f, batch, tdec).transpose(1, 0, 2)
    return jnp.concatenate([a, jnp.zeros((batch, f, PAGE - tdec), a.dtype)], axis=2)


def _layer_decode(x2d, lw, norm_w, fox_bf, final_norm, pt_flat, nsa_cache, fox_cache, logf_cache,
                  state_win_t, state_pool_tb, batch, tdec, n_pages):
    past = n_pages * PAGE
    n = batch * tdec
    pos = past + jnp.tile(jnp.arange(tdec), batch)
    (uz, q8, nsat, wint, foxt, logft, qc) = _project(
        x2d, norm_w, lw["w_tok"], lw["w_feat"], fox_bf, pos, prompt=False, batch=batch, seq=tdec)
    nsat, wint, foxt, logft = nsat[0], wint[0], foxt[0], logft[0]

    u_tb = uz[:, :POOL_WIDTH].reshape(batch, tdec, POOL_WIDTH).transpose(1, 0, 2)
    pool_seq = jnp.concatenate([state_pool_tb, u_tb], axis=0)
    d = _pool_decode(pool_seq, past, tdec).transpose(1, 0, 2).reshape(n, POOL_WIDTH)

    q5 = q8[0].reshape(NSA_KV_HEADS, NSA_REP, batch, tdec, LANES)
    qs = q5.transpose(2, 0, 1, 3, 4)
    qs = jnp.concatenate([qs, jnp.zeros((batch, 2, NSA_REP, 8 - tdec, LANES), BF16)], axis=3)
    qs = qs.reshape(batch, 2, NSA_REP * 8, LANES)
    qsw = jnp.stack([qs[:, 0], jnp.roll(qs[:, 1], HALF, axis=-1)], axis=1)
    ocmp_k, val = _cmp_decode(pt_flat, nsa_cache, qsw, batch=batch, n_pages=n_pages, past=past)
    selm = _topk_decode(val.reshape(batch * 16, -1), min(SEL_TOPK, past // SEL_BLOCK + 1)).reshape(val.shape)

    def unstack_groups(o):
        o = o.reshape(batch, 2, NSA_REP, 8, 2, HEAD_DIM)[:, :, :, :tdec]
        o = jnp.stack([o[:, 0, :, :, 0], o[:, 1, :, :, 1]], axis=1)
        return o.transpose(0, 3, 1, 2, 4).reshape(n, NSA_HEADS * HEAD_DIM)

    ocmp = unstack_groups(ocmp_k)
    owin_k, win_next_t = _win_decode(qsw, state_win_t, _new_page(wint, batch, tdec), batch, tdec)
    owin = unstack_groups(owin_k)

    nsd = past // SEL_BLOCK
    q6 = q5[..., :HEAD_DIM].transpose(2, 1, 0, 3, 4)
    q6 = jnp.concatenate([q6, jnp.zeros((batch, NSA_REP, 2, 8 - tdec, HEAD_DIM), BF16)], axis=3)
    eye2 = jnp.eye(2, dtype=BF16)
    qbd = q6[:, :, :, :, None, :] * eye2[None, None, :, None, :, None]
    qbd_t = qbd.reshape(batch, 64, LANES).transpose(0, 2, 1)
    qbd_t = jnp.concatenate([qbd_t, jnp.zeros((batch, LANES, LANES - 64), BF16)], axis=2)
    sel_t = jnp.tile(selm.reshape(batch, 16, nsd + LANES).transpose(0, 2, 1), (1, 1, NSA_REP))
    sel_t = jnp.concatenate([sel_t, jnp.zeros((batch, nsd + LANES, LANES - 64), F32)], axis=2)
    groups = lambda a, ng: jnp.stack([jnp.roll(a, k * (LANES // ng), axis=-1) for k in range(ng)], axis=1)
    osel_k = _attn_decode(pt_flat, nsa_cache, groups(qbd_t, 2), groups(sel_t[:, :nsd], 2), sel_t[:, nsd:nsd + 8],
                          jnp.zeros((2, 32, LANES), BF16), jnp.zeros((batch, 2, 8, LANES), F32),
                          _new_page(nsat[256:512], batch, tdec),
                          batch=batch, n_pages=n_pages, krow=256, vrow=384, kw=LANES, ncols=64, fox=False,
                          name="nsa_sel_decode")
    os_ = osel_k[:, :, :64].reshape(batch, 2, HEAD_DIM, NSA_REP, 2, 8)[..., :tdec]
    osel = jnp.stack([os_[:, 0, :, :, 0], os_[:, 1, :, :, 1]], axis=1)
    osel = osel.transpose(0, 4, 1, 3, 2).reshape(n, 512)

    lnew = _new_page(logft, batch, tdec)
    lnew = jnp.concatenate([lnew[:, :FOX_HEADS], jnp.zeros((batch, FOX_HEADS, PAGE), F32)], axis=1)
    csum = _logf_decode(pt_flat, logf_cache, lnew, batch=batch, n_pages=n_pages)
    c4 = csum.reshape(batch, FOX_HEADS, (n_pages + 2) * PAGE)
    ck8 = jnp.concatenate([-c4, jnp.zeros_like(c4)], axis=1)
    cq = jnp.concatenate([c4[:, :, past:past + tdec].transpose(0, 2, 1),
                          jnp.zeros((batch, tdec, 8 - FOX_HEADS), F32)], axis=2).reshape(batch, 1, tdec * 8)
    cq = jnp.concatenate([cq, jnp.zeros((batch, 7, tdec * 8), F32)], axis=1)
    cq = jnp.concatenate([cq, jnp.zeros((batch, 8, LANES - tdec * 8), F32)], axis=2)
    qf = (qc * SCALE).reshape(batch, tdec, FOX_HEADS, HEAD_DIM)
    eye = jnp.eye(8, FOX_HEADS, dtype=F32)
    qfbd = (qf[:, :, None, :, :] * eye[None, None, :, :, None]).reshape(batch, tdec * 8, FOX_HEADS * HEAD_DIM)
    qf_t = jnp.concatenate([qfbd.transpose(0, 2, 1), jnp.zeros((batch, 256, LANES - tdec * 8), F32)],
                           axis=2).astype(BF16)
    col = np.arange(LANES)
    head_cols = np.zeros((32, LANES), np.float32)
    for p in range(3):
        head_cols[p * 8 + col[:tdec * 8] % 8, col[:tdec * 8]] = 1.0
    fng = LANES // (tdec * 8)
    fppc = min(32, n_pages // fng)
    ck_chunks = ck8[:, :, :past].reshape(batch, 8, n_pages // fppc, fppc * PAGE).transpose(0, 2, 1, 3)
    oc_k = _attn_decode(pt_flat, fox_cache, groups(qf_t, fng), ck_chunks, ck8[:, :, past:past + PAGE],
                        groups(jnp.asarray(head_cols, BF16)[None], fng)[0], groups(cq, fng),
                        _new_page(foxt, batch, tdec),
                        batch=batch, n_pages=n_pages, krow=0, vrow=256, kw=256, ncols=tdec * 8, fox=True,
                        name="fox_decode")
    o6 = oc_k[:, :, :tdec * 8].reshape(batch, FOX_HEADS, HEAD_DIM, tdec, 8)
    oc = jnp.stack([o6[:, h, :, :, h] for h in range(FOX_HEADS)], axis=1)
    oc = oc.transpose(0, 3, 1, 2).reshape(n, 256)

    xo, y = _output(x2d, norm_w, d, ocmp, osel, owin, oc, lw, final_norm)
    states = (nsat.T.reshape(batch, tdec, 4, NSA_KV_HEADS, HEAD_DIM),
              foxt.T.reshape(batch, tdec, 2, FOX_HEADS, HEAD_DIM),
              logft[:FOX_HEADS].T.reshape(batch, tdec, FOX_HEADS),
              _to_token_last(win_next_t, (2, NSA_KV_HEADS, HEAD_DIM)),
              pool_seq[-POOL_STATE:].transpose(1, 0, 2))
    return xo, y, states


def kernel(x_prompt, x_sample, cache_nsa_kv, cache_fox_kv, cache_fox_logf, state_win_kv, state_pool, page_table,
           norm_w, w_in, pool_w, pool_scale, fox_bf, w_out_a, w_out_b, w_out_c, w_o, final_norm):
    depth = w_in.shape[0]
    batch, seq, dm = x_prompt.shape
    dbatch, tdec, _ = x_sample.shape
    n_phys, page = cache_nsa_kv.shape[1], cache_nsa_kv.shape[2]
    n_pages = page_table.shape[1]
    assert page == PAGE and tdec == 4 and n_pages % min(32, n_pages) == 0
    assert state_win_kv.shape[2] == WINDOW and n_pages * PAGE >= WINDOW

    nsa_cache = cache_nsa_kv.transpose(0, 1, 3, 4, 5, 2).reshape(depth * n_phys, 512, PAGE)
    fox_cache = cache_fox_kv.transpose(0, 1, 3, 4, 5, 2).reshape(depth * n_phys, 512, PAGE)
    logf_cache = cache_fox_logf.transpose(0, 1, 3, 2).reshape(depth * n_phys * FOX_HEADS, PAGE)
    win_state_t = state_win_kv.transpose(0, 1, 3, 4, 5, 2).reshape(depth, dbatch, 256, WINDOW)
    pool_state_tb = state_pool.transpose(0, 2, 1, 3)
    w_in_t = w_in.transpose(0, 2, 1)

    hp = x_prompt.reshape(batch * seq, dm)
    hs = x_sample.reshape(dbatch * tdec, dm)
    st_p, st_s = [], []
    for layer in range(depth):
        lw = _prep_layer_weights(w_in_t[layer], pool_w[layer], pool_scale[layer], w_out_a[layer],
                                 w_out_b[layer], w_out_c[layer], w_o[layer])
        hp, yp, sp = _layer_prompt(hp, lw, norm_w[layer], fox_bf[layer], final_norm, batch, seq)
        pt_flat = (page_table + layer * n_phys).reshape(-1).astype(jnp.int32)
        hs, ys, ss = _layer_decode(hs, lw, norm_w[layer], fox_bf[layer], final_norm, pt_flat, nsa_cache, fox_cache,
                                   logf_cache, win_state_t[layer], pool_state_tb[layer], dbatch, tdec, n_pages)
        st_p.append(sp)
        st_s.append(ss)
    outs = [yp.reshape(batch, seq, dm), ys.reshape(dbatch, tdec, dm)]
    for k in range(5):
        outs.append(jnp.stack([s[k] for s in st_p], 0))
        outs.append(jnp.stack([s[k] for s in st_s], 0))
    return tuple(outs)
```

```python
import functools

import numpy as np
import jax
import jax.numpy as jnp
from jax import lax
from jax.experimental import pallas as pl
from jax.experimental.pallas import tpu as pltpu

F32 = jnp.float32
BF16 = jnp.bfloat16

HEAD_DIM = 64
POOL_WIDTH = 256
POOL_WINDOWS = (2, 4, 8, 16)
POOL_STATE = 15
NSA_HEADS = 8
NSA_KV_HEADS = 2
NSA_REP = NSA_HEADS // NSA_KV_HEADS
CMP_BLOCK = 32
SEL_BLOCK = 64
SEL_TOPK = 16
WINDOW = 512
FORCE_SCORE = 1e4
FOX_HEADS = 4
ROPE_THETA = 10000.0
RMS_EPS = 1e-6
NEG_INF = -1e30
SCALE = HEAD_DIM ** -0.5

LANES = 128
HALF = LANES // 2
PAGE = 128
VMEM_LIMIT = 56 * 1024 * 1024

C_U, C_QB, C_QC, C_END = 0, 256, 768, 1024
G_ZA, G_ZB, G_ZC, G_GM, G_SM, G_END = 0, 256, 768, 1024, 4096, 4224
R_KV, R_KC, R_VC, R_FC, R_END = 0, 768, 1024, 1280, 1288
SM_GB = 0
TKV = 256
FT = 1024
ATTN_LOOK = 2
SEL_TPT = 128


def _cparams(sem):
    return pltpu.CompilerParams(dimension_semantics=sem, vmem_limit_bytes=VMEM_LIMIT)


def _lane(shape):
    return lax.broadcasted_iota(jnp.int32, shape, len(shape) - 1)


def _row(shape):
    return lax.broadcasted_iota(jnp.int32, shape, len(shape) - 2)


def _split3(x):
    hi = x.astype(BF16)
    r1 = x - hi.astype(F32)
    mid = r1.astype(BF16)
    lo = (r1 - mid.astype(F32)).astype(BF16)
    return hi, mid, lo


def _dot(a, b):
    return jnp.dot(a, b, preferred_element_type=F32)


def _dot_nt(a, b):
    return lax.dot_general(a, b, (((1,), (1,)), ((), ())), preferred_element_type=F32)


def _dot_tn(a, b):
    return lax.dot_general(a, b, (((0,), (0,)), ((), ())), preferred_element_type=F32)


def _dot3(x, m):
    n = x.shape[0]
    r = _dot(jnp.concatenate(_split3(x), axis=0), m)
    return r[0:n] + r[n:2 * n] + r[2 * n:3 * n]


def _dot2(x, m):
    n = x.shape[0]
    hi = x.astype(BF16)
    lo = (x - hi.astype(F32)).astype(BF16)
    r = _dot(jnp.concatenate([hi, lo], axis=0), m)
    return r[0:n] + r[n:2 * n]


def _dot3_left(m, x):
    n = x.shape[1]
    r = _dot(m, jnp.concatenate(_split3(x), axis=1))
    return r[:, 0:n] + r[:, n:2 * n] + r[:, 2 * n:3 * n]


def _swap_halves(x):
    return pltpu.roll(x, HALF, x.ndim - 1)


def _to_low_half(x, g, fill):
    lo = _lane(x.shape) < HALF
    return jnp.where(lo, x if g == 0 else _swap_halves(x), fill)


def _pair_heads(pieces):
    out = []
    for j in range(0, len(pieces), 2):
        lo = _lane(pieces[j].shape) < HALF
        out.append(jnp.where(lo, pieces[j], pieces[j + 1]))
    return out


def _masked_softmax(s, ok, axis):
    s = jnp.where(ok, s, NEG_INF)
    m = jnp.max(s, axis=axis, keepdims=True)
    e = jnp.where(ok, jnp.exp(s - m), 0.0)
    return e * (1.0 / jnp.maximum(jnp.sum(e, axis=axis, keepdims=True), 1e-30))


def _wprep_kernel(wt_ref, w_ref):
    w_ref[...] = wt_ref[...].T.astype(BF16)


def _wprep(wt):
    cols, d = wt.shape
    return pl.pallas_call(
        _wprep_kernel, grid=(cols // LANES,),
        in_specs=[pl.BlockSpec((LANES, d), lambda i: (i, 0))],
        out_specs=pl.BlockSpec((d, LANES), lambda i: (0, i)),
        out_shape=jax.ShapeDtypeStruct((d, cols), BF16),
        compiler_params=_cparams(("arbitrary",)), name="w_in_transpose",
    )(wt)


def _rope_tile(x, cosf, sinf):
    lane = _lane(x.shape)
    first = (lane % HEAD_DIM) < (HEAD_DIM // 2)
    rot = jnp.where(first, pltpu.roll(x, LANES - HEAD_DIM // 2, 1), pltpu.roll(x, HEAD_DIM // 2, 1))
    return x * cosf + rot * sinf


def _rope_rows(x, cos_t, sin_t):
    half = HEAD_DIM // 2
    out = []
    for h in range(x.shape[0] // HEAD_DIM):
        x1 = x[h * HEAD_DIM:h * HEAD_DIM + half]
        x2 = x[h * HEAD_DIM + half:(h + 1) * HEAD_DIM]
        out += [x1 * cos_t - x2 * sin_t, x2 * cos_t + x1 * sin_t]
    return jnp.concatenate(out, axis=0)


def _proj_kernel(*refs, prompt, tiles_per_seq):
    if prompt:
        (x_ref, nw_ref, wt_ref, wk_ref, cos_ref, sin_ref, cost_ref, sint_ref, bfc_ref,
         ohb_ref, amean_ref, triu_ref, sq_ref,
         u_ref, q8_ref, nsat_ref, wint_ref, foxt_ref, logft_ref,
         fq_ref, kcm_ref, ksel_ref, vsel_ref, kwin_ref, vwin_ref, fk_ref, fv_ref, carry_ref) = refs
    else:
        (x_ref, nw_ref, wt_ref, wk_ref, cos_ref, sin_ref, cost_ref, sint_ref, bfc_ref,
         u_ref, q8_ref, nsat_ref, wint_ref, foxt_ref, logft_ref, qc_ref) = refs

    x = x_ref[...]
    ms = jnp.mean(x * x, axis=-1, keepdims=True)
    hb = ((x * lax.rsqrt(ms + RMS_EPS)) * nw_ref[...]).astype(BF16)
    tm = x.shape[0]

    def seg(c0, n):
        return _dot(hb, wt_ref[:, c0:c0 + n])

    def seg_t(r0, n):
        return _dot_nt(wk_ref[r0:r0 + n, :], hb)

    u_ref[...] = seg(C_U, POOL_WIDTH)
    cosf = cos_ref[...]
    sinf = sin_ref[...]
    for c in range(4):
        y = _rope_tile(seg(C_QB + c * LANES, LANES), cosf, sinf) * SCALE
        q8_ref[0, 2 * c] = _to_low_half(y, 0, 0.0).astype(BF16)
        q8_ref[0, 2 * c + 1] = _to_low_half(y, 1, 0.0).astype(BF16)
    qc = seg(C_QC, 256)

    cos_t = cost_ref[...]
    sin_t = sint_ref[...]
    kcmp = _rope_rows(seg_t(R_KV, LANES), cos_t, sin_t)
    vcmp = seg_t(R_KV + 128, LANES)
    ksel = _rope_rows(seg_t(R_KV + 256, LANES), cos_t, sin_t)
    vsel = seg_t(R_KV + 384, LANES)
    kwin = _rope_rows(seg_t(R_KV + 512, LANES), cos_t, sin_t)
    vwin = seg_t(R_KV + 640, LANES)
    nsat = nsat_ref.at[0]
    foxt = foxt_ref.at[0]
    nsat[0:128] = kcmp
    nsat[128:256] = vcmp
    nsat[256:384] = ksel
    nsat[384:512] = vsel
    wint_ref[0, 0:128] = kwin
    wint_ref[0, 128:256] = vwin
    kc = seg_t(R_KC, 256)
    vc = seg_t(R_VC, 256)
    foxt[0:256] = kc
    foxt[256:512] = vc
    f_t = seg_t(R_FC, 8)
    logf = jnp.where(_row(f_t.shape) < FOX_HEADS, jax.nn.log_sigmoid(f_t + bfc_ref[...]), 0.0)
    logft_ref[0] = logf

    if not prompt:
        qc_ref[...] = qc
        return

    kcm_ref[0, 0] = _dot3(jnp.concatenate([kcmp, vcmp], axis=0), amean_ref[...])
    ohb = ohb_ref[...]
    for g in range(2):
        gs = slice(g * HEAD_DIM, (g + 1) * HEAD_DIM)
        ksel_ref[0, g, 0] = jnp.concatenate([ksel[gs].astype(BF16), ohb], axis=0)
        vsel_ref[0, g, 0] = jnp.concatenate([vsel[gs]] * 2, axis=0).astype(BF16)
        kwin_ref[0, g, 0] = jnp.concatenate([kwin[gs]] * 2, axis=0).astype(BF16)
        vwin_ref[0, g, 0] = jnp.concatenate([vwin[gs]] * 2, axis=0).astype(BF16)

    i = pl.program_id(0)
    carry = jnp.where(i % tiles_per_seq == 0, 0.0, carry_ref[:, 0:1])
    csum = _dot3(logf, triu_ref[...]) + carry
    carry_ref[...] = jnp.broadcast_to(csum[:, tm - 1:tm], carry_ref.shape)
    c_hi, c_mid, c_lo = [p.astype(F32) for p in _split3(csum)]
    r8 = _row((8, tm))
    c_tok = jnp.concatenate([csum, jnp.zeros((LANES - 8, tm), F32)], axis=0).T
    cq = _dot(jnp.concatenate(_split3(c_tok), axis=1), sq_ref[...])
    lane = _lane((tm, LANES))
    one_q = ((lane >= HALF + 3) & (lane < HALF + 6)).astype(F32)
    for h in range(FOX_HEADS):
        t, g = h // 2, h % 2
        hs = slice(h * HEAD_DIM, (h + 1) * HEAD_DIM)
        aug = jnp.where(r8 < 3, 1.0, jnp.where(r8 == 3, -c_hi[h:h + 1], jnp.where(
            r8 == 4, -c_mid[h:h + 1], jnp.where(r8 == 5, -c_lo[h:h + 1], 0.0))))
        fk_ref[0, h, 0] = jnp.concatenate([kc[hs], aug, jnp.zeros((HALF - 8, tm), F32)], axis=0).astype(BF16)
        fv_ref[0, h, 0] = jnp.concatenate([vc[hs]] * 2, axis=0).astype(BF16)
        fq_ref[0, h] = _to_low_half(qc[:, t * LANES:(t + 1) * LANES] * SCALE, g,
                                    cq[:, h * LANES:(h + 1) * LANES] + one_q).astype(BF16)


def _rope_tables(pos):
    half = HEAD_DIM // 2
    inv = ROPE_THETA ** (-jnp.arange(half, dtype=F32) / half)
    ang = pos.astype(F32)[:, None] * inv[None, :]
    cos, sin = jnp.cos(ang), jnp.sin(ang)
    return jnp.tile(cos, (1, 4)), jnp.tile(jnp.concatenate([-sin, sin], 1), (1, 2)), cos.T, sin.T


def _fox_query_aug_matrix():
    sq = np.zeros((3 * LANES, FOX_HEADS * LANES), np.float32)
    for h in range(FOX_HEADS):
        for p in range(3):
            sq[p * LANES + h, h * LANES + HALF + p] = 1.0
    return jnp.asarray(sq, BF16)


def _project(x2d, norm_w, w_tok, w_feat, fox_bf, pos, *, prompt, batch, seq):
    n, d = x2d.shape
    tm = TKV if prompt else n
    tiles_per_seq = seq // tm if prompt else 1
    cosf, sinf, cos_t, sin_t = _rope_tables(pos)
    bf_col = jnp.zeros((8, 1), F32).at[0:FOX_HEADS, 0].set(fox_bf)
    row = lambda w: pl.BlockSpec((tm, w), lambda i: (i, 0))
    full = lambda a: pl.BlockSpec(a.shape, lambda i: (0,) * a.ndim)
    if prompt:
        tab = pl.BlockSpec((tm, LANES), lambda i: (i % tiles_per_seq, 0))
        tab_t = pl.BlockSpec((HEAD_DIM // 2, tm), lambda i: (0, i % tiles_per_seq))
    else:
        tab = row(LANES)
        tab_t = pl.BlockSpec((HEAD_DIM // 2, tm), lambda i: (0, i))
    in_arrays = [x2d, norm_w.reshape(1, d), w_tok, w_feat, cosf, sinf, cos_t, sin_t, bf_col]
    in_specs = [row(d), full(in_arrays[1]), full(w_tok), full(w_feat), tab, tab, tab_t, tab_t, full(bf_col)]
    nb = batch if prompt else 1
    bidx = lambda i: (i // tiles_per_seq, 0, i % tiles_per_seq)
    feat = lambda r: pl.BlockSpec((1, r, tm), bidx)
    hb = lambda nh: pl.BlockSpec((1, nh, tm, LANES), lambda i: (i // tiles_per_seq, 0, i % tiles_per_seq, 0))
    nt = (seq if prompt else n)
    out_shapes = [
        jax.ShapeDtypeStruct((n, POOL_WIDTH), F32), jax.ShapeDtypeStruct((nb, NSA_HEADS, nt, LANES), BF16),
        jax.ShapeDtypeStruct((nb, 512, nt), F32), jax.ShapeDtypeStruct((nb, 256, nt), F32),
        jax.ShapeDtypeStruct((nb, 512, nt), F32), jax.ShapeDtypeStruct((nb, 8, nt), F32),
    ]
    out_specs = [row(POOL_WIDTH), hb(NSA_HEADS), feat(512), feat(256), feat(512), feat(8)]
    scratch = []
    if prompt:
        t = np.arange(seq)
        ohb = np.zeros((HALF, seq), np.float32)
        ohb[t // SEL_BLOCK, t] = -NEG_INF
        tt = np.arange(tm)
        amean = np.zeros((tm, LANES), np.float32)
        amean[tt, tt // CMP_BLOCK] = 1.0 / CMP_BLOCK
        triu = np.triu(np.ones((tm, tm), np.float32))
        extra = [jnp.asarray(ohb, BF16), jnp.asarray(amean, BF16), jnp.asarray(triu, BF16), _fox_query_aug_matrix()]
        in_arrays += extra
        in_specs += [pl.BlockSpec((HALF, tm), lambda i: (0, i % tiles_per_seq)),
                     full(extra[1]), full(extra[2]), full(extra[3])]
        nkt = seq // tm
        kvt = lambda ns: pl.BlockSpec((1, ns, 1, LANES, tm),
                                      lambda i: (i // tiles_per_seq, 0, i % tiles_per_seq, 0, 0))
        out_shapes.append(jax.ShapeDtypeStruct((batch, FOX_HEADS, seq, LANES), BF16))
        out_specs.append(hb(FOX_HEADS))
        out_shapes.append(jax.ShapeDtypeStruct((batch, nkt, 256, LANES), F32))
        out_specs.append(pl.BlockSpec((1, 1, 256, LANES), lambda i: (i // tiles_per_seq, i % tiles_per_seq, 0, 0)))
        tpf = FT // tm
        kvf = lambda ns: pl.BlockSpec(
            (1, ns, 1, LANES, tm),
            lambda i: (i // tiles_per_seq, 0, (i % tiles_per_seq) // tpf, 0, (i % tiles_per_seq) % tpf))
        for ns, flash in ((2, True), (2, True), (2, False), (2, False), (FOX_HEADS, True), (FOX_HEADS, True)):
            if flash:
                out_shapes.append(jax.ShapeDtypeStruct((batch, ns, seq // FT, LANES, FT), BF16))
                out_specs.append(kvf(ns))
            else:
                out_shapes.append(jax.ShapeDtypeStruct((batch, ns, nkt, LANES, tm), BF16))
                out_specs.append(kvt(ns))
        scratch = [pltpu.VMEM((8, LANES), F32)]
    else:
        out_shapes.append(jax.ShapeDtypeStruct((n, 256), F32))
        out_specs.append(row(256))
    return pl.pallas_call(
        functools.partial(_proj_kernel, prompt=prompt, tiles_per_seq=tiles_per_seq),
        grid=(n // tm,), in_specs=in_specs, out_specs=out_specs, out_shape=out_shapes,
        scratch_shapes=scratch, compiler_params=_cparams(("arbitrary",)),
        name="proj_prompt" if prompt else "proj_decode",
    )(*in_arrays)


def _pool_mix(u, prev, pos):
    grp = _lane(u.shape) // (POOL_WIDTH // len(POOL_WINDOWS))
    acc = u
    win = None
    for j in range(1, POOL_WINDOWS[-1]):
        acc = acc + prev(j)
        if j + 1 in POOL_WINDOWS:
            k = POOL_WINDOWS.index(j + 1)
            win = acc if win is None else jnp.where(grp >= k, acc, win)
    w = jnp.zeros(u.shape, jnp.int32)
    for k, wk in enumerate(POOL_WINDOWS):
        w = jnp.where(grp == k, wk, w)
    cnt = jnp.minimum(pos + 1, w).astype(F32)
    return win / cnt - u


def _pool_prompt_kernel(u_ref, halo_ref, d_ref, s_ref, *, tiles_per_seq, tm):
    i = pl.program_id(0) % tiles_per_seq
    s_ref[0:16, :] = jnp.where(i == 0, 0.0, halo_ref[...])
    s_ref[16:, :] = u_ref[...]
    pos = i * tm + _row((tm, POOL_WIDTH))
    d_ref[...] = _pool_mix(u_ref[...], lambda j: s_ref[16 - j:16 - j + tm, :], pos)


def _pool_prompt(uz, seq):
    n = uz.shape[0]
    tm = 512
    tiles_per_seq = seq // tm
    return pl.pallas_call(
        functools.partial(_pool_prompt_kernel, tiles_per_seq=tiles_per_seq, tm=tm),
        grid=(n // tm,),
        in_specs=[pl.BlockSpec((tm, POOL_WIDTH), lambda i: (i, 0)),
                  pl.BlockSpec((16, POOL_WIDTH), lambda i: (jnp.maximum(i * (tm // 16) - 1, 0), 0))],
        out_specs=pl.BlockSpec((tm, POOL_WIDTH), lambda i: (i, 0)),
        out_shape=jax.ShapeDtypeStruct((n, POOL_WIDTH), F32),
        scratch_shapes=[pltpu.VMEM((tm + 16, POOL_WIDTH), F32)],
        compiler_params=_cparams(("arbitrary",)), name="pool_prompt",
    )(uz, uz)


def _pool_decode_kernel(seq_ref, d_ref, *, pos0, tdec):
    for t in range(tdec):
        pos = jnp.full(seq_ref.shape[1:], pos0 + t, jnp.int32)
        d_ref[t] = _pool_mix(seq_ref[POOL_STATE + t], lambda j: seq_ref[POOL_STATE + t - j], pos)


def _pool_decode(seq_tb, pos0, tdec):
    _, b, w = seq_tb.shape
    return pl.pallas_call(
        functools.partial(_pool_decode_kernel, pos0=pos0, tdec=tdec),
        grid=(1,),
        in_specs=[pl.BlockSpec(seq_tb.shape, lambda i: (0, 0, 0))],
        out_specs=pl.BlockSpec((tdec, b, w), lambda i: (0, 0, 0)),
        out_shape=jax.ShapeDtypeStruct((tdec, b, w), F32),
        compiler_params=_cparams(("arbitrary",)), name="pool_decode",
    )(seq_tb)


def _select_rank(imp, blk, cur, k_top):
    forced = (blk == 0) | (blk == cur) | (blk == cur - 1)
    val = jnp.where(blk <= cur, imp + jnp.where(forced, FORCE_SCORE, 0.0), -1.0)
    ns = val.shape[0]
    sel = jnp.zeros(val.shape, jnp.bool_)
    for _ in range(k_top):
        mx = jnp.max(val, axis=0, keepdims=True)
        idx = jnp.min(jnp.where(val == mx, blk, ns), axis=0, keepdims=True)
        pick = blk == idx
        sel = sel | pick
        val = jnp.where(pick, -3e38, val)
    return sel & (blk <= cur)


def _cmp_prompt_kernel(kc_ref, vc_ref, q_ref, o_ref, qaug_ref, *, tq, seq):
    qi = pl.program_id(1)
    ns = seq // SEL_BLOCK
    imps = []
    for g in range(NSA_KV_HEADS):
        qs = jnp.concatenate([q_ref[0, g * NSA_REP + r] for r in range(NSA_REP)], axis=0)
        s = _dot_nt(kc_ref[0, g], qs)
        row = _row(s.shape)
        nblk = jnp.where(row < ns, 2 * row, 2 * (row - ns) + 1)
        t = qi * tq + _lane(s.shape) % tq
        p = _masked_softmax(s, (nblk + 1) * CMP_BLOCK - 1 <= t, 0)
        o = _dot_tn(p.astype(BF16), vc_ref[0, g])
        pieces = _pair_heads([o[r * tq:(r + 1) * tq] for r in range(NSA_REP)])
        for j, piece in enumerate(pieces):
            o_ref[0, :, (2 * g + j) * LANES:(2 * g + j + 1) * LANES] = piece
        imp = p[:, 0:tq]
        for r in range(1, NSA_REP):
            imp = imp + p[:, r * tq:(r + 1) * tq]
        imps.append(imp[:ns] + imp[ns:])

    imp = jnp.concatenate(imps, axis=1)
    blk = _row(imp.shape)
    cur = (qi * tq + _lane(imp.shape) % tq) // SEL_BLOCK
    selm = jnp.where(_select_rank(imp, blk, cur, min(SEL_TOPK, ns)), 0.0, -1.0)
    for g in range(NSA_KV_HEADS):
        pad = [jnp.zeros((HALF, tq), F32), selm[:, g * tq:(g + 1) * tq]]
        if ns < HALF:
            pad.append(jnp.zeros((HALF - ns, tq), F32))
        sel_t = jnp.concatenate(pad, axis=0).T
        for h in range(tq // SEL_TPT):
            tok = slice(h * SEL_TPT, (h + 1) * SEL_TPT)
            for r in range(NSA_REP):
                qaug_ref[0, g, h, r * SEL_TPT:(r + 1) * SEL_TPT, :] = (
                    q_ref[0, g * NSA_REP + r, tok, :].astype(F32) + sel_t[tok]).astype(BF16)


def _cmp_prompt(kc_dup, vc_dup, q8, batch, seq):
    tq = 2 * SEL_TPT
    nq = seq // tq
    nb = seq // CMP_BLOCK
    assert seq % tq == 0 and seq // SEL_BLOCK <= HALF
    kv = pl.BlockSpec((1, NSA_KV_HEADS, nb, LANES), lambda b, q: (b, 0, 0, 0))
    return pl.pallas_call(
        functools.partial(_cmp_prompt_kernel, tq=tq, seq=seq),
        grid=(batch, nq),
        in_specs=[kv, kv, pl.BlockSpec((1, NSA_HEADS, tq, LANES), lambda b, q: (b, 0, q, 0))],
        out_specs=[pl.BlockSpec((1, tq, 512), lambda b, q: (b, q, 0)),
                   pl.BlockSpec((1, NSA_KV_HEADS, tq // SEL_TPT, NSA_REP * SEL_TPT, LANES),
                                lambda b, q: (b, 0, q, 0, 0))],
        out_shape=[jax.ShapeDtypeStruct((batch, seq, 512), F32),
                   jax.ShapeDtypeStruct((batch, NSA_KV_HEADS, seq // SEL_TPT, NSA_REP * SEL_TPT, LANES), BF16)],
        compiler_params=_cparams(("arbitrary", "arbitrary")), name="nsa_cmp_prompt",
    )(kc_dup, vc_dup, q8)


def _flash_kernel(q_ref, k_ref, v_ref, o_ref, *, n_sub, n_stack, tpt):
    qi = pl.program_id(2)
    q0 = qi * tpt
    nfull = q0 // FT
    rows = n_stack * tpt
    qs = [q_ref[0, u, 0] for u in range(n_sub)]

    def step(j, carry, masked):
        ss = [_dot(qs[u], k_ref[0, u, j]) for u in range(n_sub)]
        out = []
        for u in range(n_sub):
            m, l, acc = carry[u]
            s = ss[u]
            if masked:
                key = j * FT + _lane(s.shape)
                tpos = q0 + _row(s.shape) % tpt
                s = jnp.where(key <= tpos, s, NEG_INF)
            mn = jnp.maximum(m, jnp.max(s, axis=-1, keepdims=True))
            a = jnp.exp(m - mn)
            p = jnp.exp(s - mn)
            pv = _dot_nt(p.astype(BF16), v_ref[0, u, j])
            out.append((mn, a * l + jnp.sum(p, axis=-1, keepdims=True), a * acc + pv))
        return tuple(out)

    init = tuple((jnp.full((rows, 1), -1e38, F32), jnp.zeros((rows, 1), F32), jnp.zeros((rows, LANES), F32))
                 for _ in range(n_sub))
    carry = lax.fori_loop(0, nfull, lambda j, c: step(j, c, False), init)
    pieces = []
    for _, l, acc in step(nfull, carry, True):
        o = acc * (1.0 / l)
        pieces += [o[r * tpt:(r + 1) * tpt] for r in range(n_stack)]
    for j, piece in enumerate(_pair_heads(pieces)):
        o_ref[0, :, j * LANES:(j + 1) * LANES] = piece


def _flash(q_aug, k_aug, v_dup, *, n_sub, n_stack, tpt, name):
    b, s, nq, rows, _ = q_aug.shape
    nkt = k_aug.shape[2]
    seq = nkt * FT
    assert FT % tpt == 0 and rows == n_stack * tpt
    wout = n_sub * n_stack * HEAD_DIM
    kv = pl.BlockSpec((1, n_sub, nkt, LANES, FT), lambda bb, ss, q: (bb, ss, 0, 0, 0))
    return pl.pallas_call(
        functools.partial(_flash_kernel, n_sub=n_sub, n_stack=n_stack, tpt=tpt),
        grid=(b, s // n_sub, nq),
        in_specs=[pl.BlockSpec((1, n_sub, 1, rows, LANES), lambda bb, ss, q: (bb, ss, q, 0, 0)), kv, kv],
        out_specs=pl.BlockSpec((1, tpt, wout), lambda bb, ss, q: (bb, q, ss)),
        out_shape=jax.ShapeDtypeStruct((b, seq, s * n_stack * HEAD_DIM), F32),
        compiler_params=_cparams(("arbitrary", "arbitrary", "arbitrary")), name=name,
    )(q_aug, k_aug, v_dup)


def _win_prompt_kernel(q_ref, k_ref, v_ref, o_ref, *, tq, nwt):
    qi = pl.program_id(2)
    q0 = qi * tq
    ts = jnp.maximum(qi - (nwt - 1), 0)
    qs = jnp.concatenate([q_ref[0, r] for r in range(NSA_REP)], axis=0)
    s = jnp.concatenate([_dot(qs, k_ref[0, 0, ts + i]) for i in range(nwt)], axis=1)
    key = ts * TKV + _lane(s.shape)
    t = q0 + _row(s.shape) % tq
    s = jnp.where((key <= t) & (key > t - WINDOW), s, NEG_INF)
    e = jnp.exp(s - jnp.max(s, axis=-1, keepdims=True))
    den = jnp.sum(e, axis=-1, keepdims=True)
    e = e.astype(BF16)
    o = _dot_nt(e[:, 0:TKV], v_ref[0, 0, ts])
    for i in range(1, nwt):
        o = o + _dot_nt(e[:, i * TKV:(i + 1) * TKV], v_ref[0, 0, ts + i])
    o = o * (1.0 / den)
    for j, piece in enumerate(_pair_heads([o[r * tq:(r + 1) * tq] for r in range(NSA_REP)])):
        o_ref[0, :, j * LANES:(j + 1) * LANES] = piece


def _win_prompt(q8, kwin, vwin, batch, seq):
    tq = TKV
    nkt = seq // TKV
    nwt = WINDOW // TKV + 1
    assert nkt >= nwt
    kv = pl.BlockSpec((1, 1, nkt, LANES, TKV), lambda b, g, q: (b, g, 0, 0, 0))
    return pl.pallas_call(
        functools.partial(_win_prompt_kernel, tq=tq, nwt=nwt),
        grid=(batch, NSA_KV_HEADS, seq // tq),
        in_specs=[pl.BlockSpec((1, NSA_REP, tq, LANES), lambda b, g, q: (b, g, q, 0)), kv, kv],
        out_specs=pl.BlockSpec((1, tq, 256), lambda b, g, q: (b, q, g)),
        out_shape=jax.ShapeDtypeStruct((batch, seq, 512), F32),
        compiler_params=_cparams(("arbitrary", "arbitrary", "arbitrary")), name="nsa_win_prompt",
    )(q8, kwin, vwin)


def _out_kernel(x_ref, nw_ref, d_ref, ocmp_ref, osel_ref, owin_ref, oc_ref,
                wg_ref, pw_ref, ps_ref, wa_ref, wb_ref, wc_ref, wo_ref, e3_ref, fn_ref, xo_ref, y_ref):
    x = x_ref[...]
    ms = jnp.mean(x * x, axis=-1, keepdims=True)
    hb = ((x * lax.rsqrt(ms + RMS_EPS)) * nw_ref[...]).astype(BF16)

    def gate(c0, n):
        return _dot(hb, wg_ref[:, c0:c0 + n])

    oa = _dot(d_ref[...].astype(BF16), pw_ref[...]) * ps_ref[...]
    ya = _dot((oa * jax.nn.silu(gate(G_ZA, 256))).astype(BF16), wa_ref[...])
    gx = _dot2(jax.nn.sigmoid(gate(G_SM, LANES)), e3_ref[...])
    ob = gx[:, 0:512] * ocmp_ref[...] + gx[:, 512:1024] * osel_ref[...] + gx[:, 1024:1536] * owin_ref[...]
    yb = _dot((ob * jax.nn.silu(gate(G_ZB, 512))).astype(BF16), wb_ref[...])
    yc = _dot((oc_ref[...] * jax.nn.silu(gate(G_ZC, 256))).astype(BF16), wc_ref[...])
    d = ya.shape[-1]
    mixed = (jax.nn.sigmoid(gate(G_GM, d)) * ya + jax.nn.sigmoid(gate(G_GM + d, d)) * yb
             + jax.nn.sigmoid(gate(G_GM + 2 * d, d)) * yc)
    xo = x + _dot(mixed.astype(BF16), wo_ref[...])
    xo_ref[...] = xo
    ms = jnp.mean(xo * xo, axis=-1, keepdims=True)
    y_ref[...] = (xo * lax.rsqrt(ms + RMS_EPS)) * fn_ref[...]


def _gate_expand_matrix():
    e = np.zeros((LANES, 3 * NSA_HEADS * HEAD_DIM), np.float32)
    for hd in range(NSA_HEADS):
        for c in range(3):
            e[SM_GB + hd * 3 + c, c * 512 + hd * HEAD_DIM:c * 512 + (hd + 1) * HEAD_DIM] = 1.0
    return jnp.asarray(e, BF16)


def _output(x2d, norm_w, d, ocmp, osel, owin, oc, lw, final_norm):
    n, dm = x2d.shape
    tm = min(256, n)
    row = lambda w: pl.BlockSpec((tm, w), lambda i: (i, 0))
    full = lambda a: pl.BlockSpec(a.shape, lambda i: (0,) * a.ndim)
    consts = [lw["w_gate"], lw["pool_bd"], lw["pool_scale"], lw["w_out_a"], lw["w_out_b"], lw["w_out_c"],
              lw["w_o"], _gate_expand_matrix(), final_norm.reshape(1, dm)]
    nw = norm_w.reshape(1, dm)
    return pl.pallas_call(
        _out_kernel, grid=(n // tm,),
        in_specs=[row(dm), full(nw), row(256), row(512), row(512), row(512), row(256)] + [full(c) for c in consts],
        out_specs=[row(dm), row(dm)],
        out_shape=[jax.ShapeDtypeStruct((n, dm), F32), jax.ShapeDtypeStruct((n, dm), F32)],
        compiler_params=_cparams(("arbitrary",)), name="out_proj",
    )(x2d, nw, d, ocmp, osel, owin, oc, *consts)


class _PageStream:
    def __init__(self, cache_ref, pt_ref, buf_ref, sem_ref, *, n_pages, nch, ppc, rows, nrows, look):
        self.refs = (cache_ref, pt_ref, buf_ref, sem_ref)
        self.n_pages, self.nch, self.ppc, self.rows, self.nrows, self.look = n_pages, nch, ppc, rows, nrows, look
        self.steps = len(rows) * nch
        self.nslot = look + 1
        self.g0 = pl.program_id(0) * self.steps
        self.total = pl.num_programs(0) * self.steps

    def _copies(self, g):
        cache_ref, pt_ref, buf_ref, sem_ref = self.refs
        f = g % self.steps
        row0 = self.rows[0]
        for p in range(1, len(self.rows)):
            row0 = jnp.where(f // self.nch == p, self.rows[p], row0)
        if len(self.rows) > 1:
            row0 = pl.multiple_of(row0, 8)
        base = (g // self.steps) * self.n_pages + (f % self.nch) * self.ppc
        slot = g % self.nslot
        return [pltpu.make_async_copy(cache_ref.at[pt_ref[base + i], pl.ds(row0, self.nrows), :],
                                      buf_ref.at[slot, :, pl.ds(i * PAGE, PAGE)], sem_ref.at[slot])
                for i in range(self.ppc)]

    def prime(self):
        @pl.when(pl.program_id(0) == 0)
        def _():
            for g in range(self.look):
                for cp in self._copies(g):
                    cp.start()

    def fetch(self, f):
        g = self.g0 + f

        @pl.when(g + self.look < self.total)
        def _():
            for cp in self._copies(g + self.look):
                cp.start()

        for cp in self._copies(g):
            cp.wait()
        return g % self.nslot


def _topk_extract(val, k_top):
    lane = _lane(val.shape)
    sel = jnp.zeros(val.shape, jnp.bool_)
    for _ in range(k_top):
        mx = jnp.max(val, axis=-1, keepdims=True)
        idx = jnp.min(jnp.where(val == mx, lane, val.shape[-1]), axis=-1, keepdims=True)
        pick = lane == idx
        sel = sel | pick
        val = jnp.where(pick, -3e38, val)
    return sel


def _topk_decode_kernel(val_ref, sel_ref, *, k_top):
    val = val_ref[...]
    sel = _topk_extract(val, k_top) & (val > -0.5)
    sel_ref[...] = jnp.where(sel, 0.0, -1.0)


def _topk_decode(val, k_top):
    return pl.pallas_call(
        functools.partial(_topk_decode_kernel, k_top=k_top), grid=(1,),
        in_specs=[pl.BlockSpec(val.shape, lambda i: (0, 0))],
        out_specs=pl.BlockSpec(val.shape, lambda i: (0, 0)),
        out_shape=jax.ShapeDtypeStruct(val.shape, F32),
        compiler_params=_cparams(("arbitrary",)), name="nsa_topk_decode",
    )(val)


def _cmp_decode_kernel(pt_ref, cache_ref, q_ref, amean_ref, pair_ref, o_ref, val_ref, buf_ref, sem_ref, kcs_ref,
                       *, n_pages, nch, ppc, past):
    stream = _PageStream(cache_ref, pt_ref, buf_ref, sem_ref, n_pages=n_pages, nch=nch, ppc=ppc,
                         rows=(0,), nrows=256, look=ATTN_LOOK)
    stream.prime()

    def chunk(c, carry):
        slot = stream.fetch(c)
        kcs_ref[c] = _dot2(buf_ref[slot], amean_ref[...])
        return carry

    lax.fori_loop(0, nch, chunk, 0)
    nb = n_pages * (PAGE // CMP_BLOCK)
    nsd = nb // 2
    bpc = ppc * (PAGE // CMP_BLOCK)
    means = jnp.concatenate([kcs_ref[cc][:, 0:bpc] for cc in range(nch)], axis=1).astype(BF16)
    for g in range(2):
        s = _dot(q_ref[0, g], means[0:LANES])
        nblk = _lane(s.shape)
        qpos = past + _row(s.shape) % 8
        p = _masked_softmax(s, (nblk + 1) * CMP_BLOCK - 1 <= qpos, -1)
        o_ref[0, g] = _dot_nt(p.astype(BF16), means[LANES:2 * LANES])
        imp = p[0:8]
        for r in range(1, NSA_REP):
            imp = imp + p[r * 8:(r + 1) * 8]
        imp = jnp.concatenate([_dot3(imp, pair_ref[...]), jnp.zeros((8, LANES), F32)], axis=1)
        blk = _lane(imp.shape)
        cur = (past + _row(imp.shape)) // SEL_BLOCK
        forced = (blk == 0) | (blk == cur) | (blk == cur - 1)
        val = jnp.where(blk <= cur, imp + jnp.where(forced, FORCE_SCORE, 0.0), -1.0)
        val_ref[0, g] = jnp.where(blk <= nsd, val, -3e38)


def _cmp_decode(pt_flat, cache, qsw, *, batch, n_pages, past):
    ppc = min(32, n_pages)
    nch = n_pages // ppc
    bpc = ppc * (PAGE // CMP_BLOCK)
    assert bpc % LANES == 0 or nch == 1
    nb = n_pages * (PAGE // CMP_BLOCK)
    nsd = nb // 2
    tok = np.arange(ppc * PAGE)
    amean = np.zeros((ppc * PAGE, LANES), np.float32)
    amean[tok, tok // CMP_BLOCK] = 1.0 / CMP_BLOCK
    pair = (np.arange(nb)[:, None] // 2 == np.arange(nsd)[None, :]).astype(np.float32)
    consts = [jnp.asarray(amean, BF16), jnp.asarray(pair, BF16)]
    grid_spec = pltpu.PrefetchScalarGridSpec(
        num_scalar_prefetch=1, grid=(batch,),
        in_specs=[pl.BlockSpec(memory_space=pl.ANY),
                  pl.BlockSpec((1, 2, 32, LANES), lambda b, pt: (b, 0, 0, 0))]
                 + [pl.BlockSpec(a.shape, lambda b, pt: (0, 0)) for a in consts],
        out_specs=[pl.BlockSpec((1, 2, 32, LANES), lambda b, pt: (b, 0, 0, 0)),
                   pl.BlockSpec((1, 2, 8, nsd + LANES), lambda b, pt: (b, 0, 0, 0))],
        scratch_shapes=[pltpu.VMEM((ATTN_LOOK + 1, 256, ppc * PAGE), F32), pltpu.SemaphoreType.DMA((ATTN_LOOK + 1,)),
                        pltpu.VMEM((nch, 256, LANES), F32)])
    return pl.pallas_call(
        functools.partial(_cmp_decode_kernel, n_pages=n_pages, nch=nch, ppc=ppc, past=past),
        grid_spec=grid_spec,
        out_shape=[jax.ShapeDtypeStruct((batch, 2, 32, LANES), F32),
                   jax.ShapeDtypeStruct((batch, 2, 8, nsd + LANES), F32)],
        compiler_params=_cparams(("arbitrary",)), name="nsa_cmp_decode",
    )(pt_flat, cache, qsw, *consts)


def _attn_decode_kernel(pt_ref, cache_ref, q_ref, bias_ref, bnew_ref, aux_ref, cq_ref, new_ref, o_ref,
                        buf_ref, sem_ref, s_ref, *, n_pages, nch, ppc, krow, vrow, kw, ncols, fox):
    ng = LANES // ncols
    tk = ppc * PAGE
    rows_s = (nch // ng) * tk
    stream = _PageStream(cache_ref, pt_ref, buf_ref, sem_ref, n_pages=n_pages, nch=nch, ppc=ppc,
                         rows=(krow, vrow), nrows=kw, look=ATTN_LOOK)
    stream.prime()
    s_ref[0:rows_s, :] = jnp.zeros((rows_s, LANES), F32)

    def scores(kt, k, bias_rows):
        s = _dot_tn(kt, q_ref[0, k])
        if fox:
            pieces = jnp.concatenate(list(_split3(bias_rows)) + [jnp.zeros(bias_rows.shape, BF16)], axis=0)
            s = s + _dot_tn(pieces, aux_ref[k]) + cq_ref[0, k, 0:1, :]
        return s

    def key_chunk(c, carry):
        slot = stream.fetch(c)
        k = c % ng
        kt = buf_ref[slot].astype(BF16)
        if fox:
            s = scores(kt, k, bias_ref[0, c])
        else:
            bpc = tk // SEL_BLOCK
            m = bias_ref[0, k, pl.ds(pl.multiple_of(c * bpc, 8), bpc), :]
            m = jnp.broadcast_to(m[:, None, :], (bpc, SEL_BLOCK, LANES)).reshape(tk, LANES)
            s = jnp.where(m > -0.5, scores(kt, k, None), NEG_INF)
        rows = pl.ds(pl.multiple_of((c // ng) * tk, tk), tk)
        s_ref[rows, :] += s
        return carry

    lax.fori_loop(0, nch, key_chunk, 0)

    kn = new_ref[0, 0:kw, :].astype(BF16)
    if fox:
        sn = scores(kn, 0, bnew_ref[0])
        tq = _lane(sn.shape) // 8
    else:
        sn = jnp.where(bnew_ref[0, 0:1, :] > -0.5, scores(kn, 0, None), NEG_INF)
        tq = _lane(sn.shape) % 8
    sn = jnp.where((_row(sn.shape) <= tq) & (_lane(sn.shape) < ncols), sn, NEG_INF)
    s_ref[rows_s:rows_s + PAGE, :] = sn

    def all_groups(v, op):
        v8 = jnp.broadcast_to(v, (8, LANES))
        out = v8
        for kk in range(1, ng):
            out = op(out, pltpu.roll(v8, kk * ncols, 1))
        return out[0:1]

    s = s_ref[...]
    ok = s > 0.5 * NEG_INF
    mx = all_groups(jnp.max(s, axis=0, keepdims=True), jnp.maximum)
    e = jnp.where(ok, jnp.exp(s - mx), 0.0)
    den = all_groups(jnp.sum(e, axis=0, keepdims=True), jnp.add)
    s_ref[...] = e * (1.0 / jnp.maximum(den, 1e-30))

    def value_chunk(c, acc):
        slot = stream.fetch(nch + c)
        p = s_ref[pl.ds(pl.multiple_of((c // ng) * tk, tk), tk), :]
        p = jnp.where(_lane(p.shape) // ncols == c % ng, p, 0.0).astype(BF16)
        return acc + _dot(buf_ref[slot].astype(BF16), p)

    acc = lax.fori_loop(0, nch, value_chunk, jnp.zeros((kw, LANES), F32))
    acc = acc + _dot(new_ref[0, kw:2 * kw, :].astype(BF16), s_ref[rows_s:rows_s + PAGE, :].astype(BF16))
    out = acc
    for kk in range(1, ng):
        out = out + pltpu.roll(acc, kk * ncols, 1)
    o_ref[0] = out


def _attn_decode(pt_flat, cache, q, bias, bnew, aux, cq, new_rows, *, batch, n_pages, krow, vrow, kw, ncols, fox,
                 name):
    ng = LANES // ncols
    ppc = min(32, n_pages // ng)
    nch = n_pages // ppc
    assert nch % ng == 0 and n_pages % ppc == 0
    tk = ppc * PAGE
    blk = lambda a: pl.BlockSpec((1,) + a.shape[1:], lambda b, pt: (b,) + (0,) * (a.ndim - 1))
    grid_spec = pltpu.PrefetchScalarGridSpec(
        num_scalar_prefetch=1, grid=(batch,),
        in_specs=[pl.BlockSpec(memory_space=pl.ANY), blk(q), blk(bias), blk(bnew),
                  pl.BlockSpec(aux.shape, lambda b, pt: (0, 0, 0)), blk(cq), blk(new_rows)],
        out_specs=pl.BlockSpec((1, kw, LANES), lambda b, pt: (b, 0, 0)),
        scratch_shapes=[pltpu.VMEM((ATTN_LOOK + 1, kw, tk), F32), pltpu.SemaphoreType.DMA((ATTN_LOOK + 1,)),
                        pltpu.VMEM(((nch // ng) * tk + PAGE, LANES), F32)])
    return pl.pallas_call(
        functools.partial(_attn_decode_kernel, n_pages=n_pages, nch=nch, ppc=ppc, krow=krow, vrow=vrow,
                          kw=kw, ncols=ncols, fox=fox),
        grid_spec=grid_spec, out_shape=jax.ShapeDtypeStruct((batch, kw, LANES), F32),
        compiler_params=_cparams(("arbitrary",)), name=name,
    )(pt_flat, cache, q, bias, bnew, aux, cq, new_rows)


def _win_decode_kernel(q_ref, st_ref, new_ref, o_ref, ns_ref, *, tdec):
    st32 = st_ref[0]
    nw32 = new_ref[0]
    lw = st32.shape[1]
    shifted = pltpu.roll(st32, lw - tdec, 1)
    tail = jnp.where(_lane(nw32.shape) >= PAGE - tdec, pltpu.roll(nw32, PAGE - tdec, 1), shifted[:, lw - PAGE:])
    ns_ref[0] = jnp.concatenate([shifted[:, 0:lw - PAGE], tail], axis=1)
    st = st32.astype(BF16)
    nw = nw32.astype(BF16)
    for g in range(2):
        q = q_ref[0, g]
        s = jnp.concatenate([_dot(q, st[0:LANES]), _dot(q, nw[0:LANES])], axis=1)
        i = _lane(s.shape)
        t = _row(s.shape) % 8
        p = _masked_softmax(s, (i > t + lw - WINDOW) & (i <= t + lw), -1).astype(BF16)
        o_ref[0, g] = _dot_nt(p[:, 0:lw], st[LANES:2 * LANES]) + _dot_nt(p[:, lw:], nw[LANES:2 * LANES])


def _win_decode(qsw, state_t, new_t, batch, tdec):
    lw = state_t.shape[2]
    return pl.pallas_call(
        functools.partial(_win_decode_kernel, tdec=tdec), grid=(batch,),
        in_specs=[pl.BlockSpec((1, 2, 32, LANES), lambda b: (b, 0, 0, 0)),
                  pl.BlockSpec((1, 256, lw), lambda b: (b, 0, 0)),
                  pl.BlockSpec((1, 256, PAGE), lambda b: (b, 0, 0))],
        out_specs=[pl.BlockSpec((1, 2, 32, LANES), lambda b: (b, 0, 0, 0)),
                   pl.BlockSpec((1, 256, lw), lambda b: (b, 0, 0))],
        out_shape=[jax.ShapeDtypeStruct((batch, 2, 32, LANES), F32), jax.ShapeDtypeStruct((batch, 256, lw), F32)],
        compiler_params=_cparams(("arbitrary",)), name="nsa_win_decode",
    )(qsw, state_t, new_t)


def _logf_decode_kernel(pt_ref, cache_ref, new_ref, u_ref, s_ref, perm_ref, tri_ref, c_ref, buf_ref, sem_ref,
                        *, n_pages):
    b = pl.program_id(0)
    nb = pl.num_programs(0)

    def copies(bb, slot):
        return [pltpu.make_async_copy(cache_ref.at[pl.ds(pt_ref[bb * n_pages + i] * FOX_HEADS, FOX_HEADS), :],
                                      buf_ref.at[slot, pl.ds(i * FOX_HEADS, FOX_HEADS), :], sem_ref.at[slot])
                for i in range(n_pages)]

    slot = b % 2

    @pl.when(b == 0)
    def _():
        for cp in copies(b, slot):
            cp.start()

    @pl.when(b + 1 < nb)
    def _():
        for cp in copies(b + 1, 1 - slot):
            cp.start()

    for cp in copies(b, slot):
        cp.wait()
    lp = jnp.concatenate([buf_ref[slot], new_ref[0]], axis=0)
    cs = _dot3(lp, u_ref[...])
    tot = _dot3(cs, s_ref[...])
    c_ref[0] = _dot3_left(perm_ref[...], cs) + _dot3_left(tri_ref[...], tot)


def _logf_decode(pt_flat, cache2d, new_rows, *, batch, n_pages):
    nr = (n_pages + 2) * FOX_HEADS
    r = np.arange(PAGE)
    u = (r[:, None] <= r[None, :]).astype(np.float32)
    last = np.zeros((PAGE, PAGE), np.float32)
    last[PAGE - 1, :] = 1.0
    rr = np.arange(nr)
    tri = ((rr[:, None] % FOX_HEADS == rr[None, :] % FOX_HEADS)
           & (rr[None, :] // FOX_HEADS < rr[:, None] // FOX_HEADS)).astype(np.float32)
    perm = np.zeros((nr, nr), np.float32)
    perm[(rr % FOX_HEADS) * (n_pages + 2) + rr // FOX_HEADS, rr] = 1.0
    consts = [jnp.asarray(u, BF16), jnp.asarray(last, BF16), jnp.asarray(perm, BF16), jnp.asarray(perm @ tri, BF16)]
    grid_spec = pltpu.PrefetchScalarGridSpec(
        num_scalar_prefetch=1, grid=(batch,),
        in_specs=[pl.BlockSpec(memory_space=pl.ANY),
                  pl.BlockSpec((1, 2 * FOX_HEADS, PAGE), lambda b, pt: (b, 0, 0))]
                 + [pl.BlockSpec(c.shape, lambda b, pt: (0, 0)) for c in consts],
        out_specs=pl.BlockSpec((1, nr, PAGE), lambda b, pt: (b, 0, 0)),
        scratch_shapes=[pltpu.VMEM((2, n_pages * FOX_HEADS, PAGE), F32), pltpu.SemaphoreType.DMA((2,))])
    return pl.pallas_call(
        functools.partial(_logf_decode_kernel, n_pages=n_pages),
        grid_spec=grid_spec, out_shape=jax.ShapeDtypeStruct((batch, nr, PAGE), F32),
        compiler_params=_cparams(("arbitrary",)), name="fox_logf_decode",
    )(pt_flat, cache2d, new_rows, *consts)


def _prep_layer_weights(w_in_t, pool_w, pool_scale, w_out_a, w_out_b, w_out_c, w_o):
    d = w_in_t.shape[1]
    tok = jnp.concatenate([w_in_t[0:256], w_in_t[512:1024], w_in_t[2328:2584]], axis=0)
    gate = jnp.concatenate([w_in_t[256:512], w_in_t[1816:2328], w_in_t[3100:6428], w_in_t[1792:1816],
                            jnp.zeros((LANES - 24, d), F32)], axis=0)
    feat = jnp.concatenate([w_in_t[1024:1792], w_in_t[2584:3100], jnp.zeros((8 - FOX_HEADS, d), F32)],
                           axis=0).astype(BF16)
    gd = pool_w.shape[-1]
    bd = jnp.zeros((POOL_WIDTH, POOL_WIDTH), F32)
    for g in range(pool_w.shape[0]):
        bd = bd.at[g * gd:(g + 1) * gd, g * gd:(g + 1) * gd].set(pool_w[g])
    return dict(w_tok=_wprep(tok), w_gate=_wprep(gate), w_feat=feat, pool_bd=bd.astype(BF16),
                pool_scale=pool_scale.reshape(1, -1),
                w_out_a=w_out_a.astype(BF16), w_out_b=w_out_b.astype(BF16), w_out_c=w_out_c.astype(BF16),
                w_o=w_o.astype(BF16))


def _to_token_last(a_t, lead):
    b, _, t = a_t.shape
    nd = len(lead)
    return a_t.reshape((b,) + lead + (t,)).transpose((0, nd + 1) + tuple(range(1, nd + 1)))


def _layer_prompt(x2d, lw, norm_w, fox_bf, final_norm, batch, seq):
    pos = jnp.arange(seq)
    (uz, q8, nsat, wint, foxt, logft, fq, kcm, ksel, vsel, kwin, vwin, fk, fv) = _project(
        x2d, norm_w, lw["w_tok"], lw["w_feat"], fox_bf, pos, prompt=True, batch=batch, seq=seq)
    d = _pool_prompt(uz, seq)
    bpt = TKV // CMP_BLOCK
    nb = seq // CMP_BLOCK
    means = kcm[..., :bpt].transpose(0, 1, 3, 2).reshape(batch, nb, 256)
    means = jnp.concatenate([means[:, 0::2], means[:, 1::2]], axis=1)
    dup = lambda m: jnp.stack([jnp.concatenate([m[..., g * HEAD_DIM:(g + 1) * HEAD_DIM]] * 2, axis=-1)
                               for g in range(2)], axis=1).astype(BF16)
    ocmp, qaug = _cmp_prompt(dup(means[..., 0:LANES]), dup(means[..., LANES:2 * LANES]), q8, batch, seq)
    osel = _flash(qaug, ksel, vsel, n_sub=2, n_stack=NSA_REP, tpt=LANES, name="nsa_sel_prompt")
    owin = _win_prompt(q8, kwin, vwin, batch, seq)
    tpt = 512
    oc = _flash(fq.reshape(batch, FOX_HEADS, seq // tpt, tpt, LANES), fk, fv,
                n_sub=2, n_stack=1, tpt=tpt, name="fox_prompt")
    n = batch * seq
    xo, y = _output(x2d, norm_w, d, ocmp.reshape(n, 512), osel.reshape(n, 512), owin.reshape(n, 512),
                    oc.reshape(n, 256), lw, final_norm)
    wl = min(WINDOW, seq)
    states = (_to_token_last(nsat, (4, NSA_KV_HEADS, HEAD_DIM)),
              _to_token_last(foxt, (2, FOX_HEADS, HEAD_DIM)),
              logft[:, :FOX_HEADS].transpose(0, 2, 1),
              _to_token_last(wint[:, :, seq - wl:], (2, NSA_KV_HEADS, HEAD_DIM)),
              uz[:, :POOL_WIDTH].reshape(batch, seq, POOL_WIDTH)[:, -POOL_STATE:])
    return xo, y, states


def _new_page(a_t, batch, tdec):
    f = a_t.shape[0]
    a = a_t.reshape(f, batch, tdec).transpose(1, 0, 2)
    return jnp.concatenate([a, jnp.zeros((batch, f, PAGE - tdec), a.dtype)], axis=2)


def _layer_decode(x2d, lw, norm_w, fox_bf, final_norm, pt_flat, nsa_cache, fox_cache, logf_cache,
                  state_win_t, state_pool_tb, batch, tdec, n_pages):
    past = n_pages * PAGE
    n = batch * tdec
    pos = past + jnp.tile(jnp.arange(tdec), batch)
    (uz, q8, nsat, wint, foxt, logft, qc) = _project(
        x2d, norm_w, lw["w_tok"], lw["w_feat"], fox_bf, pos, prompt=False, batch=batch, seq=tdec)
    nsat, wint, foxt, logft = nsat[0], wint[0], foxt[0], logft[0]

    u_tb = uz[:, :POOL_WIDTH].reshape(batch, tdec, POOL_WIDTH).transpose(1, 0, 2)
    pool_seq = jnp.concatenate([state_pool_tb, u_tb], axis=0)
    d = _pool_decode(pool_seq, past, tdec).transpose(1, 0, 2).reshape(n, POOL_WIDTH)

    q5 = q8[0].reshape(NSA_KV_HEADS, NSA_REP, batch, tdec, LANES)
    qs = q5.transpose(2, 0, 1, 3, 4)
    qs = jnp.concatenate([qs, jnp.zeros((batch, 2, NSA_REP, 8 - tdec, LANES), BF16)], axis=3)
    qs = qs.reshape(batch, 2, NSA_REP * 8, LANES)
    qsw = jnp.stack([qs[:, 0], jnp.roll(qs[:, 1], HALF, axis=-1)], axis=1)
    ocmp_k, val = _cmp_decode(pt_flat, nsa_cache, qsw, batch=batch, n_pages=n_pages, past=past)
    selm = _topk_decode(val.reshape(batch * 16, -1), min(SEL_TOPK, past // SEL_BLOCK + 1)).reshape(val.shape)

    def unstack_groups(o):
        o = o.reshape(batch, 2, NSA_REP, 8, 2, HEAD_DIM)[:, :, :, :tdec]
        o = jnp.stack([o[:, 0, :, :, 0], o[:, 1, :, :, 1]], axis=1)
        return o.transpose(0, 3, 1, 2, 4).reshape(n, NSA_HEADS * HEAD_DIM)

    ocmp = unstack_groups(ocmp_k)
    owin_k, win_next_t = _win_decode(qsw, state_win_t, _new_page(wint, batch, tdec), batch, tdec)
    owin = unstack_groups(owin_k)

    nsd = past // SEL_BLOCK
    q6 = q5[..., :HEAD_DIM].transpose(2, 1, 0, 3, 4)
    q6 = jnp.concatenate([q6, jnp.zeros((batch, NSA_REP, 2, 8 - tdec, HEAD_DIM), BF16)], axis=3)
    eye2 = jnp.eye(2, dtype=BF16)
    qbd = q6[:, :, :, :, None, :] * eye2[None, None, :, None, :, None]
    qbd_t = qbd.reshape(batch, 64, LANES).transpose(0, 2, 1)
    qbd_t = jnp.concatenate([qbd_t, jnp.zeros((batch, LANES, LANES - 64), BF16)], axis=2)
    sel_t = jnp.tile(selm.reshape(batch, 16, nsd + LANES).transpose(0, 2, 1), (1, 1, NSA_REP))
    sel_t = jnp.concatenate([sel_t, jnp.zeros((batch, nsd + LANES, LANES - 64), F32)], axis=2)
    groups = lambda a, ng: jnp.stack([jnp.roll(a, k * (LANES // ng), axis=-1) for k in range(ng)], axis=1)
    osel_k = _attn_decode(pt_flat, nsa_cache, groups(qbd_t, 2), groups(sel_t[:, :nsd], 2), sel_t[:, nsd:nsd + 8],
                          jnp.zeros((2, 32, LANES), BF16), jnp.zeros((batch, 2, 8, LANES), F32),
                          _new_page(nsat[256:512], batch, tdec),
                          batch=batch, n_pages=n_pages, krow=256, vrow=384, kw=LANES, ncols=64, fox=False,
                          name="nsa_sel_decode")
    os_ = osel_k[:, :, :64].reshape(batch, 2, HEAD_DIM, NSA_REP, 2, 8)[..., :tdec]
    osel = jnp.stack([os_[:, 0, :, :, 0], os_[:, 1, :, :, 1]], axis=1)
    osel = osel.transpose(0, 4, 1, 3, 2).reshape(n, 512)

    lnew = _new_page(logft, batch, tdec)
    lnew = jnp.concatenate([lnew[:, :FOX_HEADS], jnp.zeros((batch, FOX_HEADS, PAGE), F32)], axis=1)
    csum = _logf_decode(pt_flat, logf_cache, lnew, batch=batch, n_pages=n_pages)
    c4 = csum.reshape(batch, FOX_HEADS, (n_pages + 2) * PAGE)
    ck8 = jnp.concatenate([-c4, jnp.zeros_like(c4)], axis=1)
    cq = jnp.concatenate([c4[:, :, past:past + tdec].transpose(0, 2, 1),
                          jnp.zeros((batch, tdec, 8 - FOX_HEADS), F32)], axis=2).reshape(batch, 1, tdec * 8)
    cq = jnp.concatenate([cq, jnp.zeros((batch, 7, tdec * 8), F32)], axis=1)
    cq = jnp.concatenate([cq, jnp.zeros((batch, 8, LANES - tdec * 8), F32)], axis=2)
    qf = (qc * SCALE).reshape(batch, tdec, FOX_HEADS, HEAD_DIM)
    eye = jnp.eye(8, FOX_HEADS, dtype=F32)
    qfbd = (qf[:, :, None, :, :] * eye[None, None, :, :, None]).reshape(batch, tdec * 8, FOX_HEADS * HEAD_DIM)
    qf_t = jnp.concatenate([qfbd.transpose(0, 2, 1), jnp.zeros((batch, 256, LANES - tdec * 8), F32)],
                           axis=2).astype(BF16)
    col = np.arange(LANES)
    head_cols = np.zeros((32, LANES), np.float32)
    for p in range(3):
        head_cols[p * 8 + col[:tdec * 8] % 8, col[:tdec * 8]] = 1.0
    fng = LANES // (tdec * 8)
    fppc = min(32, n_pages // fng)
    ck_chunks = ck8[:, :, :past].reshape(batch, 8, n_pages // fppc, fppc * PAGE).transpose(0, 2, 1, 3)
    oc_k = _attn_decode(pt_flat, fox_cache, groups(qf_t, fng), ck_chunks, ck8[:, :, past:past + PAGE],
                        groups(jnp.asarray(head_cols, BF16)[None], fng)[0], groups(cq, fng),
                        _new_page(foxt, batch, tdec),
                        batch=batch, n_pages=n_pages, krow=0, vrow=256, kw=256, ncols=tdec * 8, fox=True,
                        name="fox_decode")
    o6 = oc_k[:, :, :tdec * 8].reshape(batch, FOX_HEADS, HEAD_DIM, tdec, 8)
    oc = jnp.stack([o6[:, h, :, :, h] for h in range(FOX_HEADS)], axis=1)
    oc = oc.transpose(0, 3, 1, 2).reshape(n, 256)

    xo, y = _output(x2d, norm_w, d, ocmp, osel, owin, oc, lw, final_norm)
    states = (nsat.T.reshape(batch, tdec, 4, NSA_KV_HEADS, HEAD_DIM),
              foxt.T.reshape(batch, tdec, 2, FOX_HEADS, HEAD_DIM),
              logft[:FOX_HEADS].T.reshape(batch, tdec, FOX_HEADS),
              _to_token_last(win_next_t, (2, NSA_KV_HEADS, HEAD_DIM)),
              pool_seq[-POOL_STATE:].transpose(1, 0, 2))
    return xo, y, states


def kernel(x_prompt, x_sample, cache_nsa_kv, cache_fox_kv, cache_fox_logf, state_win_kv, state_pool, page_table,
           norm_w, w_in, pool_w, pool_scale, fox_bf, w_out_a, w_out_b, w_out_c, w_o, final_norm):
    depth = w_in.shape[0]
    batch, seq, dm = x_prompt.shape
    dbatch, tdec, _ = x_sample.shape
    n_phys, page = cache_nsa_kv.shape[1], cache_nsa_kv.shape[2]
    n_pages = page_table.shape[1]
    assert page == PAGE and tdec == 4 and n_pages % min(32, n_pages) == 0
    assert state_win_kv.shape[2] == WINDOW and n_pages * PAGE >= WINDOW

    nsa_cache = cache_nsa_kv.transpose(0, 1, 3, 4, 5, 2).reshape(depth * n_phys, 512, PAGE)
    fox_cache = cache_fox_kv.transpose(0, 1, 3, 4, 5, 2).reshape(depth * n_phys, 512, PAGE)
    logf_cache = cache_fox_logf.transpose(0, 1, 3, 2).reshape(depth * n_phys * FOX_HEADS, PAGE)
    win_state_t = state_win_kv.transpose(0, 1, 3, 4, 5, 2).reshape(depth, dbatch, 256, WINDOW)
    pool_state_tb = state_pool.transpose(0, 2, 1, 3)
    w_in_t = w_in.transpose(0, 2, 1)

    hp = x_prompt.reshape(batch * seq, dm)
    hs = x_sample.reshape(dbatch * tdec, dm)
    st_p, st_s = [], []
    for layer in range(depth):
        lw = _prep_layer_weights(w_in_t[layer], pool_w[layer], pool_scale[layer], w_out_a[layer],
                                 w_out_b[layer], w_out_c[layer], w_o[layer])
        hp, yp, sp = _layer_prompt(hp, lw, norm_w[layer], fox_bf[layer], final_norm, batch, seq)
        pt_flat = (page_table + layer * n_phys).reshape(-1).astype(jnp.int32)
        hs, ys, ss = _layer_decode(hs, lw, norm_w[layer], fox_bf[layer], final_norm, pt_flat, nsa_cache, fox_cache,
                                   logf_cache, win_state_t[layer], pool_state_tb[layer], dbatch, tdec, n_pages)
        st_p.append(sp)
        st_s.append(ss)
    outs = [yp.reshape(batch, seq, dm), ys.reshape(dbatch, tdec, dm)]
    for k in range(5):
        outs.append(jnp.stack([s[k] for s in st_p], 0))
        outs.append(jnp.stack([s[k] for s in st_s], 0))
    return tuple(outs)
```

```python
import functools

import numpy as np
import jax
import jax.numpy as jnp
from jax import lax
from jax.experimental import pallas as pl
from jax.experimental.pallas import tpu as pltpu

F32 = jnp.float32
BF16 = jnp.bfloat16

HEAD_DIM = 64
POOL_WIDTH = 256
POOL_WINDOWS = (2, 4, 8, 16)
POOL_STATE = 15
NSA_HEADS = 8
NSA_KV_HEADS = 2
NSA_REP = NSA_HEADS // NSA_KV_HEADS
CMP_BLOCK = 32
SEL_BLOCK = 64
SEL_TOPK = 16
WINDOW = 512
FORCE_SCORE = 1e4
FOX_HEADS = 4
ROPE_THETA = 10000.0
RMS_EPS = 1e-6
NEG_INF = -1e30
SCALE = HEAD_DIM ** -0.5

LANES = 128
HALF = LANES // 2
PAGE = 128
VMEM_LIMIT = 56 * 1024 * 1024

C_U, C_QB, C_QC, C_END = 0, 256, 768, 1024
G_ZA, G_ZB, G_ZC, G_GM, G_SM, G_END = 0, 256, 768, 1024, 4096, 4224
R_KV, R_KC, R_VC, R_FC, R_END = 0, 768, 1024, 1280, 1288
SM_GB = 0
TKV = 256
FT = 1024
ATTN_LOOK = 2
SEL_TPT = 128


def _cparams(sem):
    return pltpu.CompilerParams(dimension_semantics=sem, vmem_limit_bytes=VMEM_LIMIT)


def _lane(shape):
    return lax.broadcasted_iota(jnp.int32, shape, len(shape) - 1)


def _row(shape):
    return lax.broadcasted_iota(jnp.int32, shape, len(shape) - 2)


def _split3(x):
    hi = x.astype(BF16)
    r1 = x - hi.astype(F32)
    mid = r1.astype(BF16)
    lo = (r1 - mid.astype(F32)).astype(BF16)
    return hi, mid, lo


def _dot(a, b):
    return jnp.dot(a, b, preferred_element_type=F32)


def _dot_nt(a, b):
    return lax.dot_general(a, b, (((1,), (1,)), ((), ())), preferred_element_type=F32)


def _dot_tn(a, b):
    return lax.dot_general(a, b, (((0,), (0,)), ((), ())), preferred_element_type=F32)


def _dot3(x, m):
    n = x.shape[0]
    r = _dot(jnp.concatenate(_split3(x), axis=0), m)
    return r[0:n] + r[n:2 * n] + r[2 * n:3 * n]


def _dot2(x, m):
    n = x.shape[0]
    hi = x.astype(BF16)
    lo = (x - hi.astype(F32)).astype(BF16)
    r = _dot(jnp.concatenate([hi, lo], axis=0), m)
    return r[0:n] + r[n:2 * n]


def _dot3_left(m, x):
    n = x.shape[1]
    r = _dot(m, jnp.concatenate(_split3(x), axis=1))
    return r[:, 0:n] + r[:, n:2 * n] + r[:, 2 * n:3 * n]


def _swap_halves(x):
    return pltpu.roll(x, HALF, x.ndim - 1)


def _to_low_half(x, g, fill):
    lo = _lane(x.shape) < HALF
    return jnp.where(lo, x if g == 0 else _swap_halves(x), fill)


def _pair_heads(pieces):
    out = []
    for j in range(0, len(pieces), 2):
        lo = _lane(pieces[j].shape) < HALF
        out.append(jnp.where(lo, pieces[j], pieces[j + 1]))
    return out


def _masked_softmax(s, ok, axis):
    s = jnp.where(ok, s, NEG_INF)
    m = jnp.max(s, axis=axis, keepdims=True)
    e = jnp.where(ok, jnp.exp(s - m), 0.0)
    return e * (1.0 / jnp.maximum(jnp.sum(e, axis=axis, keepdims=True), 1e-30))


def _wprep_kernel(wt_ref, w_ref):
    w_ref[...] = wt_ref[...].T.astype(BF16)


def _wprep(wt):
    cols, d = wt.shape
    return pl.pallas_call(
        _wprep_kernel, grid=(cols // LANES,),
        in_specs=[pl.BlockSpec((LANES, d), lambda i: (i, 0))],
        out_specs=pl.BlockSpec((d, LANES), lambda i: (0, i)),
        out_shape=jax.ShapeDtypeStruct((d, cols), BF16),
        compiler_params=_cparams(("arbitrary",)), name="w_in_transpose",
    )(wt)


def _rope_tile(x, cosf, sinf):
    lane = _lane(x.shape)
    first = (lane % HEAD_DIM) < (HEAD_DIM // 2)
    rot = jnp.where(first, pltpu.roll(x, LANES - HEAD_DIM // 2, 1), pltpu.roll(x, HEAD_DIM // 2, 1))
    return x * cosf + rot * sinf


def _rope_rows(x, cos_t, sin_t):
    half = HEAD_DIM // 2
    out = []
    for h in range(x.shape[0] // HEAD_DIM):
        x1 = x[h * HEAD_DIM:h * HEAD_DIM + half]
        x2 = x[h * HEAD_DIM + half:(h + 1) * HEAD_DIM]
        out += [x1 * cos_t - x2 * sin_t, x2 * cos_t + x1 * sin_t]
    return jnp.concatenate(out, axis=0)


def _proj_kernel(*refs, prompt, tiles_per_seq):
    if prompt:
        (x_ref, nw_ref, wt_ref, wk_ref, cos_ref, sin_ref, cost_ref, sint_ref, bfc_ref,
         ohb_ref, amean_ref, triu_ref, sq_ref,
         u_ref, q8_ref, nsat_ref, wint_ref, foxt_ref, logft_ref,
         fq_ref, kcm_ref, ksel_ref, vsel_ref, kwin_ref, vwin_ref, fk_ref, fv_ref, carry_ref) = refs
    else:
        (x_ref, nw_ref, wt_ref, wk_ref, cos_ref, sin_ref, cost_ref, sint_ref, bfc_ref,
         u_ref, q8_ref, nsat_ref, wint_ref, foxt_ref, logft_ref, qc_ref) = refs

    x = x_ref[...]
    ms = jnp.mean(x * x, axis=-1, keepdims=True)
    hb = ((x * lax.rsqrt(ms + RMS_EPS)) * nw_ref[...]).astype(BF16)
    tm = x.shape[0]

    def seg(c0, n):
        return _dot(hb, wt_ref[:, c0:c0 + n])

    def seg_t(r0, n):
        return _dot_nt(wk_ref[r0:r0 + n, :], hb)

    u_ref[...] = seg(C_U, POOL_WIDTH)
    cosf = cos_ref[...]
    sinf = sin_ref[...]
    for c in range(4):
        y = _rope_tile(seg(C_QB + c * LANES, LANES), cosf, sinf) * SCALE
        q8_ref[0, 2 * c] = _to_low_half(y, 0, 0.0).astype(BF16)
        q8_ref[0, 2 * c + 1] = _to_low_half(y, 1, 0.0).astype(BF16)
    qc = seg(C_QC, 256)

    cos_t = cost_ref[...]
    sin_t = sint_ref[...]
    kcmp = _rope_rows(seg_t(R_KV, LANES), cos_t, sin_t)
    vcmp = seg_t(R_KV + 128, LANES)
    ksel = _rope_rows(seg_t(R_KV + 256, LANES), cos_t, sin_t)
    vsel = seg_t(R_KV + 384, LANES)
    kwin = _rope_rows(seg_t(R_KV + 512, LANES), cos_t, sin_t)
    vwin = seg_t(R_KV + 640, LANES)
    nsat = nsat_ref.at[0]
    foxt = foxt_ref.at[0]
    nsat[0:128] = kcmp
    nsat[128:256] = vcmp
    nsat[256:384] = ksel
    nsat[384:512] = vsel
    wint_ref[0, 0:128] = kwin
    wint_ref[0, 128:256] = vwin
    kc = seg_t(R_KC, 256)
    vc = seg_t(R_VC, 256)
    foxt[0:256] = kc
    foxt[256:512] = vc
    f_t = seg_t(R_FC, 8)
    logf = jnp.where(_row(f_t.shape) < FOX_HEADS, jax.nn.log_sigmoid(f_t + bfc_ref[...]), 0.0)
    logft_ref[0] = logf

    if not prompt:
        qc_ref[...] = qc
        return

    kcm_ref[0, 0] = _dot3(jnp.concatenate([kcmp, vcmp], axis=0), amean_ref[...])
    ohb = ohb_ref[...]
    for g in range(2):
        gs = slice(g * HEAD_DIM, (g + 1) * HEAD_DIM)
        ksel_ref[0, g, 0] = jnp.concatenate([ksel[gs].astype(BF16), ohb], axis=0)
        vsel_ref[0, g, 0] = jnp.concatenate([vsel[gs]] * 2, axis=0).astype(BF16)
        kwin_ref[0, g, 0] = jnp.concatenate([kwin[gs]] * 2, axis=0).astype(BF16)
        vwin_ref[0, g, 0] = jnp.concatenate([vwin[gs]] * 2, axis=0).astype(BF16)

    i = pl.program_id(0)
    carry = jnp.where(i % tiles_per_seq == 0, 0.0, carry_ref[:, 0:1])
    csum = _dot3(logf, triu_ref[...]) + carry
    carry_ref[...] = jnp.broadcast_to(csum[:, tm - 1:tm], carry_ref.shape)
    c_hi, c_mid, c_lo = [p.astype(F32) for p in _split3(csum)]
    r8 = _row((8, tm))
    c_tok = jnp.concatenate([csum, jnp.zeros((LANES - 8, tm), F32)], axis=0).T
    cq = _dot(jnp.concatenate(_split3(c_tok), axis=1), sq_ref[...])
    lane = _lane((tm, LANES))
    one_q = ((lane >= HALF + 3) & (lane < HALF + 6)).astype(F32)
    for h in range(FOX_HEADS):
        t, g = h // 2, h % 2
        hs = slice(h * HEAD_DIM, (h + 1) * HEAD_DIM)
        aug = jnp.where(r8 < 3, 1.0, jnp.where(r8 == 3, -c_hi[h:h + 1], jnp.where(
            r8 == 4, -c_mid[h:h + 1], jnp.where(r8 == 5, -c_lo[h:h + 1], 0.0))))
        fk_ref[0, h, 0] = jnp.concatenate([kc[hs], aug, jnp.zeros((HALF - 8, tm), F32)], axis=0).astype(BF16)
        fv_ref[0, h, 0] = jnp.concatenate([vc[hs]] * 2, axis=0).astype(BF16)
        fq_ref[0, h] = _to_low_half(qc[:, t * LANES:(t + 1) * LANES] * SCALE, g,
                                    cq[:, h * LANES:(h + 1) * LANES] + one_q).astype(BF16)


def _rope_tables(pos):
    half = HEAD_DIM // 2
    inv = ROPE_THETA ** (-jnp.arange(half, dtype=F32) / half)
    ang = pos.astype(F32)[:, None] * inv[None, :]
    cos, sin = jnp.cos(ang), jnp.sin(ang)
    return jnp.tile(cos, (1, 4)), jnp.tile(jnp.concatenate([-sin, sin], 1), (1, 2)), cos.T, sin.T


def _fox_query_aug_matrix():
    sq = np.zeros((3 * LANES, FOX_HEADS * LANES), np.float32)
    for h in range(FOX_HEADS):
        for p in range(3):
            sq[p * LANES + h, h * LANES + HALF + p] = 1.0
    return jnp.asarray(sq, BF16)


def _project(x2d, norm_w, w_tok, w_feat, fox_bf, pos, *, prompt, batch, seq):
    n, d = x2d.shape
    tm = TKV if prompt else n
    tiles_per_seq = seq // tm if prompt else 1
    cosf, sinf, cos_t, sin_t = _rope_tables(pos)
    bf_col = jnp.zeros((8, 1), F32).at[0:FOX_HEADS, 0].set(fox_bf)
    row = lambda w: pl.BlockSpec((tm, w), lambda i: (i, 0))
    full = lambda a: pl.BlockSpec(a.shape, lambda i: (0,) * a.ndim)
    if prompt:
        tab = pl.BlockSpec((tm, LANES), lambda i: (i % tiles_per_seq, 0))
        tab_t = pl.BlockSpec((HEAD_DIM // 2, tm), lambda i: (0, i % tiles_per_seq))
    else:
        tab = row(LANES)
        tab_t = pl.BlockSpec((HEAD_DIM // 2, tm), lambda i: (0, i))
    in_arrays = [x2d, norm_w.reshape(1, d), w_tok, w_feat, cosf, sinf, cos_t, sin_t, bf_col]
    in_specs = [row(d), full(in_arrays[1]), full(w_tok), full(w_feat), tab, tab, tab_t, tab_t, full(bf_col)]
    nb = batch if prompt else 1
    bidx = lambda i: (i // tiles_per_seq, 0, i % tiles_per_seq)
    feat = lambda r: pl.BlockSpec((1, r, tm), bidx)
    hb = lambda nh: pl.BlockSpec((1, nh, tm, LANES), lambda i: (i // tiles_per_seq, 0, i % tiles_per_seq, 0))
    nt = (seq if prompt else n)
    out_shapes = [
        jax.ShapeDtypeStruct((n, POOL_WIDTH), F32), jax.ShapeDtypeStruct((nb, NSA_HEADS, nt, LANES), BF16),
        jax.ShapeDtypeStruct((nb, 512, nt), F32), jax.ShapeDtypeStruct((nb, 256, nt), F32),
        jax.ShapeDtypeStruct((nb, 512, nt), F32), jax.ShapeDtypeStruct((nb, 8, nt), F32),
    ]
    out_specs = [row(POOL_WIDTH), hb(NSA_HEADS), feat(512), feat(256), feat(512), feat(8)]
    scratch = []
    if prompt:
        t = np.arange(seq)
        ohb = np.zeros((HALF, seq), np.float32)
        ohb[t // SEL_BLOCK, t] = -NEG_INF
        tt = np.arange(tm)
        amean = np.zeros((tm, LANES), np.float32)
        amean[tt, tt // CMP_BLOCK] = 1.0 / CMP_BLOCK
        triu = np.triu(np.ones((tm, tm), np.float32))
        extra = [jnp.asarray(ohb, BF16), jnp.asarray(amean, BF16), jnp.asarray(triu, BF16), _fox_query_aug_matrix()]
        in_arrays += extra
        in_specs += [pl.BlockSpec((HALF, tm), lambda i: (0, i % tiles_per_seq)),
                     full(extra[1]), full(extra[2]), full(extra[3])]
        nkt = seq // tm
        kvt = lambda ns: pl.BlockSpec((1, ns, 1, LANES, tm),
                                      lambda i: (i // tiles_per_seq, 0, i % tiles_per_seq, 0, 0))
        out_shapes.append(jax.ShapeDtypeStruct((batch, FOX_HEADS, seq, LANES), BF16))
        out_specs.append(hb(FOX_HEADS))
        out_shapes.append(jax.ShapeDtypeStruct((batch, nkt, 256, LANES), F32))
        out_specs.append(pl.BlockSpec((1, 1, 256, LANES), lambda i: (i // tiles_per_seq, i % tiles_per_seq, 0, 0)))
        tpf = FT // tm
        kvf = lambda ns: pl.BlockSpec(
            (1, ns, 1, LANES, tm),
            lambda i: (i // tiles_per_seq, 0, (i % tiles_per_seq) // tpf, 0, (i % tiles_per_seq) % tpf))
        for ns, flash in ((2, True), (2, True), (2, False), (2, False), (FOX_HEADS, True), (FOX_HEADS, True)):
            if flash:
                out_shapes.append(jax.ShapeDtypeStruct((batch, ns, seq // FT, LANES, FT), BF16))
                out_specs.append(kvf(ns))
            else:
                out_shapes.append(jax.ShapeDtypeStruct((batch, ns, nkt, LANES, tm), BF16))
                out_specs.append(kvt(ns))
        scratch = [pltpu.VMEM((8, LANES), F32)]
    else:
        out_shapes.append(jax.ShapeDtypeStruct((n, 256), F32))
        out_specs.append(row(256))
    return pl.pallas_call(
        functools.partial(_proj_kernel, prompt=prompt, tiles_per_seq=tiles_per_seq),
        grid=(n // tm,), in_specs=in_specs, out_specs=out_specs, out_shape=out_shapes,
        scratch_shapes=scratch, compiler_params=_cparams(("arbitrary",)),
        name="proj_prompt" if prompt else "proj_decode",
    )(*in_arrays)


def _pool_mix(u, prev, pos):
    grp = _lane(u.shape) // (POOL_WIDTH // len(POOL_WINDOWS))
    acc = u
    win = None
    for j in range(1, POOL_WINDOWS[-1]):
        acc = acc + prev(j)
        if j + 1 in POOL_WINDOWS:
            k = POOL_WINDOWS.index(j + 1)
            win = acc if win is None else jnp.where(grp >= k, acc, win)
    w = jnp.zeros(u.shape, jnp.int32)
    for k, wk in enumerate(POOL_WINDOWS):
        w = jnp.where(grp == k, wk, w)
    cnt = jnp.minimum(pos + 1, w).astype(F32)
    return win / cnt - u


def _pool_prompt_kernel(u_ref, halo_ref, d_ref, s_ref, *, tiles_per_seq, tm):
    i = pl.program_id(0) % tiles_per_seq
    s_ref[0:16, :] = jnp.where(i == 0, 0.0, halo_ref[...])
    s_ref[16:, :] = u_ref[...]
    pos = i * tm + _row((tm, POOL_WIDTH))
    d_ref[...] = _pool_mix(u_ref[...], lambda j: s_ref[16 - j:16 - j + tm, :], pos)


def _pool_prompt(uz, seq):
    n = uz.shape[0]
    tm = 512
    tiles_per_seq = seq // tm
    return pl.pallas_call(
        functools.partial(_pool_prompt_kernel, tiles_per_seq=tiles_per_seq, tm=tm),
        grid=(n // tm,),
        in_specs=[pl.BlockSpec((tm, POOL_WIDTH), lambda i: (i, 0)),
                  pl.BlockSpec((16, POOL_WIDTH), lambda i: (jnp.maximum(i * (tm // 16) - 1, 0), 0))],
        out_specs=pl.BlockSpec((tm, POOL_WIDTH), lambda i: (i, 0)),
        out_shape=jax.ShapeDtypeStruct((n, POOL_WIDTH), F32),
        scratch_shapes=[pltpu.VMEM((tm + 16, POOL_WIDTH), F32)],
        compiler_params=_cparams(("arbitrary",)), name="pool_prompt",
    )(uz, uz)


def _pool_decode_kernel(seq_ref, d_ref, *, pos0, tdec):
    for t in range(tdec):
        pos = jnp.full(seq_ref.shape[1:], pos0 + t, jnp.int32)
        d_ref[t] = _pool_mix(seq_ref[POOL_STATE + t], lambda j: seq_ref[POOL_STATE + t - j], pos)


def _pool_decode(seq_tb, pos0, tdec):
    _, b, w = seq_tb.shape
    return pl.pallas_call(
        functools.partial(_pool_decode_kernel, pos0=pos0, tdec=tdec),
        grid=(1,),
        in_specs=[pl.BlockSpec(seq_tb.shape, lambda i: (0, 0, 0))],
        out_specs=pl.BlockSpec((tdec, b, w), lambda i: (0, 0, 0)),
        out_shape=jax.ShapeDtypeStruct((tdec, b, w), F32),
        compiler_params=_cparams(("arbitrary",)), name="pool_decode",
    )(seq_tb)


def _select_rank(imp, blk, cur, k_top):
    forced = (blk == 0) | (blk == cur) | (blk == cur - 1)
    val = jnp.where(blk <= cur, imp + jnp.where(forced, FORCE_SCORE, 0.0), -1.0)
    ns = val.shape[0]
    sel = jnp.zeros(val.shape, jnp.bool_)
    for _ in range(k_top):
        mx = jnp.max(val, axis=0, keepdims=True)
        idx = jnp.min(jnp.where(val == mx, blk, ns), axis=0, keepdims=True)
        pick = blk == idx
        sel = sel | pick
        val = jnp.where(pick, -3e38, val)
    return sel & (blk <= cur)


def _cmp_prompt_kernel(kc_ref, vc_ref, q_ref, o_ref, qaug_ref, *, tq, seq):
    qi = pl.program_id(1)
    ns = seq // SEL_BLOCK
    imps = []
    for g in range(NSA_KV_HEADS):
        qs = jnp.concatenate([q_ref[0, g * NSA_REP + r] for r in range(NSA_REP)], axis=0)
        s = _dot_nt(kc_ref[0, g], qs)
        row = _row(s.shape)
        nblk = jnp.where(row < ns, 2 * row, 2 * (row - ns) + 1)
        t = qi * tq + _lane(s.shape) % tq
        p = _masked_softmax(s, (nblk + 1) * CMP_BLOCK - 1 <= t, 0)
        o = _dot_tn(p.astype(BF16), vc_ref[0, g])
        pieces = _pair_heads([o[r * tq:(r + 1) * tq] for r in range(NSA_REP)])
        for j, piece in enumerate(pieces):
            o_ref[0, :, (2 * g + j) * LANES:(2 * g + j + 1) * LANES] = piece
        imp = p[:, 0:tq]
        for r in range(1, NSA_REP):
            imp = imp + p[:, r * tq:(r + 1) * tq]
        imps.append(imp[:ns] + imp[ns:])

    imp = jnp.concatenate(imps, axis=1)
    blk = _row(imp.shape)
    cur = (qi * tq + _lane(imp.shape) % tq) // SEL_BLOCK
    selm = jnp.where(_select_rank(imp, blk, cur, min(SEL_TOPK, ns)), 0.0, -1.0)
    for g in range(NSA_KV_HEADS):
        pad = [jnp.zeros((HALF, tq), F32), selm[:, g * tq:(g + 1) * tq]]
        if ns < HALF:
            pad.append(jnp.zeros((HALF - ns, tq), F32))
        sel_t = jnp.concatenate(pad, axis=0).T
        for h in range(tq // SEL_TPT):
            tok = slice(h * SEL_TPT, (h + 1) * SEL_TPT)
            for r in range(NSA_REP):
                qaug_ref[0, g, h, r * SEL_TPT:(r + 1) * SEL_TPT, :] = (
                    q_ref[0, g * NSA_REP + r, tok, :].astype(F32) + sel_t[tok]).astype(BF16)


def _cmp_prompt(kc_dup, vc_dup, q8, batch, seq):
    tq = 2 * SEL_TPT
    nq = seq // tq
    nb = seq // CMP_BLOCK
    assert seq % tq == 0 and seq // SEL_BLOCK <= HALF
    kv = pl.BlockSpec((1, NSA_KV_HEADS, nb, LANES), lambda b, q: (b, 0, 0, 0))
    return pl.pallas_call(
        functools.partial(_cmp_prompt_kernel, tq=tq, seq=seq),
        grid=(batch, nq),
        in_specs=[kv, kv, pl.BlockSpec((1, NSA_HEADS, tq, LANES), lambda b, q: (b, 0, q, 0))],
        out_specs=[pl.BlockSpec((1, tq, 512), lambda b, q: (b, q, 0)),
                   pl.BlockSpec((1, NSA_KV_HEADS, tq // SEL_TPT, NSA_REP * SEL_TPT, LANES),
                                lambda b, q: (b, 0, q, 0, 0))],
        out_shape=[jax.ShapeDtypeStruct((batch, seq, 512), F32),
                   jax.ShapeDtypeStruct((batch, NSA_KV_HEADS, seq // SEL_TPT, NSA_REP * SEL_TPT, LANES), BF16)],
        compiler_params=_cparams(("arbitrary", "arbitrary")), name="nsa_cmp_prompt",
    )(kc_dup, vc_dup, q8)


def _flash_kernel(q_ref, k_ref, v_ref, o_ref, *, n_sub, n_stack, tpt):
    qi = pl.program_id(2)
    q0 = qi * tpt
    nfull = q0 // FT
    rows = n_stack * tpt
    qs = [q_ref[0, u, 0] for u in range(n_sub)]

    def step(j, carry, masked):
        ss = [_dot(qs[u], k_ref[0, u, j]) for u in range(n_sub)]
        out = []
        for u in range(n_sub):
            m, l, acc = carry[u]
            s = ss[u]
            if masked:
                key = j * FT + _lane(s.shape)
                tpos = q0 + _row(s.shape) % tpt
                s = jnp.where(key <= tpos, s, NEG_INF)
            mn = jnp.maximum(m, jnp.max(s, axis=-1, keepdims=True))
            a = jnp.exp(m - mn)
            p = jnp.exp(s - mn)
            pv = _dot_nt(p.astype(BF16), v_ref[0, u, j])
            out.append((mn, a * l + jnp.sum(p, axis=-1, keepdims=True), a * acc + pv))
        return tuple(out)

    init = tuple((jnp.full((rows, 1), -1e38, F32), jnp.zeros((rows, 1), F32), jnp.zeros((rows, LANES), F32))
                 for _ in range(n_sub))
    carry = lax.fori_loop(0, nfull, lambda j, c: step(j, c, False), init)
    pieces = []
    for _, l, acc in step(nfull, carry, True):
        o = acc * (1.0 / l)
        pieces += [o[r * tpt:(r + 1) * tpt] for r in range(n_stack)]
    for j, piece in enumerate(_pair_heads(pieces)):
        o_ref[0, :, j * LANES:(j + 1) * LANES] = piece


def _flash(q_aug, k_aug, v_dup, *, n_sub, n_stack, tpt, name):
    b, s, nq, rows, _ = q_aug.shape
    nkt = k_aug.shape[2]
    seq = nkt * FT
    assert FT % tpt == 0 and rows == n_stack * tpt
    wout = n_sub * n_stack * HEAD_DIM
    kv = pl.BlockSpec((1, n_sub, nkt, LANES, FT), lambda bb, ss, q: (bb, ss, 0, 0, 0))
    return pl.pallas_call(
        functools.partial(_flash_kernel, n_sub=n_sub, n_stack=n_stack, tpt=tpt),
        grid=(b, s // n_sub, nq),
        in_specs=[pl.BlockSpec((1, n_sub, 1, rows, LANES), lambda bb, ss, q: (bb, ss, q, 0, 0)), kv, kv],
        out_specs=pl.BlockSpec((1, tpt, wout), lambda bb, ss, q: (bb, q, ss)),
        out_shape=jax.ShapeDtypeStruct((b, seq, s * n_stack * HEAD_DIM), F32),
        compiler_params=_cparams(("arbitrary", "arbitrary", "arbitrary")), name=name,
    )(q_aug, k_aug, v_dup)


def _win_prompt_kernel(q_ref, k_ref, v_ref, o_ref, *, tq, nwt):
    qi = pl.program_id(2)
    q0 = qi * tq
    ts = jnp.maximum(qi - (nwt - 1), 0)
    qs = jnp.concatenate([q_ref[0, r] for r in range(NSA_REP)], axis=0)
    s = jnp.concatenate([_dot(qs, k_ref[0, 0, ts + i]) for i in range(nwt)], axis=1)
    key = ts * TKV + _lane(s.shape)
    t = q0 + _row(s.shape) % tq
    s = jnp.where((key <= t) & (key > t - WINDOW), s, NEG_INF)
    e = jnp.exp(s - jnp.max(s, axis=-1, keepdims=True))
    den = jnp.sum(e, axis=-1, keepdims=True)
    e = e.astype(BF16)
    o = _dot_nt(e[:, 0:TKV], v_ref[0, 0, ts])
    for i in range(1, nwt):
        o = o + _dot_nt(e[:, i * TKV:(i + 1) * TKV], v_ref[0, 0, ts + i])
    o = o * (1.0 / den)
    for j, piece in enumerate(_pair_heads([o[r * tq:(r + 1) * tq] for r in range(NSA_REP)])):
        o_ref[0, :, j * LANES:(j + 1) * LANES] = piece


def _win_prompt(q8, kwin, vwin, batch, seq):
    tq = TKV
    nkt = seq // TKV
    nwt = WINDOW // TKV + 1
    assert nkt >= nwt
    kv = pl.BlockSpec((1, 1, nkt, LANES, TKV), lambda b, g, q: (b, g, 0, 0, 0))
    return pl.pallas_call(
        functools.partial(_win_prompt_kernel, tq=tq, nwt=nwt),
        grid=(batch, NSA_KV_HEADS, seq // tq),
        in_specs=[pl.BlockSpec((1, NSA_REP, tq, LANES), lambda b, g, q: (b, g, q, 0)), kv, kv],
        out_specs=pl.BlockSpec((1, tq, 256), lambda b, g, q: (b, q, g)),
        out_shape=jax.ShapeDtypeStruct((batch, seq, 512), F32),
        compiler_params=_cparams(("arbitrary", "arbitrary", "arbitrary")), name="nsa_win_prompt",
    )(q8, kwin, vwin)


def _out_kernel(x_ref, nw_ref, d_ref, ocmp_ref, osel_ref, owin_ref, oc_ref,
                wg_ref, pw_ref, ps_ref, wa_ref, wb_ref, wc_ref, wo_ref, e3_ref, fn_ref, xo_ref, y_ref):
    x = x_ref[...]
    ms = jnp.mean(x * x, axis=-1, keepdims=True)
    hb = ((x * lax.rsqrt(ms + RMS_EPS)) * nw_ref[...]).astype(BF16)

    def gate(c0, n):
        return _dot(hb, wg_ref[:, c0:c0 + n])

    oa = _dot(d_ref[...].astype(BF16), pw_ref[...]) * ps_ref[...]
    ya = _dot((oa * jax.nn.silu(gate(G_ZA, 256))).astype(BF16), wa_ref[...])
    gx = _dot2(jax.nn.sigmoid(gate(G_SM, LANES)), e3_ref[...])
    ob = gx[:, 0:512] * ocmp_ref[...] + gx[:, 512:1024] * osel_ref[...] + gx[:, 1024:1536] * owin_ref[...]
    yb = _dot((ob * jax.nn.silu(gate(G_ZB, 512))).astype(BF16), wb_ref[...])
    yc = _dot((oc_ref[...] * jax.nn.silu(gate(G_ZC, 256))).astype(BF16), wc_ref[...])
    d = ya.shape[-1]
    mixed = (jax.nn.sigmoid(gate(G_GM, d)) * ya + jax.nn.sigmoid(gate(G_GM + d, d)) * yb
             + jax.nn.sigmoid(gate(G_GM + 2 * d, d)) * yc)
    xo = x + _dot(mixed.astype(BF16), wo_ref[...])
    xo_ref[...] = xo
    ms = jnp.mean(xo * xo, axis=-1, keepdims=True)
    y_ref[...] = (xo * lax.rsqrt(ms + RMS_EPS)) * fn_ref[...]


def _gate_expand_matrix():
    e = np.zeros((LANES, 3 * NSA_HEADS * HEAD_DIM), np.float32)
    for hd in range(NSA_HEADS):
        for c in range(3):
            e[SM_GB + hd * 3 + c, c * 512 + hd * HEAD_DIM:c * 512 + (hd + 1) * HEAD_DIM] = 1.0
    return jnp.asarray(e, BF16)


def _output(x2d, norm_w, d, ocmp, osel, owin, oc, lw, final_norm):
    n, dm = x2d.shape
    tm = min(512, n)
    row = lambda w: pl.BlockSpec((tm, w), lambda i: (i, 0))
    full = lambda a: pl.BlockSpec(a.shape, lambda i: (0,) * a.ndim)
    consts = [lw["w_gate"], lw["pool_bd"], lw["pool_scale"], lw["w_out_a"], lw["w_out_b"], lw["w_out_c"],
              lw["w_o"], _gate_expand_matrix(), final_norm.reshape(1, dm)]
    nw = norm_w.reshape(1, dm)
    return pl.pallas_call(
        _out_kernel, grid=(n // tm,),
        in_specs=[row(dm), full(nw), row(256), row(512), row(512), row(512), row(256)] + [full(c) for c in consts],
        out_specs=[row(dm), row(dm)],
        out_shape=[jax.ShapeDtypeStruct((n, dm), F32), jax.ShapeDtypeStruct((n, dm), F32)],
        compiler_params=_cparams(("arbitrary",)), name="out_proj",
    )(x2d, nw, d, ocmp, osel, owin, oc, *consts)


class _PageStream:
    def __init__(self, cache_ref, pt_ref, buf_ref, sem_ref, *, n_pages, nch, ppc, rows, nrows, look):
        self.refs = (cache_ref, pt_ref, buf_ref, sem_ref)
        self.n_pages, self.nch, self.ppc, self.rows, self.nrows, self.look = n_pages, nch, ppc, rows, nrows, look
        self.steps = len(rows) * nch
        self.nslot = look + 1
        self.g0 = pl.program_id(0) * self.steps
        self.total = pl.num_programs(0) * self.steps

    def _copies(self, g):
        cache_ref, pt_ref, buf_ref, sem_ref = self.refs
        f = g % self.steps
        row0 = self.rows[0]
        for p in range(1, len(self.rows)):
            row0 = jnp.where(f // self.nch == p, self.rows[p], row0)
        if len(self.rows) > 1:
            row0 = pl.multiple_of(row0, 8)
        base = (g // self.steps) * self.n_pages + (f % self.nch) * self.ppc
        slot = g % self.nslot
        return [pltpu.make_async_copy(cache_ref.at[pt_ref[base + i], pl.ds(row0, self.nrows), :],
                                      buf_ref.at[slot, :, pl.ds(i * PAGE, PAGE)], sem_ref.at[slot])
                for i in range(self.ppc)]

    def prime(self):
        @pl.when(pl.program_id(0) == 0)
        def _():
            for g in range(self.look):
                for cp in self._copies(g):
                    cp.start()

    def fetch(self, f):
        g = self.g0 + f

        @pl.when(g + self.look < self.total)
        def _():
            for cp in self._copies(g + self.look):
                cp.start()

        for cp in self._copies(g):
            cp.wait()
        return g % self.nslot


def _topk_extract(val, k_top):
    lane = _lane(val.shape)
    sel = jnp.zeros(val.shape, jnp.bool_)
    for _ in range(k_top):
        mx = jnp.max(val, axis=-1, keepdims=True)
        idx = jnp.min(jnp.where(val == mx, lane, val.shape[-1]), axis=-1, keepdims=True)
        pick = lane == idx
        sel = sel | pick
        val = jnp.where(pick, -3e38, val)
    return sel


def _topk_decode_kernel(val_ref, sel_ref, *, k_top):
    val = val_ref[...]
    sel = _topk_extract(val, k_top) & (val > -0.5)
    sel_ref[...] = jnp.where(sel, 0.0, -1.0)


def _topk_decode(val, k_top):
    return pl.pallas_call(
        functools.partial(_topk_decode_kernel, k_top=k_top), grid=(1,),
        in_specs=[pl.BlockSpec(val.shape, lambda i: (0, 0))],
        out_specs=pl.BlockSpec(val.shape, lambda i: (0, 0)),
        out_shape=jax.ShapeDtypeStruct(val.shape, F32),
        compiler_params=_cparams(("arbitrary",)), name="nsa_topk_decode",
    )(val)


def _cmp_decode_kernel(pt_ref, cache_ref, q_ref, amean_ref, pair_ref, o_ref, val_ref, buf_ref, sem_ref, kcs_ref,
                       *, n_pages, nch, ppc, past):
    stream = _PageStream(cache_ref, pt_ref, buf_ref, sem_ref, n_pages=n_pages, nch=nch, ppc=ppc,
                         rows=(0,), nrows=256, look=ATTN_LOOK)
    stream.prime()

    def chunk(c, carry):
        slot = stream.fetch(c)
        kcs_ref[c] = _dot2(buf_ref[slot], amean_ref[...])
        return carry

    lax.fori_loop(0, nch, chunk, 0)
    nb = n_pages * (PAGE // CMP_BLOCK)
    nsd = nb // 2
    bpc = ppc * (PAGE // CMP_BLOCK)
    means = jnp.concatenate([kcs_ref[cc][:, 0:bpc] for cc in range(nch)], axis=1).astype(BF16)
    for g in range(2):
        s = _dot(q_ref[0, g], means[0:LANES])
        nblk = _lane(s.shape)
        qpos = past + _row(s.shape) % 8
        p = _masked_softmax(s, (nblk + 1) * CMP_BLOCK - 1 <= qpos, -1)
        o_ref[0, g] = _dot_nt(p.astype(BF16), means[LANES:2 * LANES])
        imp = p[0:8]
        for r in range(1, NSA_REP):
            imp = imp + p[r * 8:(r + 1) * 8]
        imp = jnp.concatenate([_dot3(imp, pair_ref[...]), jnp.zeros((8, LANES), F32)], axis=1)
        blk = _lane(imp.shape)
        cur = (past + _row(imp.shape)) // SEL_BLOCK
        forced = (blk == 0) | (blk == cur) | (blk == cur - 1)
        val = jnp.where(blk <= cur, imp + jnp.where(forced, FORCE_SCORE, 0.0), -1.0)
        val_ref[0, g] = jnp.where(blk <= nsd, val, -3e38)


def _cmp_decode(pt_flat, cache, qsw, *, batch, n_pages, past):
    ppc = min(32, n_pages)
    nch = n_pages // ppc
    bpc = ppc * (PAGE // CMP_BLOCK)
    assert bpc % LANES == 0 or nch == 1
    nb = n_pages * (PAGE // CMP_BLOCK)
    nsd = nb // 2
    tok = np.arange(ppc * PAGE)
    amean = np.zeros((ppc * PAGE, LANES), np.float32)
    amean[tok, tok // CMP_BLOCK] = 1.0 / CMP_BLOCK
    pair = (np.arange(nb)[:, None] // 2 == np.arange(nsd)[None, :]).astype(np.float32)
    consts = [jnp.asarray(amean, BF16), jnp.asarray(pair, BF16)]
    grid_spec = pltpu.PrefetchScalarGridSpec(
        num_scalar_prefetch=1, grid=(batch,),
        in_specs=[pl.BlockSpec(memory_space=pl.ANY),
                  pl.BlockSpec((1, 2, 32, LANES), lambda b, pt: (b, 0, 0, 0))]
                 + [pl.BlockSpec(a.shape, lambda b, pt: (0, 0)) for a in consts],
        out_specs=[pl.BlockSpec((1, 2, 32, LANES), lambda b, pt: (b, 0, 0, 0)),
                   pl.BlockSpec((1, 2, 8, nsd + LANES), lambda b, pt: (b, 0, 0, 0))],
        scratch_shapes=[pltpu.VMEM((ATTN_LOOK + 1, 256, ppc * PAGE), F32), pltpu.SemaphoreType.DMA((ATTN_LOOK + 1,)),
                        pltpu.VMEM((nch, 256, LANES), F32)])
    return pl.pallas_call(
        functools.partial(_cmp_decode_kernel, n_pages=n_pages, nch=nch, ppc=ppc, past=past),
        grid_spec=grid_spec,
        out_shape=[jax.ShapeDtypeStruct((batch, 2, 32, LANES), F32),
                   jax.ShapeDtypeStruct((batch, 2, 8, nsd + LANES), F32)],
        compiler_params=_cparams(("arbitrary",)), name="nsa_cmp_decode",
    )(pt_flat, cache, qsw, *consts)


def _attn_decode_kernel(pt_ref, cache_ref, q_ref, bias_ref, bnew_ref, aux_ref, cq_ref, new_ref, o_ref,
                        buf_ref, sem_ref, s_ref, *, n_pages, nch, ppc, krow, vrow, kw, ncols, fox):
    ng = LANES // ncols
    tk = ppc * PAGE
    rows_s = (nch // ng) * tk
    stream = _PageStream(cache_ref, pt_ref, buf_ref, sem_ref, n_pages=n_pages, nch=nch, ppc=ppc,
                         rows=(krow, vrow), nrows=kw, look=ATTN_LOOK)
    stream.prime()
    s_ref[0:rows_s, :] = jnp.zeros((rows_s, LANES), F32)

    def scores(kt, k, bias_rows):
        s = _dot_tn(kt, q_ref[0, k])
        if fox:
            pieces = jnp.concatenate(list(_split3(bias_rows)) + [jnp.zeros(bias_rows.shape, BF16)], axis=0)
            s = s + _dot_tn(pieces, aux_ref[k]) + cq_ref[0, k, 0:1, :]
        return s

    def key_chunk(c, carry):
        slot = stream.fetch(c)
        k = c % ng
        kt = buf_ref[slot].astype(BF16)
        if fox:
            s = scores(kt, k, bias_ref[0, c])
        else:
            bpc = tk // SEL_BLOCK
            m = bias_ref[0, k, pl.ds(pl.multiple_of(c * bpc, 8), bpc), :]
            m = jnp.broadcast_to(m[:, None, :], (bpc, SEL_BLOCK, LANES)).reshape(tk, LANES)
            s = jnp.where(m > -0.5, scores(kt, k, None), NEG_INF)
        rows = pl.ds(pl.multiple_of((c // ng) * tk, tk), tk)
        s_ref[rows, :] += s
        return carry

    lax.fori_loop(0, nch, key_chunk, 0)

    kn = new_ref[0, 0:kw, :].astype(BF16)
    if fox:
        sn = scores(kn, 0, bnew_ref[0])
        tq = _lane(sn.shape) // 8
    else:
        sn = jnp.where(bnew_ref[0, 0:1, :] > -0.5, scores(kn, 0, None), NEG_INF)
        tq = _lane(sn.shape) % 8
    sn = jnp.where((_row(sn.shape) <= tq) & (_lane(sn.shape) < ncols), sn, NEG_INF)
    s_ref[rows_s:rows_s + PAGE, :] = sn

    def all_groups(v, op):
        v8 = jnp.broadcast_to(v, (8, LANES))
        out = v8
        for kk in range(1, ng):
            out = op(out, pltpu.roll(v8, kk * ncols, 1))
        return out[0:1]

    s = s_ref[...]
    ok = s > 0.5 * NEG_INF
    mx = all_groups(jnp.max(s, axis=0, keepdims=True), jnp.maximum)
    e = jnp.where(ok, jnp.exp(s - mx), 0.0)
    den = all_groups(jnp.sum(e, axis=0, keepdims=True), jnp.add)
    s_ref[...] = e * (1.0 / jnp.maximum(den, 1e-30))

    def value_chunk(c, acc):
        slot = stream.fetch(nch + c)
        p = s_ref[pl.ds(pl.multiple_of((c // ng) * tk, tk), tk), :]
        p = jnp.where(_lane(p.shape) // ncols == c % ng, p, 0.0).astype(BF16)
        return acc + _dot(buf_ref[slot].astype(BF16), p)

    acc = lax.fori_loop(0, nch, value_chunk, jnp.zeros((kw, LANES), F32))
    acc = acc + _dot(new_ref[0, kw:2 * kw, :].astype(BF16), s_ref[rows_s:rows_s + PAGE, :].astype(BF16))
    out = acc
    for kk in range(1, ng):
        out = out + pltpu.roll(acc, kk * ncols, 1)
    o_ref[0] = out


def _attn_decode(pt_flat, cache, q, bias, bnew, aux, cq, new_rows, *, batch, n_pages, krow, vrow, kw, ncols, fox,
                 name):
    ng = LANES // ncols
    ppc = min(32, n_pages // ng)
    nch = n_pages // ppc
    assert nch % ng == 0 and n_pages % ppc == 0
    tk = ppc * PAGE
    blk = lambda a: pl.BlockSpec((1,) + a.shape[1:], lambda b, pt: (b,) + (0,) * (a.ndim - 1))
    grid_spec = pltpu.PrefetchScalarGridSpec(
        num_scalar_prefetch=1, grid=(batch,),
        in_specs=[pl.BlockSpec(memory_space=pl.ANY), blk(q), blk(bias), blk(bnew),
                  pl.BlockSpec(aux.shape, lambda b, pt: (0, 0, 0)), blk(cq), blk(new_rows)],
        out_specs=pl.BlockSpec((1, kw, LANES), lambda b, pt: (b, 0, 0)),
        scratch_shapes=[pltpu.VMEM((ATTN_LOOK + 1, kw, tk), F32), pltpu.SemaphoreType.DMA((ATTN_LOOK + 1,)),
                        pltpu.VMEM(((nch // ng) * tk + PAGE, LANES), F32)])
    return pl.pallas_call(
        functools.partial(_attn_decode_kernel, n_pages=n_pages, nch=nch, ppc=ppc, krow=krow, vrow=vrow,
                          kw=kw, ncols=ncols, fox=fox),
        grid_spec=grid_spec, out_shape=jax.ShapeDtypeStruct((batch, kw, LANES), F32),
        compiler_params=_cparams(("arbitrary",)), name=name,
    )(pt_flat, cache, q, bias, bnew, aux, cq, new_rows)


def _win_decode_kernel(q_ref, st_ref, new_ref, o_ref, ns_ref, *, tdec):
    st32 = st_ref[0]
    nw32 = new_ref[0]
    lw = st32.shape[1]
    shifted = pltpu.roll(st32, lw - tdec, 1)
    tail = jnp.where(_lane(nw32.shape) >= PAGE - tdec, pltpu.roll(nw32, PAGE - tdec, 1), shifted[:, lw - PAGE:])
    ns_ref[0] = jnp.concatenate([shifted[:, 0:lw - PAGE], tail], axis=1)
    st = st32.astype(BF16)
    nw = nw32.astype(BF16)
    for g in range(2):
        q = q_ref[0, g]
        s = jnp.concatenate([_dot(q, st[0:LANES]), _dot(q, nw[0:LANES])], axis=1)
        i = _lane(s.shape)
        t = _row(s.shape) % 8
        p = _masked_softmax(s, (i > t + lw - WINDOW) & (i <= t + lw), -1).astype(BF16)
        o_ref[0, g] = _dot_nt(p[:, 0:lw], st[LANES:2 * LANES]) + _dot_nt(p[:, lw:], nw[LANES:2 * LANES])


def _win_decode(qsw, state_t, new_t, batch, tdec):
    lw = state_t.shape[2]
    return pl.pallas_call(
        functools.partial(_win_decode_kernel, tdec=tdec), grid=(batch,),
        in_specs=[pl.BlockSpec((1, 2, 32, LANES), lambda b: (b, 0, 0, 0)),
                  pl.BlockSpec((1, 256, lw), lambda b: (b, 0, 0)),
                  pl.BlockSpec((1, 256, PAGE), lambda b: (b, 0, 0))],
        out_specs=[pl.BlockSpec((1, 2, 32, LANES), lambda b: (b, 0, 0, 0)),
                   pl.BlockSpec((1, 256, lw), lambda b: (b, 0, 0))],
        out_shape=[jax.ShapeDtypeStruct((batch, 2, 32, LANES), F32), jax.ShapeDtypeStruct((batch, 256, lw), F32)],
        compiler_params=_cparams(("arbitrary",)), name="nsa_win_decode",
    )(qsw, state_t, new_t)


def _logf_decode_kernel(pt_ref, cache_ref, new_ref, u_ref, s_ref, perm_ref, tri_ref, c_ref, buf_ref, sem_ref,
                        *, n_pages):
    b = pl.program_id(0)
    nb = pl.num_programs(0)

    def copies(bb, slot):
        return [pltpu.make_async_copy(cache_ref.at[pl.ds(pt_ref[bb * n_pages + i] * FOX_HEADS, FOX_HEADS), :],
                                      buf_ref.at[slot, pl.ds(i * FOX_HEADS, FOX_HEADS), :], sem_ref.at[slot])
                for i in range(n_pages)]

    slot = b % 2

    @pl.when(b == 0)
    def _():
        for cp in copies(b, slot):
            cp.start()

    @pl.when(b + 1 < nb)
    def _():
        for cp in copies(b + 1, 1 - slot):
            cp.start()

    for cp in copies(b, slot):
        cp.wait()
    lp = jnp.concatenate([buf_ref[slot], new_ref[0]], axis=0)
    cs = _dot3(lp, u_ref[...])
    tot = _dot3(cs, s_ref[...])
    c_ref[0] = _dot3_left(perm_ref[...], cs) + _dot3_left(tri_ref[...], tot)


def _logf_decode(pt_flat, cache2d, new_rows, *, batch, n_pages):
    nr = (n_pages + 2) * FOX_HEADS
    r = np.arange(PAGE)
    u = (r[:, None] <= r[None, :]).astype(np.float32)
    last = np.zeros((PAGE, PAGE), np.float32)
    last[PAGE - 1, :] = 1.0
    rr = np.arange(nr)
    tri = ((rr[:, None] % FOX_HEADS == rr[None, :] % FOX_HEADS)
           & (rr[None, :] // FOX_HEADS < rr[:, None] // FOX_HEADS)).astype(np.float32)
    perm = np.zeros((nr, nr), np.float32)
    perm[(rr % FOX_HEADS) * (n_pages + 2) + rr // FOX_HEADS, rr] = 1.0
    consts = [jnp.asarray(u, BF16), jnp.asarray(last, BF16), jnp.asarray(perm, BF16), jnp.asarray(perm @ tri, BF16)]
    grid_spec = pltpu.PrefetchScalarGridSpec(
        num_scalar_prefetch=1, grid=(batch,),
        in_specs=[pl.BlockSpec(memory_space=pl.ANY),
                  pl.BlockSpec((1, 2 * FOX_HEADS, PAGE), lambda b, pt: (b, 0, 0))]
                 + [pl.BlockSpec(c.shape, lambda b, pt: (0, 0)) for c in consts],
        out_specs=pl.BlockSpec((1, nr, PAGE), lambda b, pt: (b, 0, 0)),
        scratch_shapes=[pltpu.VMEM((2, n_pages * FOX_HEADS, PAGE), F32), pltpu.SemaphoreType.DMA((2,))])
    return pl.pallas_call(
        functools.partial(_logf_decode_kernel, n_pages=n_pages),
        grid_spec=grid_spec, out_shape=jax.ShapeDtypeStruct((batch, nr, PAGE), F32),
        compiler_params=_cparams(("arbitrary",)), name="fox_logf_decode",
    )(pt_flat, cache2d, new_rows, *consts)


def _prep_layer_weights(w_in_t, pool_w, pool_scale, w_out_a, w_out_b, w_out_c, w_o):
    d = w_in_t.shape[1]
    tok = jnp.concatenate([w_in_t[0:256], w_in_t[512:1024], w_in_t[2328:2584]], axis=0)
    gate = jnp.concatenate([w_in_t[256:512], w_in_t[1816:2328], w_in_t[3100:6428], w_in_t[1792:1816],
                            jnp.zeros((LANES - 24, d), F32)], axis=0)
    feat = jnp.concatenate([w_in_t[1024:1792], w_in_t[2584:3100], jnp.zeros((8 - FOX_HEADS, d), F32)],
                           axis=0).astype(BF16)
    gd = pool_w.shape[-1]
    bd = jnp.zeros((POOL_WIDTH, POOL_WIDTH), F32)
    for g in range(pool_w.shape[0]):
        bd = bd.at[g * gd:(g + 1) * gd, g * gd:(g + 1) * gd].set(pool_w[g])
    return dict(w_tok=_wprep(tok), w_gate=_wprep(gate), w_feat=feat, pool_bd=bd.astype(BF16),
                pool_scale=pool_scale.reshape(1, -1),
                w_out_a=w_out_a.astype(BF16), w_out_b=w_out_b.astype(BF16), w_out_c=w_out_c.astype(BF16),
                w_o=w_o.astype(BF16))


def _to_token_last(a_t, lead):
    b, _, t = a_t.shape
    nd = len(lead)
    return a_t.reshape((b,) + lead + (t,)).transpose((0, nd + 1) + tuple(range(1, nd + 1)))


def _layer_prompt(x2d, lw, norm_w, fox_bf, final_norm, batch, seq):
    pos = jnp.arange(seq)
    (uz, q8, nsat, wint, foxt, logft, fq, kcm, ksel, vsel, kwin, vwin, fk, fv) = _project(
        x2d, norm_w, lw["w_tok"], lw["w_feat"], fox_bf, pos, prompt=True, batch=batch, seq=seq)
    d = _pool_prompt(uz, seq)
    bpt = TKV // CMP_BLOCK
    nb = seq // CMP_BLOCK
    means = kcm[..., :bpt].transpose(0, 1, 3, 2).reshape(batch, nb, 256)
    means = jnp.concatenate([means[:, 0::2], means[:, 1::2]], axis=1)
    dup = lambda m: jnp.stack([jnp.concatenate([m[..., g * HEAD_DIM:(g + 1) * HEAD_DIM]] * 2, axis=-1)
                               for g in range(2)], axis=1).astype(BF16)
    ocmp, qaug = _cmp_prompt(dup(means[..., 0:LANES]), dup(means[..., LANES:2 * LANES]), q8, batch, seq)
    osel = _flash(qaug, ksel, vsel, n_sub=2, n_stack=NSA_REP, tpt=LANES, name="nsa_sel_prompt")
    owin = _win_prompt(q8, kwin, vwin, batch, seq)
    tpt = 512
    oc = _flash(fq.reshape(batch, FOX_HEADS, seq // tpt, tpt, LANES), fk, fv,
                n_sub=2, n_stack=1, tpt=tpt, name="fox_prompt")
    n = batch * seq
    xo, y = _output(x2d, norm_w, d, ocmp.reshape(n, 512), osel.reshape(n, 512), owin.reshape(n, 512),
                    oc.reshape(n, 256), lw, final_norm)
    wl = min(WINDOW, seq)
    states = (_to_token_last(nsat, (4, NSA_KV_HEADS, HEAD_DIM)),
              _to_token_last(foxt, (2, FOX_HEADS, HEAD_DIM)),
              logft[:, :FOX_HEADS].transpose(0, 2, 1),
              _to_token_last(wint[:, :, seq - wl:], (2, NSA_KV_HEADS, HEAD_DIM)),
              uz[:, :POOL_WIDTH].reshape(batch, seq, POOL_WIDTH)[:, -POOL_STATE:])
    return xo, y, states


def _new_page(a_t, batch, tdec):
    f = a_t.shape[0]
    a = a_t.reshape(f, batch, tdec).transpose(1, 0, 2)
    return jnp.concatenate([a, jnp.zeros((batch, f, PAGE - tdec), a.dtype)], axis=2)


def _layer_decode(x2d, lw, norm_w, fox_bf, final_norm, pt_flat, nsa_cache, fox_cache, logf_cache,
                  state_win_t, state_pool_tb, batch, tdec, n_pages):
    past = n_pages * PAGE
    n = batch * tdec
    pos = past + jnp.tile(jnp.arange(tdec), batch)
    (uz, q8, nsat, wint, foxt, logft, qc) = _project(
        x2d, norm_w, lw["w_tok"], lw["w_feat"], fox_bf, pos, prompt=False, batch=batch, seq=tdec)
    nsat, wint, foxt, logft = nsat[0], wint[0], foxt[0], logft[0]

    u_tb = uz[:, :POOL_WIDTH].reshape(batch, tdec, POOL_WIDTH).transpose(1, 0, 2)
    pool_seq = jnp.concatenate([state_pool_tb, u_tb], axis=0)
    d = _pool_decode(pool_seq, past, tdec).transpose(1, 0, 2).reshape(n, POOL_WIDTH)

    q5 = q8[0].reshape(NSA_KV_HEADS, NSA_REP, batch, tdec, LANES)
    qs = q5.transpose(2, 0, 1, 3, 4)
    qs = jnp.concatenate([qs, jnp.zeros((batch, 2, NSA_REP, 8 - tdec, LANES), BF16)], axis=3)
    qs = qs.reshape(batch, 2, NSA_REP * 8, LANES)
    qsw = jnp.stack([qs[:, 0], jnp.roll(qs[:, 1], HALF, axis=-1)], axis=1)
    ocmp_k, val = _cmp_decode(pt_flat, nsa_cache, qsw, batch=batch, n_pages=n_pages, past=past)
    selm = _topk_decode(val.reshape(batch * 16, -1), min(SEL_TOPK, past // SEL_BLOCK + 1)).reshape(val.shape)

    def unstack_groups(o):
        o = o.reshape(batch, 2, NSA_REP, 8, 2, HEAD_DIM)[:, :, :, :tdec]
        o = jnp.stack([o[:, 0, :, :, 0], o[:, 1, :, :, 1]], axis=1)
        return o.transpose(0, 3, 1, 2, 4).reshape(n, NSA_HEADS * HEAD_DIM)

    ocmp = unstack_groups(ocmp_k)
    owin_k, win_next_t = _win_decode(qsw, state_win_t, _new_page(wint, batch, tdec), batch, tdec)
    owin = unstack_groups(owin_k)

    nsd = past // SEL_BLOCK
    q6 = q5[..., :HEAD_DIM].transpose(2, 1, 0, 3, 4)
    q6 = jnp.concatenate([q6, jnp.zeros((batch, NSA_REP, 2, 8 - tdec, HEAD_DIM), BF16)], axis=3)
    eye2 = jnp.eye(2, dtype=BF16)
    qbd = q6[:, :, :, :, None, :] * eye2[None, None, :, None, :, None]
    qbd_t = qbd.reshape(batch, 64, LANES).transpose(0, 2, 1)
    qbd_t = jnp.concatenate([qbd_t, jnp.zeros((batch, LANES, LANES - 64), BF16)], axis=2)
    sel_t = jnp.tile(selm.reshape(batch, 16, nsd + LANES).transpose(0, 2, 1), (1, 1, NSA_REP))
    sel_t = jnp.concatenate([sel_t, jnp.zeros((batch, nsd + LANES, LANES - 64), F32)], axis=2)
    groups = lambda a, ng: jnp.stack([jnp.roll(a, k * (LANES // ng), axis=-1) for k in range(ng)], axis=1)
    osel_k = _attn_decode(pt_flat, nsa_cache, groups(qbd_t, 2), groups(sel_t[:, :nsd], 2), sel_t[:, nsd:nsd + 8],
                          jnp.zeros((2, 32, LANES), BF16), jnp.zeros((batch, 2, 8, LANES), F32),
                          _new_page(nsat[256:512], batch, tdec),
                          batch=batch, n_pages=n_pages, krow=256, vrow=384, kw=LANES, ncols=64, fox=False,
                          name="nsa_sel_decode")
    os_ = osel_k[:, :, :64].reshape(batch, 2, HEAD_DIM, NSA_REP, 2, 8)[..., :tdec]
    osel = jnp.stack([os_[:, 0, :, :, 0], os_[:, 1, :, :, 1]], axis=1)
    osel = osel.transpose(0, 4, 1, 3, 2).reshape(n, 512)

    lnew = _new_page(logft, batch, tdec)
    lnew = jnp.concatenate([lnew[:, :FOX_HEADS], jnp.zeros((batch, FOX_HEADS, PAGE), F32)], axis=1)
    csum = _logf_decode(pt_flat, logf_cache, lnew, batch=batch, n_pages=n_pages)
    c4 = csum.reshape(batch, FOX_HEADS, (n_pages + 2) * PAGE)
    ck8 = jnp.concatenate([-c4, jnp.zeros_like(c4)], axis=1)
    cq = jnp.concatenate([c4[:, :, past:past + tdec].transpose(0, 2, 1),
                          jnp.zeros((batch, tdec, 8 - FOX_HEADS), F32)], axis=2).reshape(batch, 1, tdec * 8)
    cq = jnp.concatenate([cq, jnp.zeros((batch, 7, tdec * 8), F32)], axis=1)
    cq = jnp.concatenate([cq, jnp.zeros((batch, 8, LANES - tdec * 8), F32)], axis=2)
    qf = (qc * SCALE).reshape(batch, tdec, FOX_HEADS, HEAD_DIM)
    eye = jnp.eye(8, FOX_HEADS, dtype=F32)
    qfbd = (qf[:, :, None, :, :] * eye[None, None, :, :, None]).reshape(batch, tdec * 8, FOX_HEADS * HEAD_DIM)
    qf_t = jnp.concatenate([qfbd.transpose(0, 2, 1), jnp.zeros((batch, 256, LANES - tdec * 8), F32)],
                           axis=2).astype(BF16)
    col = np.arange(LANES)
    head_cols = np.zeros((32, LANES), np.float32)
    for p in range(3):
        head_cols[p * 8 + col[:tdec * 8] % 8, col[:tdec * 8]] = 1.0
    fng = LANES // (tdec * 8)
    fppc = min(32, n_pages // fng)
    ck_chunks = ck8[:, :, :past].reshape(batch, 8, n_pages // fppc, fppc * PAGE).transpose(0, 2, 1, 3)
    oc_k = _attn_decode(pt_flat, fox_cache, groups(qf_t, fng), ck_chunks, ck8[:, :, past:past + PAGE],
                        groups(jnp.asarray(head_cols, BF16)[None], fng)[0], groups(cq, fng),
                        _new_page(foxt, batch, tdec),
                        batch=batch, n_pages=n_pages, krow=0, vrow=256, kw=256, ncols=tdec * 8, fox=True,
                        name="fox_decode")
    o6 = oc_k[:, :, :tdec * 8].reshape(batch, FOX_HEADS, HEAD_DIM, tdec, 8)
    oc = jnp.stack([o6[:, h, :, :, h] for h in range(FOX_HEADS)], axis=1)
    oc = oc.transpose(0, 3, 1, 2).reshape(n, 256)

    xo, y = _output(x2d, norm_w, d, ocmp, osel, owin, oc, lw, final_norm)
    states = (nsat.T.reshape(batch, tdec, 4, NSA_KV_HEADS, HEAD_DIM),
              foxt.T.reshape(batch, tdec, 2, FOX_HEADS, HEAD_DIM),
              logft[:FOX_HEADS].T.reshape(batch, tdec, FOX_HEADS),
              _to_token_last(win_next_t, (2, NSA_KV_HEADS, HEAD_DIM)),
              pool_seq[-POOL_STATE:].transpose(1, 0, 2))
    return xo, y, states


def kernel(x_prompt, x_sample, cache_nsa_kv, cache_fox_kv, cache_fox_logf, state_win_kv, state_pool, page_table,
           norm_w, w_in, pool_w, pool_scale, fox_bf, w_out_a, w_out_b, w_out_c, w_o, final_norm):
    depth = w_in.shape[0]
    batch, seq, dm = x_prompt.shape
    dbatch, tdec, _ = x_sample.shape
    n_phys, page = cache_nsa_kv.shape[1], cache_nsa_kv.shape[2]
    n_pages = page_table.shape[1]
    assert page == PAGE and tdec == 4 and n_pages % min(32, n_pages) == 0
    assert state_win_kv.shape[2] == WINDOW and n_pages * PAGE >= WINDOW

    nsa_cache = cache_nsa_kv.transpose(0, 1, 3, 4, 5, 2).reshape(depth * n_phys, 512, PAGE)
    fox_cache = cache_fox_kv.transpose(0, 1, 3, 4, 5, 2).reshape(depth * n_phys, 512, PAGE)
    logf_cache = cache_fox_logf.transpose(0, 1, 3, 2).reshape(depth * n_phys * FOX_HEADS, PAGE)
    win_state_t = state_win_kv.transpose(0, 1, 3, 4, 5, 2).reshape(depth, dbatch, 256, WINDOW)
    pool_state_tb = state_pool.transpose(0, 2, 1, 3)
    w_in_t = w_in.transpose(0, 2, 1)

    hp = x_prompt.reshape(batch * seq, dm)
    hs = x_sample.reshape(dbatch * tdec, dm)
    st_p, st_s = [], []
    for layer in range(depth):
        lw = _prep_layer_weights(w_in_t[layer], pool_w[layer], pool_scale[layer], w_out_a[layer],
                                 w_out_b[layer], w_out_c[layer], w_o[layer])
        hp, yp, sp = _layer_prompt(hp, lw, norm_w[layer], fox_bf[layer], final_norm, batch, seq)
        pt_flat = (page_table + layer * n_phys).reshape(-1).astype(jnp.int32)
        hs, ys, ss = _layer_decode(hs, lw, norm_w[layer], fox_bf[layer], final_norm, pt_flat, nsa_cache, fox_cache,
                                   logf_cache, win_state_t[layer], pool_state_tb[layer], dbatch, tdec, n_pages)
        st_p.append(sp)
        st_s.append(ss)
    outs = [yp.reshape(batch, seq, dm), ys.reshape(dbatch, tdec, dm)]
    for k in range(5):
        outs.append(jnp.stack([s[k] for s in st_p], 0))
        outs.append(jnp.stack([s[k] for s in st_s], 0))
    return tuple(outs)
```

```python
import functools

import numpy as np
import jax
import jax.numpy as jnp
from jax import lax
from jax.experimental import pallas as pl
from jax.experimental.pallas import tpu as pltpu

F32 = jnp.float32
BF16 = jnp.bfloat16

HEAD_DIM = 64
POOL_WIDTH = 256
POOL_WINDOWS = (2, 4, 8, 16)
POOL_STATE = 15
NSA_HEADS = 8
NSA_KV_HEADS = 2
NSA_REP = NSA_HEADS // NSA_KV_HEADS
CMP_BLOCK = 32
SEL_BLOCK = 64
SEL_TOPK = 16
WINDOW = 512
FORCE_SCORE = 1e4
FOX_HEADS = 4
ROPE_THETA = 10000.0
RMS_EPS = 1e-6
NEG_INF = -1e30
SCALE = HEAD_DIM ** -0.5

LANES = 128
HALF = LANES // 2
PAGE = 128
VMEM_LIMIT = 56 * 1024 * 1024

C_U, C_QB, C_QC, C_END = 0, 256, 768, 1024
G_ZA, G_ZB, G_ZC, G_GM, G_SM, G_END = 0, 256, 768, 1024, 4096, 4224
R_KV, R_KC, R_VC, R_FC, R_END = 0, 768, 1024, 1280, 1288
SM_GB = 0
TKV = 256
FT = 1024
ATTN_LOOK = 2
SEL_TPT = 128


def _cparams(sem):
    return pltpu.CompilerParams(dimension_semantics=sem, vmem_limit_bytes=VMEM_LIMIT)


def _lane(shape):
    return lax.broadcasted_iota(jnp.int32, shape, len(shape) - 1)


def _row(shape):
    return lax.broadcasted_iota(jnp.int32, shape, len(shape) - 2)


def _split3(x):
    hi = x.astype(BF16)
    r1 = x - hi.astype(F32)
    mid = r1.astype(BF16)
    lo = (r1 - mid.astype(F32)).astype(BF16)
    return hi, mid, lo


def _dot(a, b):
    return jnp.dot(a, b, preferred_element_type=F32)


def _dot_nt(a, b):
    return lax.dot_general(a, b, (((1,), (1,)), ((), ())), preferred_element_type=F32)


def _dot_tn(a, b):
    return lax.dot_general(a, b, (((0,), (0,)), ((), ())), preferred_element_type=F32)


def _dot3(x, m):
    n = x.shape[0]
    r = _dot(jnp.concatenate(_split3(x), axis=0), m)
    return r[0:n] + r[n:2 * n] + r[2 * n:3 * n]


def _dot2(x, m):
    n = x.shape[0]
    hi = x.astype(BF16)
    lo = (x - hi.astype(F32)).astype(BF16)
    r = _dot(jnp.concatenate([hi, lo], axis=0), m)
    return r[0:n] + r[n:2 * n]


def _dot3_left(m, x):
    n = x.shape[1]
    r = _dot(m, jnp.concatenate(_split3(x), axis=1))
    return r[:, 0:n] + r[:, n:2 * n] + r[:, 2 * n:3 * n]


def _swap_halves(x):
    return pltpu.roll(x, HALF, x.ndim - 1)


def _to_low_half(x, g, fill):
    lo = _lane(x.shape) < HALF
    return jnp.where(lo, x if g == 0 else _swap_halves(x), fill)


def _pair_heads(pieces):
    out = []
    for j in range(0, len(pieces), 2):
        lo = _lane(pieces[j].shape) < HALF
        out.append(jnp.where(lo, pieces[j], pieces[j + 1]))
    return out


def _masked_softmax(s, ok, axis):
    s = jnp.where(ok, s, NEG_INF)
    m = jnp.max(s, axis=axis, keepdims=True)
    e = jnp.where(ok, jnp.exp(s - m), 0.0)
    return e * (1.0 / jnp.maximum(jnp.sum(e, axis=axis, keepdims=True), 1e-30))


def _wprep_kernel(wt_ref, w_ref):
    w_ref[...] = wt_ref[...].T.astype(BF16)


def _wprep(wt):
    cols, d = wt.shape
    return pl.pallas_call(
        _wprep_kernel, grid=(cols // LANES,),
        in_specs=[pl.BlockSpec((LANES, d), lambda i: (i, 0))],
        out_specs=pl.BlockSpec((d, LANES), lambda i: (0, i)),
        out_shape=jax.ShapeDtypeStruct((d, cols), BF16),
        compiler_params=_cparams(("arbitrary",)), name="w_in_transpose",
    )(wt)


def _rope_tile(x, cosf, sinf):
    lane = _lane(x.shape)
    first = (lane % HEAD_DIM) < (HEAD_DIM // 2)
    rot = jnp.where(first, pltpu.roll(x, LANES - HEAD_DIM // 2, 1), pltpu.roll(x, HEAD_DIM // 2, 1))
    return x * cosf + rot * sinf


def _rope_rows(x, cos_t, sin_t):
    half = HEAD_DIM // 2
    out = []
    for h in range(x.shape[0] // HEAD_DIM):
        x1 = x[h * HEAD_DIM:h * HEAD_DIM + half]
        x2 = x[h * HEAD_DIM + half:(h + 1) * HEAD_DIM]
        out += [x1 * cos_t - x2 * sin_t, x2 * cos_t + x1 * sin_t]
    return jnp.concatenate(out, axis=0)


def _proj_kernel(*refs, prompt, tiles_per_seq):
    if prompt:
        (x_ref, nw_ref, wt_ref, wk_ref, cos_ref, sin_ref, cost_ref, sint_ref, bfc_ref,
         ohb_ref, amean_ref, triu_ref, sq_ref,
         u_ref, q8_ref, nsat_ref, wint_ref, foxt_ref, logft_ref,
         fq_ref, kcm_ref, ksel_ref, vsel_ref, kwin_ref, vwin_ref, fk_ref, fv_ref, carry_ref) = refs
    else:
        (x_ref, nw_ref, wt_ref, wk_ref, cos_ref, sin_ref, cost_ref, sint_ref, bfc_ref,
         u_ref, q8_ref, nsat_ref, wint_ref, foxt_ref, logft_ref, qc_ref) = refs

    x = x_ref[...]
    ms = jnp.mean(x * x, axis=-1, keepdims=True)
    hb = ((x * lax.rsqrt(ms + RMS_EPS)) * nw_ref[...]).astype(BF16)
    tm = x.shape[0]

    def seg(c0, n):
        return _dot(hb, wt_ref[:, c0:c0 + n])

    def seg_t(r0, n):
        return _dot_nt(wk_ref[r0:r0 + n, :], hb)

    u_ref[...] = seg(C_U, POOL_WIDTH)
    cosf = cos_ref[...]
    sinf = sin_ref[...]
    for c in range(4):
        y = _rope_tile(seg(C_QB + c * LANES, LANES), cosf, sinf) * SCALE
        q8_ref[0, 2 * c] = _to_low_half(y, 0, 0.0).astype(BF16)
        q8_ref[0, 2 * c + 1] = _to_low_half(y, 1, 0.0).astype(BF16)
    qc = seg(C_QC, 256)

    cos_t = cost_ref[...]
    sin_t = sint_ref[...]
    kcmp = _rope_rows(seg_t(R_KV, LANES), cos_t, sin_t)
    vcmp = seg_t(R_KV + 128, LANES)
    ksel = _rope_rows(seg_t(R_KV + 256, LANES), cos_t, sin_t)
    vsel = seg_t(R_KV + 384, LANES)
    kwin = _rope_rows(seg_t(R_KV + 512, LANES), cos_t, sin_t)
    vwin = seg_t(R_KV + 640, LANES)
    nsat = nsat_ref.at[0]
    foxt = foxt_ref.at[0]
    nsat[0:128] = kcmp
    nsat[128:256] = vcmp
    nsat[256:384] = ksel
    nsat[384:512] = vsel
    wint_ref[0, 0:128] = kwin
    wint_ref[0, 128:256] = vwin
    kc = seg_t(R_KC, 256)
    vc = seg_t(R_VC, 256)
    foxt[0:256] = kc
    foxt[256:512] = vc
    f_t = seg_t(R_FC, 8)
    logf = jnp.where(_row(f_t.shape) < FOX_HEADS, jax.nn.log_sigmoid(f_t + bfc_ref[...]), 0.0)
    logft_ref[0] = logf

    if not prompt:
        qc_ref[...] = qc
        return

    kcm_ref[0, 0] = _dot3(jnp.concatenate([kcmp, vcmp], axis=0), amean_ref[...])
    ohb = ohb_ref[...]
    ones_rows = (_row((HALF, tm)) == 0).astype(F32)
    for g in range(2):
        gs = slice(g * HEAD_DIM, (g + 1) * HEAD_DIM)
        ksel_ref[0, g, 0] = jnp.concatenate([ksel[gs].astype(BF16), ohb], axis=0)
        vsel_ref[0, g, 0] = jnp.concatenate([vsel[gs], ones_rows], axis=0).astype(BF16)
        kwin_ref[0, g, 0] = jnp.concatenate([kwin[gs]] * 2, axis=0).astype(BF16)
        vwin_ref[0, g, 0] = jnp.concatenate([vwin[gs]] * 2, axis=0).astype(BF16)

    i = pl.program_id(0)
    carry = jnp.where(i % tiles_per_seq == 0, 0.0, carry_ref[:, 0:1])
    csum = _dot3(logf, triu_ref[...]) + carry
    carry_ref[...] = jnp.broadcast_to(csum[:, tm - 1:tm], carry_ref.shape)
    c_hi, c_mid, c_lo = [p.astype(F32) for p in _split3(csum)]
    r8 = _row((8, tm))
    c_tok = jnp.concatenate([csum, jnp.zeros((LANES - 8, tm), F32)], axis=0).T
    cq = _dot(jnp.concatenate(_split3(c_tok), axis=1), sq_ref[...])
    lane = _lane((tm, LANES))
    one_q = ((lane >= HALF + 3) & (lane < HALF + 6)).astype(F32)
    for h in range(FOX_HEADS):
        t, g = h // 2, h % 2
        hs = slice(h * HEAD_DIM, (h + 1) * HEAD_DIM)
        aug = jnp.where(r8 < 3, 1.0, jnp.where(r8 == 3, -c_hi[h:h + 1], jnp.where(
            r8 == 4, -c_mid[h:h + 1], jnp.where(r8 == 5, -c_lo[h:h + 1], 0.0))))
        fk_ref[0, h, 0] = jnp.concatenate([kc[hs], aug, jnp.zeros((HALF - 8, tm), F32)], axis=0).astype(BF16)
        fv_ref[0, h, 0] = jnp.concatenate([vc[hs], ones_rows], axis=0).astype(BF16)
        fq_ref[0, h] = _to_low_half(qc[:, t * LANES:(t + 1) * LANES] * SCALE, g,
                                    cq[:, h * LANES:(h + 1) * LANES] + one_q).astype(BF16)


def _rope_tables(pos):
    half = HEAD_DIM // 2
    inv = ROPE_THETA ** (-jnp.arange(half, dtype=F32) / half)
    ang = pos.astype(F32)[:, None] * inv[None, :]
    cos, sin = jnp.cos(ang), jnp.sin(ang)
    return jnp.tile(cos, (1, 4)), jnp.tile(jnp.concatenate([-sin, sin], 1), (1, 2)), cos.T, sin.T


def _fox_query_aug_matrix():
    sq = np.zeros((3 * LANES, FOX_HEADS * LANES), np.float32)
    for h in range(FOX_HEADS):
        for p in range(3):
            sq[p * LANES + h, h * LANES + HALF + p] = 1.0
    return jnp.asarray(sq, BF16)


def _project(x2d, norm_w, w_tok, w_feat, fox_bf, pos, *, prompt, batch, seq):
    n, d = x2d.shape
    tm = TKV if prompt else n
    tiles_per_seq = seq // tm if prompt else 1
    cosf, sinf, cos_t, sin_t = _rope_tables(pos)
    bf_col = jnp.zeros((8, 1), F32).at[0:FOX_HEADS, 0].set(fox_bf)
    row = lambda w: pl.BlockSpec((tm, w), lambda i: (i, 0))
    full = lambda a: pl.BlockSpec(a.shape, lambda i: (0,) * a.ndim)
    if prompt:
        tab = pl.BlockSpec((tm, LANES), lambda i: (i % tiles_per_seq, 0))
        tab_t = pl.BlockSpec((HEAD_DIM // 2, tm), lambda i: (0, i % tiles_per_seq))
    else:
        tab = row(LANES)
        tab_t = pl.BlockSpec((HEAD_DIM // 2, tm), lambda i: (0, i))
    in_arrays = [x2d, norm_w.reshape(1, d), w_tok, w_feat, cosf, sinf, cos_t, sin_t, bf_col]
    in_specs = [row(d), full(in_arrays[1]), full(w_tok), full(w_feat), tab, tab, tab_t, tab_t, full(bf_col)]
    nb = batch if prompt else 1
    bidx = lambda i: (i // tiles_per_seq, 0, i % tiles_per_seq)
    feat = lambda r: pl.BlockSpec((1, r, tm), bidx)
    hb = lambda nh: pl.BlockSpec((1, nh, tm, LANES), lambda i: (i // tiles_per_seq, 0, i % tiles_per_seq, 0))
    nt = (seq if prompt else n)
    out_shapes = [
        jax.ShapeDtypeStruct((n, POOL_WIDTH), F32), jax.ShapeDtypeStruct((nb, NSA_HEADS, nt, LANES), BF16),
        jax.ShapeDtypeStruct((nb, 512, nt), F32), jax.ShapeDtypeStruct((nb, 256, nt), F32),
        jax.ShapeDtypeStruct((nb, 512, nt), F32), jax.ShapeDtypeStruct((nb, 8, nt), F32),
    ]
    out_specs = [row(POOL_WIDTH), hb(NSA_HEADS), feat(512), feat(256), feat(512), feat(8)]
    scratch = []
    if prompt:
        t = np.arange(seq)
        ohb = np.zeros((HALF, seq), np.float32)
        ohb[t // SEL_BLOCK, t] = -NEG_INF
        tt = np.arange(tm)
        amean = np.zeros((tm, LANES), np.float32)
        amean[tt, tt // CMP_BLOCK] = 1.0 / CMP_BLOCK
        triu = np.triu(np.ones((tm, tm), np.float32))
        extra = [jnp.asarray(ohb, BF16), jnp.asarray(amean, BF16), jnp.asarray(triu, BF16), _fox_query_aug_matrix()]
        in_arrays += extra
        in_specs += [pl.BlockSpec((HALF, tm), lambda i: (0, i % tiles_per_seq)),
                     full(extra[1]), full(extra[2]), full(extra[3])]
        nkt = seq // tm
        kvt = lambda ns: pl.BlockSpec((1, ns, 1, LANES, tm),
                                      lambda i: (i // tiles_per_seq, 0, i % tiles_per_seq, 0, 0))
        out_shapes.append(jax.ShapeDtypeStruct((batch, FOX_HEADS, seq, LANES), BF16))
        out_specs.append(hb(FOX_HEADS))
        out_shapes.append(jax.ShapeDtypeStruct((batch, nkt, 256, LANES), F32))
        out_specs.append(pl.BlockSpec((1, 1, 256, LANES), lambda i: (i // tiles_per_seq, i % tiles_per_seq, 0, 0)))
        tpf = FT // tm
        kvf = lambda ns: pl.BlockSpec(
            (1, ns, 1, LANES, tm),
            lambda i: (i // tiles_per_seq, 0, (i % tiles_per_seq) // tpf, 0, (i % tiles_per_seq) % tpf))
        for ns, flash in ((2, True), (2, True), (2, False), (2, False), (FOX_HEADS, True), (FOX_HEADS, True)):
            if flash:
                out_shapes.append(jax.ShapeDtypeStruct((batch, ns, seq // FT, LANES, FT), BF16))
                out_specs.append(kvf(ns))
            else:
                out_shapes.append(jax.ShapeDtypeStruct((batch, ns, nkt, LANES, tm), BF16))
                out_specs.append(kvt(ns))
        scratch = [pltpu.VMEM((8, LANES), F32)]
    else:
        out_shapes.append(jax.ShapeDtypeStruct((n, 256), F32))
        out_specs.append(row(256))
    return pl.pallas_call(
        functools.partial(_proj_kernel, prompt=prompt, tiles_per_seq=tiles_per_seq),
        grid=(n // tm,), in_specs=in_specs, out_specs=out_specs, out_shape=out_shapes,
        scratch_shapes=scratch, compiler_params=_cparams(("arbitrary",)),
        name="proj_prompt" if prompt else "proj_decode",
    )(*in_arrays)


def _pool_mix(u, prev, pos):
    grp = _lane(u.shape) // (POOL_WIDTH // len(POOL_WINDOWS))
    acc = u
    win = None
    for j in range(1, POOL_WINDOWS[-1]):
        acc = acc + prev(j)
        if j + 1 in POOL_WINDOWS:
            k = POOL_WINDOWS.index(j + 1)
            win = acc if win is None else jnp.where(grp >= k, acc, win)
    w = jnp.zeros(u.shape, jnp.int32)
    for k, wk in enumerate(POOL_WINDOWS):
        w = jnp.where(grp == k, wk, w)
    cnt = jnp.minimum(pos + 1, w).astype(F32)
    return win / cnt - u


def _pool_prompt_kernel(u_ref, halo_ref, d_ref, s_ref, *, tiles_per_seq, tm):
    i = pl.program_id(0) % tiles_per_seq
    s_ref[0:16, :] = jnp.where(i == 0, 0.0, halo_ref[...])
    s_ref[16:, :] = u_ref[...]
    pos = i * tm + _row((tm, POOL_WIDTH))
    d_ref[...] = _pool_mix(u_ref[...], lambda j: s_ref[16 - j:16 - j + tm, :], pos)


def _pool_prompt(uz, seq):
    n = uz.shape[0]
    tm = 512
    tiles_per_seq = seq // tm
    return pl.pallas_call(
        functools.partial(_pool_prompt_kernel, tiles_per_seq=tiles_per_seq, tm=tm),
        grid=(n // tm,),
        in_specs=[pl.BlockSpec((tm, POOL_WIDTH), lambda i: (i, 0)),
                  pl.BlockSpec((16, POOL_WIDTH), lambda i: (jnp.maximum(i * (tm // 16) - 1, 0), 0))],
        out_specs=pl.BlockSpec((tm, POOL_WIDTH), lambda i: (i, 0)),
        out_shape=jax.ShapeDtypeStruct((n, POOL_WIDTH), F32),
        scratch_shapes=[pltpu.VMEM((tm + 16, POOL_WIDTH), F32)],
        compiler_params=_cparams(("arbitrary",)), name="pool_prompt",
    )(uz, uz)


def _pool_decode_kernel(seq_ref, d_ref, *, pos0, tdec):
    for t in range(tdec):
        pos = jnp.full(seq_ref.shape[1:], pos0 + t, jnp.int32)
        d_ref[t] = _pool_mix(seq_ref[POOL_STATE + t], lambda j: seq_ref[POOL_STATE + t - j], pos)


def _pool_decode(seq_tb, pos0, tdec):
    _, b, w = seq_tb.shape
    return pl.pallas_call(
        functools.partial(_pool_decode_kernel, pos0=pos0, tdec=tdec),
        grid=(1,),
        in_specs=[pl.BlockSpec(seq_tb.shape, lambda i: (0, 0, 0))],
        out_specs=pl.BlockSpec((tdec, b, w), lambda i: (0, 0, 0)),
        out_shape=jax.ShapeDtypeStruct((tdec, b, w), F32),
        compiler_params=_cparams(("arbitrary",)), name="pool_decode",
    )(seq_tb)


def _select_rank(imp, blk, cur, k_top):
    forced = (blk == 0) | (blk == cur) | (blk == cur - 1)
    val = jnp.where(blk <= cur, imp + jnp.where(forced, FORCE_SCORE, 0.0), -1.0)
    ns = val.shape[0]
    sel = jnp.zeros(val.shape, jnp.bool_)
    for _ in range(k_top):
        mx = jnp.max(val, axis=0, keepdims=True)
        idx = jnp.min(jnp.where(val == mx, blk, ns), axis=0, keepdims=True)
        pick = blk == idx
        sel = sel | pick
        val = jnp.where(pick, -3e38, val)
    return sel & (blk <= cur)


def _cmp_prompt_kernel(kc_ref, vc_ref, q_ref, o_ref, qaug_ref, *, tq, seq):
    qi = pl.program_id(1)
    ns = seq // SEL_BLOCK
    imps = []
    for g in range(NSA_KV_HEADS):
        qs = jnp.concatenate([q_ref[0, g * NSA_REP + r] for r in range(NSA_REP)], axis=0)
        s = _dot_nt(kc_ref[0, g], qs)
        row = _row(s.shape)
        nblk = jnp.where(row < ns, 2 * row, 2 * (row - ns) + 1)
        t = qi * tq + _lane(s.shape) % tq
        p = _masked_softmax(s, (nblk + 1) * CMP_BLOCK - 1 <= t, 0)
        o = _dot_tn(p.astype(BF16), vc_ref[0, g])
        pieces = _pair_heads([o[r * tq:(r + 1) * tq] for r in range(NSA_REP)])
        for j, piece in enumerate(pieces):
            o_ref[0, :, (2 * g + j) * LANES:(2 * g + j + 1) * LANES] = piece
        imp = p[:, 0:tq]
        for r in range(1, NSA_REP):
            imp = imp + p[:, r * tq:(r + 1) * tq]
        imps.append(imp[:ns] + imp[ns:])

    imp = jnp.concatenate(imps, axis=1)
    blk = _row(imp.shape)
    cur = (qi * tq + _lane(imp.shape) % tq) // SEL_BLOCK
    selm = jnp.where(_select_rank(imp, blk, cur, min(SEL_TOPK, ns)), 0.0, -1.0)
    for g in range(NSA_KV_HEADS):
        pad = [jnp.zeros((HALF, tq), F32), selm[:, g * tq:(g + 1) * tq]]
        if ns < HALF:
            pad.append(jnp.zeros((HALF - ns, tq), F32))
        sel_t = jnp.concatenate(pad, axis=0).T
        for h in range(tq // SEL_TPT):
            tok = slice(h * SEL_TPT, (h + 1) * SEL_TPT)
            for r in range(NSA_REP):
                qaug_ref[0, g, h, r * SEL_TPT:(r + 1) * SEL_TPT, :] = (
                    q_ref[0, g * NSA_REP + r, tok, :].astype(F32) + sel_t[tok]).astype(BF16)


def _cmp_prompt(kc_dup, vc_dup, q8, batch, seq):
    tq = 2 * SEL_TPT
    nq = seq // tq
    nb = seq // CMP_BLOCK
    assert seq % tq == 0 and seq // SEL_BLOCK <= HALF
    kv = pl.BlockSpec((1, NSA_KV_HEADS, nb, LANES), lambda b, q: (b, 0, 0, 0))
    return pl.pallas_call(
        functools.partial(_cmp_prompt_kernel, tq=tq, seq=seq),
        grid=(batch, nq),
        in_specs=[kv, kv, pl.BlockSpec((1, NSA_HEADS, tq, LANES), lambda b, q: (b, 0, q, 0))],
        out_specs=[pl.BlockSpec((1, tq, 512), lambda b, q: (b, q, 0)),
                   pl.BlockSpec((1, NSA_KV_HEADS, tq // SEL_TPT, NSA_REP * SEL_TPT, LANES),
                                lambda b, q: (b, 0, q, 0, 0))],
        out_shape=[jax.ShapeDtypeStruct((batch, seq, 512), F32),
                   jax.ShapeDtypeStruct((batch, NSA_KV_HEADS, seq // SEL_TPT, NSA_REP * SEL_TPT, LANES), BF16)],
        compiler_params=_cparams(("arbitrary", "arbitrary")), name="nsa_cmp_prompt",
    )(kc_dup, vc_dup, q8)


def _flash_kernel(q_ref, k_ref, v_ref, o_ref, *, n_sub, n_stack, tpt):
    qi = pl.program_id(2)
    q0 = qi * tpt
    nfull = q0 // FT
    rows = n_stack * tpt
    qs = [q_ref[0, u, 0] for u in range(n_sub)]

    def step(j, carry, masked):
        ss = [_dot(qs[u], k_ref[0, u, j]) for u in range(n_sub)]
        out = []
        for u in range(n_sub):
            m, acc = carry[u]
            s = ss[u]
            if masked:
                key = j * FT + _lane(s.shape)
                tpos = q0 + _row(s.shape) % tpt
                s = jnp.where(key <= tpos, s, NEG_INF)
            mn = jnp.maximum(m, jnp.max(s, axis=-1, keepdims=True))
            a = jnp.exp(m - mn)
            p = jnp.exp(s - mn)
            pv = _dot_nt(p.astype(BF16), v_ref[0, u, j])
            out.append((mn, a * acc + pv))
        return tuple(out)

    init = tuple((jnp.full((rows, 1), -1e38, F32), jnp.zeros((rows, LANES), F32)) for _ in range(n_sub))
    carry = lax.fori_loop(0, nfull, lambda j, c: step(j, c, False), init)
    pieces = []
    for _, acc in step(nfull, carry, True):
        o = acc * (1.0 / acc[:, HALF:HALF + 1])
        heads = [o[r * tpt:(r + 1) * tpt] for r in range(n_stack)]
        pieces += [h if (len(pieces) + i) % 2 == 0 else _swap_halves(h) for i, h in enumerate(heads)]
    for j, piece in enumerate(_pair_heads(pieces)):
        o_ref[0, :, j * LANES:(j + 1) * LANES] = piece


def _flash(q_aug, k_aug, v_dup, *, n_sub, n_stack, tpt, name):
    b, s, nq, rows, _ = q_aug.shape
    nkt = k_aug.shape[2]
    seq = nkt * FT
    assert FT % tpt == 0 and rows == n_stack * tpt
    wout = n_sub * n_stack * HEAD_DIM
    kv = pl.BlockSpec((1, n_sub, nkt, LANES, FT), lambda bb, ss, q: (bb, ss, 0, 0, 0))
    return pl.pallas_call(
        functools.partial(_flash_kernel, n_sub=n_sub, n_stack=n_stack, tpt=tpt),
        grid=(b, s // n_sub, nq),
        in_specs=[pl.BlockSpec((1, n_sub, 1, rows, LANES), lambda bb, ss, q: (bb, ss, q, 0, 0)), kv, kv],
        out_specs=pl.BlockSpec((1, tpt, wout), lambda bb, ss, q: (bb, q, ss)),
        out_shape=jax.ShapeDtypeStruct((b, seq, s * n_stack * HEAD_DIM), F32),
        compiler_params=_cparams(("arbitrary", "arbitrary", "arbitrary")), name=name,
    )(q_aug, k_aug, v_dup)


def _win_prompt_kernel(q_ref, k_ref, v_ref, o_ref, *, tq, nwt):
    qi = pl.program_id(2)
    q0 = qi * tq
    ts = jnp.maximum(qi - (nwt - 1), 0)
    qs = jnp.concatenate([q_ref[0, r] for r in range(NSA_REP)], axis=0)
    s = jnp.concatenate([_dot(qs, k_ref[0, 0, ts + i]) for i in range(nwt)], axis=1)
    key = ts * TKV + _lane(s.shape)
    t = q0 + _row(s.shape) % tq
    s = jnp.where((key <= t) & (key > t - WINDOW), s, NEG_INF)
    e = jnp.exp(s - jnp.max(s, axis=-1, keepdims=True))
    den = jnp.sum(e, axis=-1, keepdims=True)
    e = e.astype(BF16)
    o = _dot_nt(e[:, 0:TKV], v_ref[0, 0, ts])
    for i in range(1, nwt):
        o = o + _dot_nt(e[:, i * TKV:(i + 1) * TKV], v_ref[0, 0, ts + i])
    o = o * (1.0 / den)
    for j, piece in enumerate(_pair_heads([o[r * tq:(r + 1) * tq] for r in range(NSA_REP)])):
        o_ref[0, :, j * LANES:(j + 1) * LANES] = piece


def _win_prompt(q8, kwin, vwin, batch, seq):
    tq = TKV
    nkt = seq // TKV
    nwt = WINDOW // TKV + 1
    assert nkt >= nwt
    kv = pl.BlockSpec((1, 1, nkt, LANES, TKV), lambda b, g, q: (b, g, 0, 0, 0))
    return pl.pallas_call(
        functools.partial(_win_prompt_kernel, tq=tq, nwt=nwt),
        grid=(batch, NSA_KV_HEADS, seq // tq),
        in_specs=[pl.BlockSpec((1, NSA_REP, tq, LANES), lambda b, g, q: (b, g, q, 0)), kv, kv],
        out_specs=pl.BlockSpec((1, tq, 256), lambda b, g, q: (b, q, g)),
        out_shape=jax.ShapeDtypeStruct((batch, seq, 512), F32),
        compiler_params=_cparams(("arbitrary", "arbitrary", "arbitrary")), name="nsa_win_prompt",
    )(q8, kwin, vwin)


def _out_kernel(x_ref, nw_ref, d_ref, ocmp_ref, osel_ref, owin_ref, oc_ref,
                wg_ref, pw_ref, ps_ref, wa_ref, wb_ref, wc_ref, wo_ref, e3_ref, fn_ref, xo_ref, y_ref):
    x = x_ref[...]
    ms = jnp.mean(x * x, axis=-1, keepdims=True)
    hb = ((x * lax.rsqrt(ms + RMS_EPS)) * nw_ref[...]).astype(BF16)

    def gate(c0, n):
        return _dot(hb, wg_ref[:, c0:c0 + n])

    oa = _dot(d_ref[...].astype(BF16), pw_ref[...]) * ps_ref[...]
    ya = _dot((oa * jax.nn.silu(gate(G_ZA, 256))).astype(BF16), wa_ref[...])
    gx = _dot2(jax.nn.sigmoid(gate(G_SM, LANES)), e3_ref[...])
    ob = gx[:, 0:512] * ocmp_ref[...] + gx[:, 512:1024] * osel_ref[...] + gx[:, 1024:1536] * owin_ref[...]
    yb = _dot((ob * jax.nn.silu(gate(G_ZB, 512))).astype(BF16), wb_ref[...])
    yc = _dot((oc_ref[...] * jax.nn.silu(gate(G_ZC, 256))).astype(BF16), wc_ref[...])
    d = ya.shape[-1]
    mixed = (jax.nn.sigmoid(gate(G_GM, d)) * ya + jax.nn.sigmoid(gate(G_GM + d, d)) * yb
             + jax.nn.sigmoid(gate(G_GM + 2 * d, d)) * yc)
    xo = x + _dot(mixed.astype(BF16), wo_ref[...])
    xo_ref[...] = xo
    ms = jnp.mean(xo * xo, axis=-1, keepdims=True)
    y_ref[...] = (xo * lax.rsqrt(ms + RMS_EPS)) * fn_ref[...]


def _gate_expand_matrix():
    e = np.zeros((LANES, 3 * NSA_HEADS * HEAD_DIM), np.float32)
    for hd in range(NSA_HEADS):
        for c in range(3):
            e[SM_GB + hd * 3 + c, c * 512 + hd * HEAD_DIM:c * 512 + (hd + 1) * HEAD_DIM] = 1.0
    return jnp.asarray(e, BF16)


def _output(x2d, norm_w, d, ocmp, osel, owin, oc, lw, final_norm):
    n, dm = x2d.shape
    tm = min(512, n)
    row = lambda w: pl.BlockSpec((tm, w), lambda i: (i, 0))
    full = lambda a: pl.BlockSpec(a.shape, lambda i: (0,) * a.ndim)
    consts = [lw["w_gate"], lw["pool_bd"], lw["pool_scale"], lw["w_out_a"], lw["w_out_b"], lw["w_out_c"],
              lw["w_o"], _gate_expand_matrix(), final_norm.reshape(1, dm)]
    nw = norm_w.reshape(1, dm)
    return pl.pallas_call(
        _out_kernel, grid=(n // tm,),
        in_specs=[row(dm), full(nw), row(256), row(512), row(512), row(512), row(256)] + [full(c) for c in consts],
        out_specs=[row(dm), row(dm)],
        out_shape=[jax.ShapeDtypeStruct((n, dm), F32), jax.ShapeDtypeStruct((n, dm), F32)],
        compiler_params=_cparams(("arbitrary",)), name="out_proj",
    )(x2d, nw, d, ocmp, osel, owin, oc, *consts)


class _PageStream:
    def __init__(self, cache_ref, pt_ref, buf_ref, sem_ref, *, n_pages, nch, ppc, rows, nrows, look):
        self.refs = (cache_ref, pt_ref, buf_ref, sem_ref)
        self.n_pages, self.nch, self.ppc, self.rows, self.nrows, self.look = n_pages, nch, ppc, rows, nrows, look
        self.steps = len(rows) * nch
        self.nslot = look + 1
        self.g0 = pl.program_id(0) * self.steps
        self.total = pl.num_programs(0) * self.steps

    def _copies(self, g):
        cache_ref, pt_ref, buf_ref, sem_ref = self.refs
        f = g % self.steps
        row0 = self.rows[0]
        for p in range(1, len(self.rows)):
            row0 = jnp.where(f // self.nch == p, self.rows[p], row0)
        if len(self.rows) > 1:
            row0 = pl.multiple_of(row0, 8)
        base = (g // self.steps) * self.n_pages + (f % self.nch) * self.ppc
        slot = g % self.nslot
        return [pltpu.make_async_copy(cache_ref.at[pt_ref[base + i], pl.ds(row0, self.nrows), :],
                                      buf_ref.at[slot, :, pl.ds(i * PAGE, PAGE)], sem_ref.at[slot])
                for i in range(self.ppc)]

    def prime(self):
        @pl.when(pl.program_id(0) == 0)
        def _():
            for g in range(self.look):
                for cp in self._copies(g):
                    cp.start()

    def fetch(self, f):
        g = self.g0 + f

        @pl.when(g + self.look < self.total)
        def _():
            for cp in self._copies(g + self.look):
                cp.start()

        for cp in self._copies(g):
            cp.wait()
        return g % self.nslot


def _topk_extract(val, k_top):
    lane = _lane(val.shape)
    sel = jnp.zeros(val.shape, jnp.bool_)
    for _ in range(k_top):
        mx = jnp.max(val, axis=-1, keepdims=True)
        idx = jnp.min(jnp.where(val == mx, lane, val.shape[-1]), axis=-1, keepdims=True)
        pick = lane == idx
        sel = sel | pick
        val = jnp.where(pick, -3e38, val)
    return sel


def _topk_decode_kernel(val_ref, sel_ref, *, k_top):
    val = val_ref[...]
    sel = _topk_extract(val, k_top) & (val > -0.5)
    sel_ref[...] = jnp.where(sel, 0.0, -1.0)


def _topk_decode(val, k_top):
    return pl.pallas_call(
        functools.partial(_topk_decode_kernel, k_top=k_top), grid=(1,),
        in_specs=[pl.BlockSpec(val.shape, lambda i: (0, 0))],
        out_specs=pl.BlockSpec(val.shape, lambda i: (0, 0)),
        out_shape=jax.ShapeDtypeStruct(val.shape, F32),
        compiler_params=_cparams(("arbitrary",)), name="nsa_topk_decode",
    )(val)


def _cmp_decode_kernel(pt_ref, cache_ref, q_ref, amean_ref, pair_ref, o_ref, val_ref, buf_ref, sem_ref, kcs_ref,
                       *, n_pages, nch, ppc, past):
    stream = _PageStream(cache_ref, pt_ref, buf_ref, sem_ref, n_pages=n_pages, nch=nch, ppc=ppc,
                         rows=(0,), nrows=256, look=ATTN_LOOK)
    stream.prime()

    def chunk(c, carry):
        slot = stream.fetch(c)
        kcs_ref[c] = _dot2(buf_ref[slot], amean_ref[...])
        return carry

    lax.fori_loop(0, nch, chunk, 0)
    nb = n_pages * (PAGE // CMP_BLOCK)
    nsd = nb // 2
    bpc = ppc * (PAGE // CMP_BLOCK)
    means = jnp.concatenate([kcs_ref[cc][:, 0:bpc] for cc in range(nch)], axis=1).astype(BF16)
    for g in range(2):
        s = _dot(q_ref[0, g], means[0:LANES])
        nblk = _lane(s.shape)
        qpos = past + _row(s.shape) % 8
        p = _masked_softmax(s, (nblk + 1) * CMP_BLOCK - 1 <= qpos, -1)
        o_ref[0, g] = _dot_nt(p.astype(BF16), means[LANES:2 * LANES])
        imp = p[0:8]
        for r in range(1, NSA_REP):
            imp = imp + p[r * 8:(r + 1) * 8]
        imp = jnp.concatenate([_dot3(imp, pair_ref[...]), jnp.zeros((8, LANES), F32)], axis=1)
        blk = _lane(imp.shape)
        cur = (past + _row(imp.shape)) // SEL_BLOCK
        forced = (blk == 0) | (blk == cur) | (blk == cur - 1)
        val = jnp.where(blk <= cur, imp + jnp.where(forced, FORCE_SCORE, 0.0), -1.0)
        val_ref[0, g] = jnp.where(blk <= nsd, val, -3e38)


def _cmp_decode(pt_flat, cache, qsw, *, batch, n_pages, past):
    ppc = min(32, n_pages)
    nch = n_pages // ppc
    bpc = ppc * (PAGE // CMP_BLOCK)
    assert bpc % LANES == 0 or nch == 1
    nb = n_pages * (PAGE // CMP_BLOCK)
    nsd = nb // 2
    tok = np.arange(ppc * PAGE)
    amean = np.zeros((ppc * PAGE, LANES), np.float32)
    amean[tok, tok // CMP_BLOCK] = 1.0 / CMP_BLOCK
    pair = (np.arange(nb)[:, None] // 2 == np.arange(nsd)[None, :]).astype(np.float32)
    consts = [jnp.asarray(amean, BF16), jnp.asarray(pair, BF16)]
    grid_spec = pltpu.PrefetchScalarGridSpec(
        num_scalar_prefetch=1, grid=(batch,),
        in_specs=[pl.BlockSpec(memory_space=pl.ANY),
                  pl.BlockSpec((1, 2, 32, LANES), lambda b, pt: (b, 0, 0, 0))]
                 + [pl.BlockSpec(a.shape, lambda b, pt: (0, 0)) for a in consts],
        out_specs=[pl.BlockSpec((1, 2, 32, LANES), lambda b, pt: (b, 0, 0, 0)),
                   pl.BlockSpec((1, 2, 8, nsd + LANES), lambda b, pt: (b, 0, 0, 0))],
        scratch_shapes=[pltpu.VMEM((ATTN_LOOK + 1, 256, ppc * PAGE), F32), pltpu.SemaphoreType.DMA((ATTN_LOOK + 1,)),
                        pltpu.VMEM((nch, 256, LANES), F32)])
    return pl.pallas_call(
        functools.partial(_cmp_decode_kernel, n_pages=n_pages, nch=nch, ppc=ppc, past=past),
        grid_spec=grid_spec,
        out_shape=[jax.ShapeDtypeStruct((batch, 2, 32, LANES), F32),
                   jax.ShapeDtypeStruct((batch, 2, 8, nsd + LANES), F32)],
        compiler_params=_cparams(("arbitrary",)), name="nsa_cmp_decode",
    )(pt_flat, cache, qsw, *consts)


def _attn_decode_kernel(pt_ref, cache_ref, q_ref, bias_ref, bnew_ref, aux_ref, cq_ref, new_ref, o_ref,
                        buf_ref, sem_ref, s_ref, *, n_pages, nch, ppc, krow, vrow, kw, ncols, fox):
    ng = LANES // ncols
    tk = ppc * PAGE
    rows_s = (nch // ng) * tk
    stream = _PageStream(cache_ref, pt_ref, buf_ref, sem_ref, n_pages=n_pages, nch=nch, ppc=ppc,
                         rows=(krow, vrow), nrows=kw, look=ATTN_LOOK)
    stream.prime()
    s_ref[0:rows_s, :] = jnp.zeros((rows_s, LANES), F32)

    def scores(kt, k, bias_rows):
        s = _dot_tn(kt, q_ref[0, k])
        if fox:
            pieces = jnp.concatenate(list(_split3(bias_rows)) + [jnp.zeros(bias_rows.shape, BF16)], axis=0)
            s = s + _dot_tn(pieces, aux_ref[k]) + cq_ref[0, k, 0:1, :]
        return s

    def key_chunk(c, carry):
        slot = stream.fetch(c)
        k = c % ng
        kt = buf_ref[slot].astype(BF16)
        if fox:
            s = scores(kt, k, bias_ref[0, c])
        else:
            bpc = tk // SEL_BLOCK
            m = bias_ref[0, k, pl.ds(pl.multiple_of(c * bpc, 8), bpc), :]
            m = jnp.broadcast_to(m[:, None, :], (bpc, SEL_BLOCK, LANES)).reshape(tk, LANES)
            s = jnp.where(m > -0.5, scores(kt, k, None), NEG_INF)
        rows = pl.ds(pl.multiple_of((c // ng) * tk, tk), tk)
        s_ref[rows, :] += s
        return carry

    lax.fori_loop(0, nch, key_chunk, 0)

    kn = new_ref[0, 0:kw, :].astype(BF16)
    if fox:
        sn = scores(kn, 0, bnew_ref[0])
        tq = _lane(sn.shape) // 8
    else:
        sn = jnp.where(bnew_ref[0, 0:1, :] > -0.5, scores(kn, 0, None), NEG_INF)
        tq = _lane(sn.shape) % 8
    sn = jnp.where((_row(sn.shape) <= tq) & (_lane(sn.shape) < ncols), sn, NEG_INF)
    s_ref[rows_s:rows_s + PAGE, :] = sn

    def all_groups(v, op):
        v8 = jnp.broadcast_to(v, (8, LANES))
        out = v8
        for kk in range(1, ng):
            out = op(out, pltpu.roll(v8, kk * ncols, 1))
        return out[0:1]

    s = s_ref[...]
    ok = s > 0.5 * NEG_INF
    mx = all_groups(jnp.max(s, axis=0, keepdims=True), jnp.maximum)
    e = jnp.where(ok, jnp.exp(s - mx), 0.0)
    den = all_groups(jnp.sum(e, axis=0, keepdims=True), jnp.add)
    s_ref[...] = e * (1.0 / jnp.maximum(den, 1e-30))

    def value_chunk(c, acc):
        slot = stream.fetch(nch + c)
        p = s_ref[pl.ds(pl.multiple_of((c // ng) * tk, tk), tk), :]
        p = jnp.where(_lane(p.shape) // ncols == c % ng, p, 0.0).astype(BF16)
        return acc + _dot(buf_ref[slot].astype(BF16), p)

    acc = lax.fori_loop(0, nch, value_chunk, jnp.zeros((kw, LANES), F32))
    acc = acc + _dot(new_ref[0, kw:2 * kw, :].astype(BF16), s_ref[rows_s:rows_s + PAGE, :].astype(BF16))
    out = acc
    for kk in range(1, ng):
        out = out + pltpu.roll(acc, kk * ncols, 1)
    o_ref[0] = out


def _attn_decode(pt_flat, cache, q, bias, bnew, aux, cq, new_rows, *, batch, n_pages, krow, vrow, kw, ncols, fox,
                 name):
    ng = LANES // ncols
    ppc = min(32, n_pages // ng)
    nch = n_pages // ppc
    assert nch % ng == 0 and n_pages % ppc == 0
    tk = ppc * PAGE
    blk = lambda a: pl.BlockSpec((1,) + a.shape[1:], lambda b, pt: (b,) + (0,) * (a.ndim - 1))
    grid_spec = pltpu.PrefetchScalarGridSpec(
        num_scalar_prefetch=1, grid=(batch,),
        in_specs=[pl.BlockSpec(memory_space=pl.ANY), blk(q), blk(bias), blk(bnew),
                  pl.BlockSpec(aux.shape, lambda b, pt: (0, 0, 0)), blk(cq), blk(new_rows)],
        out_specs=pl.BlockSpec((1, kw, LANES), lambda b, pt: (b, 0, 0)),
        scratch_shapes=[pltpu.VMEM((ATTN_LOOK + 1, kw, tk), F32), pltpu.SemaphoreType.DMA((ATTN_LOOK + 1,)),
                        pltpu.VMEM(((nch // ng) * tk + PAGE, LANES), F32)])
    return pl.pallas_call(
        functools.partial(_attn_decode_kernel, n_pages=n_pages, nch=nch, ppc=ppc, krow=krow, vrow=vrow,
                          kw=kw, ncols=ncols, fox=fox),
        grid_spec=grid_spec, out_shape=jax.ShapeDtypeStruct((batch, kw, LANES), F32),
        compiler_params=_cparams(("arbitrary",)), name=name,
    )(pt_flat, cache, q, bias, bnew, aux, cq, new_rows)


def _win_decode_kernel(q_ref, st_ref, new_ref, o_ref, ns_ref, *, tdec):
    st32 = st_ref[0]
    nw32 = new_ref[0]
    lw = st32.shape[1]
    shifted = pltpu.roll(st32, lw - tdec, 1)
    tail = jnp.where(_lane(nw32.shape) >= PAGE - tdec, pltpu.roll(nw32, PAGE - tdec, 1), shifted[:, lw - PAGE:])
    ns_ref[0] = jnp.concatenate([shifted[:, 0:lw - PAGE], tail], axis=1)
    st = st32.astype(BF16)
    nw = nw32.astype(BF16)
    for g in range(2):
        q = q_ref[0, g]
        s = jnp.concatenate([_dot(q, st[0:LANES]), _dot(q, nw[0:LANES])], axis=1)
        i = _lane(s.shape)
        t = _row(s.shape) % 8
        p = _masked_softmax(s, (i > t + lw - WINDOW) & (i <= t + lw), -1).astype(BF16)
        o_ref[0, g] = _dot_nt(p[:, 0:lw], st[LANES:2 * LANES]) + _dot_nt(p[:, lw:], nw[LANES:2 * LANES])


def _win_decode(qsw, state_t, new_t, batch, tdec):
    lw = state_t.shape[2]
    return pl.pallas_call(
        functools.partial(_win_decode_kernel, tdec=tdec), grid=(batch,),
        in_specs=[pl.BlockSpec((1, 2, 32, LANES), lambda b: (b, 0, 0, 0)),
                  pl.BlockSpec((1, 256, lw), lambda b: (b, 0, 0)),
                  pl.BlockSpec((1, 256, PAGE), lambda b: (b, 0, 0))],
        out_specs=[pl.BlockSpec((1, 2, 32, LANES), lambda b: (b, 0, 0, 0)),
                   pl.BlockSpec((1, 256, lw), lambda b: (b, 0, 0))],
        out_shape=[jax.ShapeDtypeStruct((batch, 2, 32, LANES), F32), jax.ShapeDtypeStruct((batch, 256, lw), F32)],
        compiler_params=_cparams(("arbitrary",)), name="nsa_win_decode",
    )(qsw, state_t, new_t)


def _logf_decode_kernel(pt_ref, cache_ref, new_ref, u_ref, s_ref, perm_ref, tri_ref, c_ref, buf_ref, sem_ref,
                        *, n_pages):
    b = pl.program_id(0)
    nb = pl.num_programs(0)

    def copies(bb, slot):
        return [pltpu.make_async_copy(cache_ref.at[pl.ds(pt_ref[bb * n_pages + i] * FOX_HEADS, FOX_HEADS), :],
                                      buf_ref.at[slot, pl.ds(i * FOX_HEADS, FOX_HEADS), :], sem_ref.at[slot])
                for i in range(n_pages)]

    slot = b % 2

    @pl.when(b == 0)
    def _():
        for cp in copies(b, slot):
            cp.start()

    @pl.when(b + 1 < nb)
    def _():
        for cp in copies(b + 1, 1 - slot):
            cp.start()

    for cp in copies(b, slot):
        cp.wait()
    lp = jnp.concatenate([buf_ref[slot], new_ref[0]], axis=0)
    cs = _dot3(lp, u_ref[...])
    tot = _dot3(cs, s_ref[...])
    c_ref[0] = _dot3_left(perm_ref[...], cs) + _dot3_left(tri_ref[...], tot)


def _logf_decode(pt_flat, cache2d, new_rows, *, batch, n_pages):
    nr = (n_pages + 2) * FOX_HEADS
    r = np.arange(PAGE)
    u = (r[:, None] <= r[None, :]).astype(np.float32)
    last = np.zeros((PAGE, PAGE), np.float32)
    last[PAGE - 1, :] = 1.0
    rr = np.arange(nr)
    tri = ((rr[:, None] % FOX_HEADS == rr[None, :] % FOX_HEADS)
           & (rr[None, :] // FOX_HEADS < rr[:, None] // FOX_HEADS)).astype(np.float32)
    perm = np.zeros((nr, nr), np.float32)
    perm[(rr % FOX_HEADS) * (n_pages + 2) + rr // FOX_HEADS, rr] = 1.0
    consts = [jnp.asarray(u, BF16), jnp.asarray(last, BF16), jnp.asarray(perm, BF16), jnp.asarray(perm @ tri, BF16)]
    grid_spec = pltpu.PrefetchScalarGridSpec(
        num_scalar_prefetch=1, grid=(batch,),
        in_specs=[pl.BlockSpec(memory_space=pl.ANY),
                  pl.BlockSpec((1, 2 * FOX_HEADS, PAGE), lambda b, pt: (b, 0, 0))]
                 + [pl.BlockSpec(c.shape, lambda b, pt: (0, 0)) for c in consts],
        out_specs=pl.BlockSpec((1, nr, PAGE), lambda b, pt: (b, 0, 0)),
        scratch_shapes=[pltpu.VMEM((2, n_pages * FOX_HEADS, PAGE), F32), pltpu.SemaphoreType.DMA((2,))])
    return pl.pallas_call(
        functools.partial(_logf_decode_kernel, n_pages=n_pages),
        grid_spec=grid_spec, out_shape=jax.ShapeDtypeStruct((batch, nr, PAGE), F32),
        compiler_params=_cparams(("arbitrary",)), name="fox_logf_decode",
    )(pt_flat, cache2d, new_rows, *consts)


def _prep_layer_weights(w_in_t, pool_w, pool_scale, w_out_a, w_out_b, w_out_c, w_o):
    d = w_in_t.shape[1]
    tok = jnp.concatenate([w_in_t[0:256], w_in_t[512:1024], w_in_t[2328:2584]], axis=0)
    gate = jnp.concatenate([w_in_t[256:512], w_in_t[1816:2328], w_in_t[3100:6428], w_in_t[1792:1816],
                            jnp.zeros((LANES - 24, d), F32)], axis=0)
    feat = jnp.concatenate([w_in_t[1024:1792], w_in_t[2584:3100], jnp.zeros((8 - FOX_HEADS, d), F32)],
                           axis=0).astype(BF16)
    gd = pool_w.shape[-1]
    bd = jnp.zeros((POOL_WIDTH, POOL_WIDTH), F32)
    for g in range(pool_w.shape[0]):
        bd = bd.at[g * gd:(g + 1) * gd, g * gd:(g + 1) * gd].set(pool_w[g])
    return dict(w_tok=_wprep(tok), w_gate=_wprep(gate), w_feat=feat, pool_bd=bd.astype(BF16),
                pool_scale=pool_scale.reshape(1, -1),
                w_out_a=w_out_a.astype(BF16), w_out_b=w_out_b.astype(BF16), w_out_c=w_out_c.astype(BF16),
                w_o=w_o.astype(BF16))


def _to_token_last(a_t, lead):
    b, _, t = a_t.shape
    nd = len(lead)
    return a_t.reshape((b,) + lead + (t,)).transpose((0, nd + 1) + tuple(range(1, nd + 1)))


def _layer_prompt(x2d, lw, norm_w, fox_bf, final_norm, batch, seq):
    pos = jnp.arange(seq)
    (uz, q8, nsat, wint, foxt, logft, fq, kcm, ksel, vsel, kwin, vwin, fk, fv) = _project(
        x2d, norm_w, lw["w_tok"], lw["w_feat"], fox_bf, pos, prompt=True, batch=batch, seq=seq)
    d = _pool_prompt(uz, seq)
    bpt = TKV // CMP_BLOCK
    nb = seq // CMP_BLOCK
    means = kcm[..., :bpt].transpose(0, 1, 3, 2).reshape(batch, nb, 256)
    means = jnp.concatenate([means[:, 0::2], means[:, 1::2]], axis=1)
    dup = lambda m: jnp.stack([jnp.concatenate([m[..., g * HEAD_DIM:(g + 1) * HEAD_DIM]] * 2, axis=-1)
                               for g in range(2)], axis=1).astype(BF16)
    ocmp, qaug = _cmp_prompt(dup(means[..., 0:LANES]), dup(means[..., LANES:2 * LANES]), q8, batch, seq)
    osel = _flash(qaug, ksel, vsel, n_sub=2, n_stack=NSA_REP, tpt=LANES, name="nsa_sel_prompt")
    owin = _win_prompt(q8, kwin, vwin, batch, seq)
    tpt = 512
    oc = _flash(fq.reshape(batch, FOX_HEADS, seq // tpt, tpt, LANES), fk, fv,
                n_sub=2, n_stack=1, tpt=tpt, name="fox_prompt")
    n = batch * seq
    xo, y = _output(x2d, norm_w, d, ocmp.reshape(n, 512), osel.reshape(n, 512), owin.reshape(n, 512),
                    oc.reshape(n, 256), lw, final_norm)
    wl = min(WINDOW, seq)
    states = (_to_token_last(nsat, (4, NSA_KV_HEADS, HEAD_DIM)),
              _to_token_last(foxt, (2, FOX_HEADS, HEAD_DIM)),
              logft[:, :FOX_HEADS].transpose(0, 2, 1),
              _to_token_last(wint[:, :, seq - wl:], (2, NSA_KV_HEADS, HEAD_DIM)),
              uz[:, :POOL_WIDTH].reshape(batch, seq, POOL_WIDTH)[:, -POOL_STATE:])
    return xo, y, states


def _new_page(a_t, batch, tdec):
    f = a_t.shape[0]
    a = a_t.reshape(f, batch, tdec).transpose(1, 0, 2)
    return jnp.concatenate([a, jnp.zeros((batch, f, PAGE - tdec), a.dtype)], axis=2)


def _layer_decode(x2d, lw, norm_w, fox_bf, final_norm, pt_flat, nsa_cache, fox_cache, logf_cache,
                  state_win_t, state_pool_tb, batch, tdec, n_pages):
    past = n_pages * PAGE
    n = batch * tdec
    pos = past + jnp.tile(jnp.arange(tdec), batch)
    (uz, q8, nsat, wint, foxt, logft, qc) = _project(
        x2d, norm_w, lw["w_tok"], lw["w_feat"], fox_bf, pos, prompt=False, batch=batch, seq=tdec)
    nsat, wint, foxt, logft = nsat[0], wint[0], foxt[0], logft[0]

    u_tb = uz[:, :POOL_WIDTH].reshape(batch, tdec, POOL_WIDTH).transpose(1, 0, 2)
    pool_seq = jnp.concatenate([state_pool_tb, u_tb], axis=0)
    d = _pool_decode(pool_seq, past, tdec).transpose(1, 0, 2).reshape(n, POOL_WIDTH)

    q5 = q8[0].reshape(NSA_KV_HEADS, NSA_REP, batch, tdec, LANES)
    qs = q5.transpose(2, 0, 1, 3, 4)
    qs = jnp.concatenate([qs, jnp.zeros((batch, 2, NSA_REP, 8 - tdec, LANES), BF16)], axis=3)
    qs = qs.reshape(batch, 2, NSA_REP * 8, LANES)
    qsw = jnp.stack([qs[:, 0], jnp.roll(qs[:, 1], HALF, axis=-1)], axis=1)
    ocmp_k, val = _cmp_decode(pt_flat, nsa_cache, qsw, batch=batch, n_pages=n_pages, past=past)
    selm = _topk_decode(val.reshape(batch * 16, -1), min(SEL_TOPK, past // SEL_BLOCK + 1)).reshape(val.shape)

    def unstack_groups(o):
        o = o.reshape(batch, 2, NSA_REP, 8, 2, HEAD_DIM)[:, :, :, :tdec]
        o = jnp.stack([o[:, 0, :, :, 0], o[:, 1, :, :, 1]], axis=1)
        return o.transpose(0, 3, 1, 2, 4).reshape(n, NSA_HEADS * HEAD_DIM)

    ocmp = unstack_groups(ocmp_k)
    owin_k, win_next_t = _win_decode(qsw, state_win_t, _new_page(wint, batch, tdec), batch, tdec)
    owin = unstack_groups(owin_k)

    nsd = past // SEL_BLOCK
    q6 = q5[..., :HEAD_DIM].transpose(2, 1, 0, 3, 4)
    q6 = jnp.concatenate([q6, jnp.zeros((batch, NSA_REP, 2, 8 - tdec, HEAD_DIM), BF16)], axis=3)
    eye2 = jnp.eye(2, dtype=BF16)
    qbd = q6[:, :, :, :, None, :] * eye2[None, None, :, None, :, None]
    qbd_t = qbd.reshape(batch, 64, LANES).transpose(0, 2, 1)
    qbd_t = jnp.concatenate([qbd_t, jnp.zeros((batch, LANES, LANES - 64), BF16)], axis=2)
    sel_t = jnp.tile(selm.reshape(batch, 16, nsd + LANES).transpose(0, 2, 1), (1, 1, NSA_REP))
    sel_t = jnp.concatenate([sel_t, jnp.zeros((batch, nsd + LANES, LANES - 64), F32)], axis=2)
    groups = lambda a, ng: jnp.stack([jnp.roll(a, k * (LANES // ng), axis=-1) for k in range(ng)], axis=1)
    osel_k = _attn_decode(pt_flat, nsa_cache, groups(qbd_t, 2), groups(sel_t[:, :nsd], 2), sel_t[:, nsd:nsd + 8],
                          jnp.zeros((2, 32, LANES), BF16), jnp.zeros((batch, 2, 8, LANES), F32),
                          _new_page(nsat[256:512], batch, tdec),
                          batch=batch, n_pages=n_pages, krow=256, vrow=384, kw=LANES, ncols=64, fox=False,
                          name="nsa_sel_decode")
    os_ = osel_k[:, :, :64].reshape(batch, 2, HEAD_DIM, NSA_REP, 2, 8)[..., :tdec]
    osel = jnp.stack([os_[:, 0, :, :, 0], os_[:, 1, :, :, 1]], axis=1)
    osel = osel.transpose(0, 4, 1, 3, 2).reshape(n, 512)

    lnew = _new_page(logft, batch, tdec)
    lnew = jnp.concatenate([lnew[:, :FOX_HEADS], jnp.zeros((batch, FOX_HEADS, PAGE), F32)], axis=1)
    csum = _logf_decode(pt_flat, logf_cache, lnew, batch=batch, n_pages=n_pages)
    c4 = csum.reshape(batch, FOX_HEADS, (n_pages + 2) * PAGE)
    ck8 = jnp.concatenate([-c4, jnp.zeros_like(c4)], axis=1)
    cq = jnp.concatenate([c4[:, :, past:past + tdec].transpose(0, 2, 1),
                          jnp.zeros((batch, tdec, 8 - FOX_HEADS), F32)], axis=2).reshape(batch, 1, tdec * 8)
    cq = jnp.concatenate([cq, jnp.zeros((batch, 7, tdec * 8), F32)], axis=1)
    cq = jnp.concatenate([cq, jnp.zeros((batch, 8, LANES - tdec * 8), F32)], axis=2)
    qf = (qc * SCALE).reshape(batch, tdec, FOX_HEADS, HEAD_DIM)
    eye = jnp.eye(8, FOX_HEADS, dtype=F32)
    qfbd = (qf[:, :, None, :, :] * eye[None, None, :, :, None]).reshape(batch, tdec * 8, FOX_HEADS * HEAD_DIM)
    qf_t = jnp.concatenate([qfbd.transpose(0, 2, 1), jnp.zeros((batch, 256, LANES - tdec * 8), F32)],
                           axis=2).astype(BF16)
    col = np.arange(LANES)
    head_cols = np.zeros((32, LANES), np.float32)
    for p in range(3):
        head_cols[p * 8 + col[:tdec * 8] % 8, col[:tdec * 8]] = 1.0
    fng = LANES // (tdec * 8)
    fppc = min(32, n_pages // fng)
    ck_chunks = ck8[:, :, :past].reshape(batch, 8, n_pages // fppc, fppc * PAGE).transpose(0, 2, 1, 3)
    oc_k = _attn_decode(pt_flat, fox_cache, groups(qf_t, fng), ck_chunks, ck8[:, :, past:past + PAGE],
                        groups(jnp.asarray(head_cols, BF16)[None], fng)[0], groups(cq, fng),
                        _new_page(foxt, batch, tdec),
                        batch=batch, n_pages=n_pages, krow=0, vrow=256, kw=256, ncols=tdec * 8, fox=True,
                        name="fox_decode")
    o6 = oc_k[:, :, :tdec * 8].reshape(batch, FOX_HEADS, HEAD_DIM, tdec, 8)
    oc = jnp.stack([o6[:, h, :, :, h] for h in range(FOX_HEADS)], axis=1)
    oc = oc.transpose(0, 3, 1, 2).reshape(n, 256)

    xo, y = _output(x2d, norm_w, d, ocmp, osel, owin, oc, lw, final_norm)
    states = (nsat.T.reshape(batch, tdec, 4, NSA_KV_HEADS, HEAD_DIM),
              foxt.T.reshape(batch, tdec, 2, FOX_HEADS, HEAD_DIM),
              logft[:FOX_HEADS].T.reshape(batch, tdec, FOX_HEADS),
              _to_token_last(win_next_t, (2, NSA_KV_HEADS, HEAD_DIM)),
              pool_seq[-POOL_STATE:].transpose(1, 0, 2))
    return xo, y, states


def kernel(x_prompt, x_sample, cache_nsa_kv, cache_fox_kv, cache_fox_logf, state_win_kv, state_pool, page_table,
           norm_w, w_in, pool_w, pool_scale, fox_bf, w_out_a, w_out_b, w_out_c, w_o, final_norm):
    depth = w_in.shape[0]
    batch, seq, dm = x_prompt.shape
    dbatch, tdec, _ = x_sample.shape
    n_phys, page = cache_nsa_kv.shape[1], cache_nsa_kv.shape[2]
    n_pages = page_table.shape[1]
    assert page == PAGE and tdec == 4 and n_pages % min(32, n_pages) == 0
    assert state_win_kv.shape[2] == WINDOW and n_pages * PAGE >= WINDOW

    nsa_cache = cache_nsa_kv.transpose(0, 1, 3, 4, 5, 2).reshape(depth * n_phys, 512, PAGE)
    fox_cache = cache_fox_kv.transpose(0, 1, 3, 4, 5, 2).reshape(depth * n_phys, 512, PAGE)
    logf_cache = cache_fox_logf.transpose(0, 1, 3, 2).reshape(depth * n_phys * FOX_HEADS, PAGE)
    win_state_t = state_win_kv.transpose(0, 1, 3, 4, 5, 2).reshape(depth, dbatch, 256, WINDOW)
    pool_state_tb = state_pool.transpose(0, 2, 1, 3)
    w_in_t = w_in.transpose(0, 2, 1)

    hp = x_prompt.reshape(batch * seq, dm)
    hs = x_sample.reshape(dbatch * tdec, dm)
    st_p, st_s = [], []
    for layer in range(depth):
        lw = _prep_layer_weights(w_in_t[layer], pool_w[layer], pool_scale[layer], w_out_a[layer],
                                 w_out_b[layer], w_out_c[layer], w_o[layer])
        hp, yp, sp = _layer_prompt(hp, lw, norm_w[layer], fox_bf[layer], final_norm, batch, seq)
        pt_flat = (page_table + layer * n_phys).reshape(-1).astype(jnp.int32)
        hs, ys, ss = _layer_decode(hs, lw, norm_w[layer], fox_bf[layer], final_norm, pt_flat, nsa_cache, fox_cache,
                                   logf_cache, win_state_t[layer], pool_state_tb[layer], dbatch, tdec, n_pages)
        st_p.append(sp)
        st_s.append(ss)
    outs = [yp.reshape(batch, seq, dm), ys.reshape(dbatch, tdec, dm)]
    for k in range(5):
        outs.append(jnp.stack([s[k] for s in st_p], 0))
        outs.append(jnp.stack([s[k] for s in st_s], 0))
    return tuple(outs)
```
